```python
import math
import jax, jax.numpy as jnp
from jax import lax
import numpy as np

D_MODEL = 1024
BATCH = 32
SEQ = 256
DEPTH = 4
DEC_BATCH = 4
DEC_SEQ = 4096
PAST_LEN = 256

GRID_W = 64
SSD_HEADS = 8
SSD_HEAD_DIM = 64
SSD_WIDTH = 512
SSD_GROUPS = 2
SSD_STATE = 128
SSD_BC = SSD_GROUPS * SSD_STATE
SSD_CONV = 5
SSD_CONV_DIM = SSD_WIDTH + 2 * SSD_BC
SSD_CHUNK = 128
GLA_HEADS = 4
GLA_DK = 32
GLA_DV = 64
GLA_KEY_DIM = GLA_HEADS * GLA_DK
GLA_WIDTH = GLA_HEADS * GLA_DV
GLA_LOWRANK = 16
GLA_GATE_NORM = 16.0
GLA_CHUNK = 64
ATT_HEADS = 4
ATT_KV_HEADS = 2
ATT_GROUP = ATT_HEADS // ATT_KV_HEADS
HEAD_DIM = 64
ATT_WIDTH = ATT_HEADS * HEAD_DIM
ATT_KV_DIM = ATT_KV_HEADS * HEAD_DIM
WINDOW = 128
ATT_BLOCK = 128
ROPE_THETA = 10000.0
D_MIX = SSD_WIDTH + GLA_WIDTH + ATT_WIDTH
IN_SIZES = (SSD_WIDTH, SSD_CONV_DIM, 2 * SSD_HEADS,
            GLA_KEY_DIM, GLA_KEY_DIM, GLA_WIDTH, GLA_WIDTH, 2 * GLA_LOWRANK,
            ATT_WIDTH, ATT_KV_DIM, ATT_KV_DIM, ATT_WIDTH)
D_IN = (SSD_WIDTH + SSD_CONV_DIM + 2 * SSD_HEADS
        + 2 * GLA_KEY_DIM + 2 * GLA_WIDTH + 2 * GLA_LOWRANK
        + 2 * ATT_WIDTH + 2 * ATT_KV_DIM)
EPS = 1e-6
NEG_INF = -1e30

kernel_name = "hybrid_ssd_gla_swa_prefix_dit_step"


def rms_norm(x, w):
    xf = x.astype(jnp.float32)
    y = xf * lax.rsqrt(jnp.mean(xf * xf, axis=-1, keepdims=True) + EPS)
    return (y * w.astype(jnp.float32)).astype(x.dtype)


def flip(t):
    return jnp.flip(t, axis=1)


def split_proj(p):
    parts, start = [], 0
    for size in IN_SIZES:
        parts.append(p[..., start:start + size])
        start += size
    return parts


def modulation(cond, w_ada, b_ada):
    mod = jax.nn.silu(cond) @ w_ada + b_ada
    return mod[..., :D_MODEL], mod[..., D_MODEL:2 * D_MODEL], mod[..., 2 * D_MODEL:]


def depthwise_conv(u, w, b):
    ch = u.shape[-1]
    pad = (SSD_CONV - 1) // 2
    out = lax.conv_general_dilated(u, w[:, None, :].astype(u.dtype), window_strides=(1,),
                                   padding=[(pad, pad)], dimension_numbers=('NWC', 'WIO', 'NWC'),
                                   feature_group_count=ch)
    return out + b


def ssd_scan(x, a, bm, cm, h0):
    b, l, h, p = x.shape
    n = bm.shape[-1]
    nc = l // SSD_CHUNK
    x = x.reshape(b, nc, SSD_CHUNK, h, p)
    a = a.reshape(b, nc, SSD_CHUNK, h)
    bm = bm.reshape(b, nc, SSD_CHUNK, h, n)
    cm = cm.reshape(b, nc, SSD_CHUNK, h, n)
    a_cs = jnp.cumsum(a, axis=2)
    tri = jnp.tril(jnp.ones((SSD_CHUNK, SSD_CHUNK), dtype=bool))[None, None, :, :, None]
    seg = a_cs[:, :, :, None, :] - a_cs[:, :, None, :, :]
    decay = jnp.where(tri, jnp.exp(jnp.where(tri, seg, 0.0)), 0.0)
    scores = jnp.einsum('bcqhn,bcshn->bcqsh', cm, bm) * decay
    y_diag = jnp.einsum('bcqsh,bcshp->bcqhp', scores, x)
    a_last = a_cs[:, :, -1]
    states = jnp.einsum('bcshn,bcsh,bcshp->bchpn', bm, jnp.exp(a_last[:, :, None] - a_cs), x)

    def step(carry, inp):
        st, dec = inp
        return carry * dec[:, :, None, None] + st, carry

    h_fin, h_enter = lax.scan(step, h0, (jnp.moveaxis(states, 1, 0), jnp.moveaxis(jnp.exp(a_last), 1, 0)))
    h_enter = jnp.moveaxis(h_enter, 0, 1)
    y_off = jnp.einsum('bcqhn,bchpn,bcqh->bcqhp', cm, h_enter, jnp.exp(a_cs))
    return (y_diag + y_off).reshape(b, l, h, p), h_fin


def gla_scan(q, k, v, g, s0):
    b, l, h, dk = q.shape
    dv = v.shape[-1]
    nc = l // GLA_CHUNK
    q = q.reshape(b, nc, GLA_CHUNK, h, dk)
    k = k.reshape(b, nc, GLA_CHUNK, h, dk)
    v = v.reshape(b, nc, GLA_CHUNK, h, dv)
    g = g.reshape(b, nc, GLA_CHUNK, h, dk)
    gc = jnp.cumsum(g, axis=2)
    g_last = gc[:, :, -1]
    q_in = q * jnp.exp(gc)
    k_in = k * jnp.exp(-gc)
    tri = jnp.tril(jnp.ones((GLA_CHUNK, GLA_CHUNK), dtype=bool))[None, None, None]
    att = jnp.where(tri, jnp.einsum('bcqhd,bcshd->bchqs', q_in, k_in), 0.0)
    o_intra = jnp.einsum('bchqs,bcshv->bcqhv', att, v)
    states = jnp.einsum('bcshd,bcshv->bchdv', k * jnp.exp(g_last[:, :, None] - gc), v)

    def step(carry, inp):
        st, dec = inp
        return carry * dec[..., None] + st, carry

    s_fin, s_enter = lax.scan(step, s0, (jnp.moveaxis(states, 1, 0), jnp.moveaxis(jnp.exp(g_last), 1, 0)))
    s_enter = jnp.moveaxis(s_enter, 0, 1)
    o_inter = jnp.einsum('bcqhd,bchdv->bcqhv', q_in, s_enter)
    return (o_intra + o_inter).reshape(b, l, h, dv), s_fin


def ssd_branch(z, xbc, dt, conv_w, conv_b, a_log, dt_bias, d_skip, norm_w, h0):
    b, l, _ = xbc.shape
    xbc = jax.nn.silu(depthwise_conv(xbc, conv_w, conv_b)).astype(jnp.float32)
    xs = xbc[..., :SSD_WIDTH].reshape(b, l, SSD_HEADS, SSD_HEAD_DIM)
    rep = SSD_HEADS // SSD_GROUPS
    bm = jnp.repeat(xbc[..., SSD_WIDTH:SSD_WIDTH + SSD_BC].reshape(b, l, SSD_GROUPS, SSD_STATE), rep, axis=2)
    cm = jnp.repeat(xbc[..., SSD_WIDTH + SSD_BC:].reshape(b, l, SSD_GROUPS, SSD_STATE), rep, axis=2)
    dt = jax.nn.softplus(dt.reshape(b, l, 2, SSD_HEADS).astype(jnp.float32) + dt_bias.astype(jnp.float32))
    a = -jnp.exp(a_log.astype(jnp.float32))
    y_f, s_f = ssd_scan(xs * dt[:, :, 0, :, None], dt[:, :, 0] * a[0], bm, cm, h0[:, 0])
    y_b, s_b = ssd_scan(flip(xs * dt[:, :, 1, :, None]), flip(dt[:, :, 1] * a[1]), flip(bm), flip(cm), h0[:, 1])
    y = y_f + flip(y_b) + xs * d_skip.astype(jnp.float32)[:, None]
    y = y.reshape(b, l, SSD_WIDTH).astype(z.dtype)
    y = rms_norm(y * jax.nn.silu(z), norm_w)
    return y, jnp.stack([s_f, s_b], axis=1)


def gla_branch(q, k, v, g, gk_lr, gk_up, gk_b, norm_w, s0):
    b, l, _ = q.shape
    qh = q.reshape(b, l, GLA_HEADS, GLA_DK).astype(jnp.float32) * (GLA_DK ** -0.5)
    kh = k.reshape(b, l, GLA_HEADS, GLA_DK).astype(jnp.float32)
    vh = v.reshape(b, l, GLA_HEADS, GLA_DV).astype(jnp.float32)
    lr = gk_lr.reshape(b, l, 2, GLA_LOWRANK)
    gk = jnp.einsum('blzr,zrk->blzk', lr, gk_up) + gk_b
    gk = (jax.nn.log_sigmoid(gk.astype(jnp.float32)) / GLA_GATE_NORM).reshape(b, l, 2, GLA_HEADS, GLA_DK)
    o_f, s_f = gla_scan(qh, kh, vh, gk[:, :, 0], s0[:, 0])
    o_b, s_b = gla_scan(flip(qh), flip(kh), flip(vh), flip(gk[:, :, 1]), s0[:, 1])
    o = rms_norm(o_f + flip(o_b), norm_w).reshape(b, l, GLA_WIDTH).astype(g.dtype)
    return o * jax.nn.silu(g), jnp.stack([s_f, s_b], axis=1)


def attn_heads(aq, ak, av, q_norm, k_norm):
    b, l, _ = aq.shape
    qh = rms_norm(aq.reshape(b, l, ATT_KV_HEADS, ATT_GROUP, HEAD_DIM), q_norm)
    kh = rms_norm(ak.reshape(b, l, ATT_KV_HEADS, HEAD_DIM), k_norm)
    vh = av.reshape(b, l, ATT_KV_HEADS, HEAD_DIM)
    return qh, kh, vh


def axial_rope(x, row_pos, col_pos):
    half = HEAD_DIM // 2
    quarter = HEAD_DIM // 4
    inv = ROPE_THETA ** (-jnp.arange(quarter, dtype=jnp.float32) / quarter)
    bshape = (x.shape[1],) + (1,) * (x.ndim - 3) + (quarter,)

    def rot(xa, pos):
        ang = pos[:, None] * inv[None, :]
        cos = jnp.cos(ang).reshape(bshape).astype(x.dtype)
        sin = jnp.sin(ang).reshape(bshape).astype(x.dtype)
        x1, x2 = xa[..., :quarter], xa[..., quarter:]
        return jnp.concatenate([x1 * cos - x2 * sin, x1 * sin + x2 * cos], axis=-1)

    return jnp.concatenate([rot(x[..., :half], row_pos), rot(x[..., half:], col_pos)], axis=-1)


def sink_logits(sink, lead):
    s = sink.astype(jnp.float32).reshape(ATT_KV_HEADS, ATT_GROUP)
    return jnp.broadcast_to(s[:, :, None, None], lead + (ATT_KV_HEADS, ATT_GROUP, ATT_BLOCK, 1))


def ctx_attention(q, k, v, sink):
    b, l = q.shape[:2]
    nb = l // ATT_BLOCK
    qb = jnp.moveaxis(q.reshape(b, nb, ATT_BLOCK, ATT_KV_HEADS, ATT_GROUP, HEAD_DIM), 1, 0)
    scale = HEAD_DIM ** -0.5

    def one_block(qblk):
        s = jnp.einsum('bqkgd,bskd->bkgqs', qblk, k).astype(jnp.float32) * scale
        p = jax.nn.softmax(jnp.concatenate([sink_logits(sink, (b,)), s], axis=-1), axis=-1)
        return jnp.einsum('bkgqs,bskd->bqkgd', p[..., 1:].astype(v.dtype), v)

    out = lax.map(one_block, qb)
    return jnp.moveaxis(out, 0, 1).reshape(b, l, ATT_WIDTH)


def latent_attention(q, k, v, k_ctx, v_ctx, sink):
    b, l = q.shape[:2]
    nb = l // ATT_BLOCK
    scale = HEAD_DIM ** -0.5
    qb = q.reshape(b, nb, ATT_BLOCK, ATT_KV_HEADS, ATT_GROUP, HEAD_DIM)

    def windows(t):
        tb = t.reshape(b, nb, ATT_BLOCK, ATT_KV_HEADS, HEAD_DIM)
        tp = jnp.pad(tb, ((0, 0), (1, 1), (0, 0), (0, 0), (0, 0)))
        return jnp.concatenate([tp[:, :-2], tp[:, 1:-1], tp[:, 2:]], axis=2)

    k_win, v_win = windows(k), windows(v)
    q_pos = jnp.arange(l).reshape(nb, ATT_BLOCK)
    k_pos = (jnp.arange(nb)[:, None] - 1) * ATT_BLOCK + jnp.arange(3 * ATT_BLOCK)[None, :]
    kp = k_pos[:, None, :]
    valid = (jnp.abs(q_pos[:, :, None] - kp) <= WINDOW) & (kp >= 0) & (kp < l)
    s_loc = jnp.einsum('bnqkgd,bnskd->bnkgqs', qb, k_win).astype(jnp.float32) * scale
    s_loc = jnp.where(valid[None, :, None, None], s_loc, NEG_INF)
    s_ctx = jnp.einsum('bnqkgd,bskd->bnkgqs', qb, k_ctx).astype(jnp.float32) * scale
    p = jax.nn.softmax(jnp.concatenate([sink_logits(sink, (b, nb)), s_ctx, s_loc], axis=-1), axis=-1)
    lc = k_ctx.shape[1]
    o = (jnp.einsum('bnkgqs,bskd->bnqkgd', p[..., 1:1 + lc].astype(v.dtype), v_ctx)
         + jnp.einsum('bnkgqs,bnskd->bnqkgd', p[..., 1 + lc:].astype(v.dtype), v_win))
    return o.reshape(b, l, ATT_WIDTH)


def context_layer(x, c_ctx, lw):
    (w_ada, b_ada, norm_w, w_in, conv_w, conv_b, a_log, dt_bias, d_skip, ssd_norm_w,
     gk_up, gk_b, gla_norm_w, q_norm, k_norm, sink, w_out) = lw
    b = x.shape[0]
    shift, scale, gate = modulation(c_ctx, w_ada, b_ada)
    h = rms_norm(x, norm_w) * (1 + scale) + shift
    z, xbc, dt, gq, gk, gv, gg, glr, aq, ak, av, ag = split_proj(h @ w_in)
    h0 = jnp.zeros((b, 2, SSD_HEADS, SSD_HEAD_DIM, SSD_STATE), jnp.float32)
    s0 = jnp.zeros((b, 2, GLA_HEADS, GLA_DK, GLA_DV), jnp.float32)
    y_ssd, st_ssd = ssd_branch(z, xbc, dt, conv_w, conv_b, a_log, dt_bias, d_skip, ssd_norm_w, h0)
    y_gla, st_gla = gla_branch(gq, gk, gv, gg, glr, gk_up, gk_b, gla_norm_w, s0)
    qh, kh, vh = attn_heads(aq, ak, av, q_norm, k_norm)
    y_att = ctx_attention(qh, kh, vh, sink) * jax.nn.silu(ag)
    y = jnp.concatenate([y_ssd, y_gla, y_att], axis=-1) @ w_out
    return x + gate * y, kh, vh, st_ssd, st_gla


def latent_layer(x, c, row_pos, col_pos, k_ctx, v_ctx, ssd_h0, gla_s0, lw):
    (w_ada, b_ada, norm_w, w_in, conv_w, conv_b, a_log, dt_bias, d_skip, ssd_norm_w,
     gk_up, gk_b, gla_norm_w, q_norm, k_norm, sink, w_out) = lw
    shift, scale, gate = modulation(c, w_ada, b_ada)
    h = rms_norm(x, norm_w) * (1 + scale[:, None]) + shift[:, None]
    z, xbc, dt, gq, gk, gv, gg, glr, aq, ak, av, ag = split_proj(h @ w_in)
    y_ssd, _ = ssd_branch(z, xbc, dt, conv_w, conv_b, a_log, dt_bias, d_skip, ssd_norm_w,
                          ssd_h0.astype(jnp.float32))
    y_gla, _ = gla_branch(gq, gk, gv, gg, glr, gk_up, gk_b, gla_norm_w, gla_s0.astype(jnp.float32))
    qh, kh, vh = attn_heads(aq, ak, av, q_norm, k_norm)
    qh = axial_rope(qh, row_pos, col_pos)
    kh = axial_rope(kh, row_pos, col_pos)
    y_att = latent_attention(qh, kh, vh, k_ctx, v_ctx, sink) * jax.nn.silu(ag)
    y = jnp.concatenate([y_ssd, y_gla, y_att], axis=-1) @ w_out
    return x + gate[:, None] * y


def setup_inputs(seed: int = 0) -> dict:
    key = jax.random.key(seed)
    ks = jax.random.split(key, 26)

    def nrm(k, shape, s):
        return jax.random.normal(k, shape, jnp.float32) * s

    x_prompt = nrm(ks[0], (BATCH, SEQ, D_MODEL), 1.0)
    x_sample = nrm(ks[1], (DEC_BATCH, DEC_SEQ, D_MODEL), 1.0)
    c = nrm(ks[2], (DEC_BATCH, D_MODEL), 1.0)
    cache_k = nrm(ks[3], (DEC_BATCH, DEPTH, PAST_LEN, ATT_KV_HEADS, HEAD_DIM), 1.0)
    cache_v = nrm(ks[4], (DEC_BATCH, DEPTH, PAST_LEN, ATT_KV_HEADS, HEAD_DIM), 1.0)
    state_ssd = nrm(ks[5], (DEC_BATCH, DEPTH, 2, SSD_HEADS, SSD_HEAD_DIM, SSD_STATE), 0.1)
    state_gla = nrm(ks[6], (DEC_BATCH, DEPTH, 2, GLA_HEADS, GLA_DK, GLA_DV), 0.1)
    c_ctx = nrm(ks[7], (D_MODEL,), 1.0)
    w_ada = nrm(ks[8], (DEPTH, D_MODEL, 3 * D_MODEL), 0.5 * D_MODEL ** -0.5)
    b_ada = nrm(ks[9], (DEPTH, 3 * D_MODEL), 0.02)
    norm_w = 1.0 + nrm(ks[10], (DEPTH, D_MODEL), 0.01)
    w_in = nrm(ks[11], (DEPTH, D_MODEL, D_IN), D_MODEL ** -0.5)
    conv_w = nrm(ks[12], (DEPTH, SSD_CONV, SSD_CONV_DIM), SSD_CONV ** -0.5)
    conv_b = nrm(ks[13], (DEPTH, SSD_CONV_DIM), 0.02)
    ssd_a_log = jnp.log(jax.random.uniform(ks[14], (DEPTH, 2, SSD_HEADS), jnp.float32, 1.0, 16.0))
    dt0 = jnp.exp(jax.random.uniform(ks[15], (DEPTH, 2, SSD_HEADS), jnp.float32,
                                     math.log(1e-3), math.log(1e-1)))
    ssd_dt_bias = dt0 + jnp.log(-jnp.expm1(-dt0))
    ssd_d = 1.0 + nrm(ks[16], (DEPTH, SSD_HEADS), 0.1)
    ssd_norm_w = 1.0 + nrm(ks[17], (DEPTH, SSD_WIDTH), 0.01)
    gla_gk_up = nrm(ks[18], (DEPTH, 2, GLA_LOWRANK, GLA_KEY_DIM), GLA_LOWRANK ** -0.5)
    gla_gk_b = nrm(ks[19], (DEPTH, 2, GLA_KEY_DIM), 0.1)
    gla_norm_w = 1.0 + nrm(ks[20], (DEPTH, GLA_DV), 0.01)
    attn_q_norm = 1.0 + nrm(ks[21], (DEPTH, HEAD_DIM), 0.01)
    attn_k_norm = 1.0 + nrm(ks[22], (DEPTH, HEAD_DIM), 0.01)
    attn_sink = nrm(ks[23], (DEPTH, ATT_HEADS), 1.0)
    w_out = nrm(ks[24], (DEPTH, D_MIX, D_MODEL), D_MIX ** -0.5)
    return {"x_prompt": x_prompt, "x_sample": x_sample, "c": c,
            "cache_k": cache_k, "cache_v": cache_v, "state_ssd": state_ssd, "state_gla": state_gla,
            "c_ctx": c_ctx, "w_ada": w_ada, "b_ada": b_ada, "norm_w": norm_w, "w_in": w_in,
            "conv_w": conv_w, "conv_b": conv_b, "ssd_a_log": ssd_a_log, "ssd_dt_bias": ssd_dt_bias,
            "ssd_d": ssd_d, "ssd_norm_w": ssd_norm_w, "gla_gk_up": gla_gk_up, "gla_gk_b": gla_gk_b,
            "gla_norm_w": gla_norm_w, "attn_q_norm": attn_q_norm, "attn_k_norm": attn_k_norm,
            "attn_sink": attn_sink, "w_out": w_out}


def reference(x_prompt, x_sample, c, cache_k, cache_v, state_ssd, state_gla, c_ctx, w_ada, b_ada,
              norm_w, w_in, conv_w, conv_b, ssd_a_log, ssd_dt_bias, ssd_d, ssd_norm_w, gla_gk_up,
              gla_gk_b, gla_norm_w, attn_q_norm, attn_k_norm, attn_sink, w_out):
    rows = x_sample.shape[1] // GRID_W
    row_pos = jnp.repeat(jnp.arange(rows, dtype=jnp.float32), GRID_W)
    col_pos = jnp.tile(jnp.arange(GRID_W, dtype=jnp.float32), rows)
    x_ctx, x_lat = x_prompt, x_sample
    ks_out, vs_out, ssd_out, gla_out = [], [], [], []
    for l in range(DEPTH):
        lw = (w_ada[l], b_ada[l], norm_w[l], w_in[l], conv_w[l], conv_b[l], ssd_a_log[l], ssd_dt_bias[l],
              ssd_d[l], ssd_norm_w[l], gla_gk_up[l], gla_gk_b[l], gla_norm_w[l], attn_q_norm[l],
              attn_k_norm[l], attn_sink[l], w_out[l])
        x_ctx, k_l, v_l, s_ssd_l, s_gla_l = context_layer(x_ctx, c_ctx, lw)
        ks_out.append(k_l)
        vs_out.append(v_l)
        ssd_out.append(s_ssd_l)
        gla_out.append(s_gla_l)
        x_lat = latent_layer(x_lat, c, row_pos, col_pos, cache_k[:, l], cache_v[:, l],
                             state_ssd[:, l], state_gla[:, l], lw)
    new_cache_k = jnp.stack(ks_out, axis=1)
    new_cache_v = jnp.stack(vs_out, axis=1)
    new_state_ssd = jnp.stack(ssd_out, axis=1)
    new_state_gla = jnp.stack(gla_out, axis=1)
    return (x_ctx, x_lat, new_cache_k, new_cache_v, new_state_ssd, new_state_gla)
```

```python
import functools
import math

import jax
import jax.numpy as jnp
from jax import lax
from jax.experimental import pallas as pl
from jax.experimental.pallas import tpu as pltpu

D_MODEL = 1024
GRID_W = 64
SSD_HEADS = 8
SSD_HEAD_DIM = 64
SSD_WIDTH = 512
SSD_GROUPS = 2
SSD_STATE = 128
SSD_BC = SSD_GROUPS * SSD_STATE
SSD_CONV = 5
SSD_CONV_DIM = SSD_WIDTH + 2 * SSD_BC
SSD_CHUNK = 128
GLA_HEADS = 4
GLA_DK = 32
GLA_DV = 64
GLA_KEY_DIM = GLA_HEADS * GLA_DK
GLA_WIDTH = GLA_HEADS * GLA_DV
GLA_LOWRANK = 16
GLA_GATE_NORM = 16.0
GLA_CHUNK = 64
ATT_HEADS = 4
ATT_KV_HEADS = 2
ATT_GROUP = ATT_HEADS // ATT_KV_HEADS
HEAD_DIM = 64
ATT_WIDTH = ATT_HEADS * HEAD_DIM
ATT_KV_DIM = ATT_KV_HEADS * HEAD_DIM
WINDOW = 128
ATT_BLOCK = 128
ROPE_THETA = 10000.0
D_MIX = SSD_WIDTH + GLA_WIDTH + ATT_WIDTH
EPS = 1e-6
NEG_INF = -1e30

LANES = 128
SUBLANES = 8
VMEM_LIMIT_BYTES = 56 * 1024 * 1024

C_XBC = 0
C_Z = C_XBC + SSD_CONV_DIM
C_DT = C_Z + SSD_WIDTH
C_GQ = C_DT + LANES
C_GK = C_GQ + GLA_KEY_DIM
C_GV = C_GK + GLA_KEY_DIM
C_GG = C_GV + GLA_WIDTH
C_GLR = C_GG + GLA_WIDTH
C_AQ = C_GLR + LANES
C_AK = C_AQ + ATT_WIDTH
C_AV = C_AK + ATT_KV_DIM
C_AG = C_AV + ATT_KV_DIM
D_PROJ = C_AG + ATT_WIDTH
BWD_COLS = C_AQ

TILE = 256
IN_TILE = 512
HALO = SUBLANES

HI = lax.Precision.HIGHEST
F32 = jnp.float32
BF16 = jnp.bfloat16

NT_DIMS = (((1,), (1,)), ((), ()))
TN_DIMS = (((0,), (0,)), ((), ()))


def _dot(a, b):
    return jnp.dot(a, b, preferred_element_type=F32)


def _dot_hi(a, b):
    return jnp.dot(a, b, preferred_element_type=F32, precision=HI)


def _dot_nt(a, b):
    return lax.dot_general(a, b, NT_DIMS, preferred_element_type=F32)


def _dot_tn(a, b):
    return lax.dot_general(a, b, TN_DIMS, preferred_element_type=F32)


def _sigmoid(x):
    return 1.0 / (1.0 + jnp.exp(-x))


def _silu(x):
    return x * _sigmoid(x)


def _softplus(x):
    return jnp.maximum(x, 0.0) + jnp.log(1.0 + jnp.exp(-jnp.abs(x)))


def _log_sigmoid(x):
    return jnp.minimum(x, 0.0) - jnp.log(1.0 + jnp.exp(-jnp.abs(x)))


def _iota2(shape, dim):
    return lax.broadcasted_iota(jnp.int32, shape, dim)


def _mod_kernel(cond_ref, w_ref, b_ref, o_ref):
    o_ref[...] = _dot_hi(_silu(cond_ref[...]), w_ref[...]) + b_ref[...]


def _modulation(cond, w_ada, b_ada):
    depth = w_ada.shape[0]
    rows = cond.shape[0]
    nblk = 3 * D_MODEL // D_MODEL
    return pl.pallas_call(
        _mod_kernel,
        grid=(depth, nblk),
        in_specs=[
            pl.BlockSpec((rows, D_MODEL), lambda l, j: (0, 0)),
            pl.BlockSpec((None, D_MODEL, D_MODEL), lambda l, j: (l, 0, j)),
            pl.BlockSpec((None, 1, D_MODEL), lambda l, j: (l, 0, j)),
        ],
        out_specs=pl.BlockSpec((None, rows, D_MODEL), lambda l, j: (l, 0, j)),
        out_shape=jax.ShapeDtypeStruct((depth, rows, 3 * D_MODEL), F32),
        compiler_params=pltpu.CompilerParams(dimension_semantics=("arbitrary", "arbitrary")),
        name="modulation",
    )(cond, w_ada, b_ada.reshape(depth, 1, 3 * D_MODEL))


def _in_proj_kernel(x_ref, shift_ref, scale_ref, nw_ref, w_ref, o_ref):
    x = x_ref[...]
    xn = x * lax.rsqrt(jnp.mean(x * x, axis=-1, keepdims=True) + EPS) * nw_ref[...]
    h = xn * (1.0 + scale_ref[...]) + shift_ref[...]
    o_ref[...] = _dot(h.astype(BF16), w_ref[...])


def _in_proj(x, shift, scale, norm_w, w_in_p, mod_row):
    n = x.shape[0]
    return pl.pallas_call(
        _in_proj_kernel,
        grid=(n // IN_TILE,),
        in_specs=[
            pl.BlockSpec((IN_TILE, D_MODEL), lambda i: (i, 0)),
            pl.BlockSpec((None, 1, D_MODEL), lambda i: (mod_row(i), 0, 0)),
            pl.BlockSpec((None, 1, D_MODEL), lambda i: (mod_row(i), 0, 0)),
            pl.BlockSpec((1, D_MODEL), lambda i: (0, 0)),
            pl.BlockSpec((D_MODEL, D_PROJ), lambda i: (0, 0)),
        ],
        out_specs=pl.BlockSpec((IN_TILE, D_PROJ), lambda i: (i, 0)),
        out_shape=jax.ShapeDtypeStruct((n, D_PROJ), F32),
        compiler_params=pltpu.CompilerParams(dimension_semantics=("arbitrary",),
                                             vmem_limit_bytes=VMEM_LIMIT_BYTES),
        name="in_proj",
    )(x, shift, scale, norm_w, w_in_p)


def _conv_silu(main, prev, nxt, conv_w, conv_b):
    t = main.shape[0]
    up = jnp.concatenate([prev, main, nxt], axis=0)
    rows = t + 2 * HALO
    pad = (SSD_CONV - 1) // 2
    acc = conv_b
    for k in range(SSD_CONV):
        off = k - pad
        shifted = up if off == 0 else pltpu.roll(up, (rows - off) % rows, 0)
        acc = acc + conv_w[k:k + 1, :] * shifted[HALO:HALO + t, :]
    return _silu(acc)


def _ssd_direction(xs, bm, cm, dt_small, a_small_row, a_exp_row, ht_ref, rev, direction):
    t = xs.shape[0]
    q = SSD_CHUNK
    hpg = SSD_HEADS // SSD_GROUPS
    gw = hpg * SSD_HEAD_DIM
    er = _iota2((LANES, SSD_WIDTH), 0)
    ec = _iota2((LANES, SSD_WIDTH), 1)
    expand = (er == (ec >> 6) + SSD_HEADS * direction).astype(F32)
    dt_exp = _dot_hi(dt_small, expand)
    xdt = xs * dt_exp
    a_exp = dt_exp * a_exp_row
    a_small = dt_small * a_small_row
    ri = _iota2((q, q), 0)
    ci = _iota2((q, q), 1)
    tri_incl = (ri >= ci).astype(F32)
    keep = (ri <= ci) if rev else (ri >= ci)
    lane_lo = _iota2((q, LANES), 1) < SSD_HEAD_DIM
    ys = [None] * (t // q)
    order = range(t // q - 1, -1, -1) if rev else range(t // q)
    for c in order:
        sl = slice(c * q, (c + 1) * q)
        a_e = a_exp[sl]
        a_s = a_small[sl]
        cs_e = _dot_hi(tri_incl, a_e)
        cs_s = _dot_hi(tri_incl, a_s)
        tot_e = cs_e[q - 1:q, :]
        if rev:
            pos_e = cs_e - a_e
            pos_s = cs_s - a_s
            yoff_fac = jnp.exp(tot_e - pos_e)
            wgt = jnp.exp(pos_e)
        else:
            pos_e = cs_e
            pos_s = cs_s
            yoff_fac = jnp.exp(pos_e)
            wgt = jnp.exp(tot_e - pos_e)
        dec_state = jnp.exp(tot_e)
        pos_t = pos_s.T
        xc = xdt[sl]
        y_groups = []
        for g in range(SSD_GROUPS):
            bg = bm[sl, g * SSD_STATE:(g + 1) * SSD_STATE].astype(BF16)
            cg = cm[sl, g * SSD_STATE:(g + 1) * SSD_STATE].astype(BF16)
            gmat = _dot_nt(cg, bg)
            ht = ht_ref[g]
            glanes = slice(g * gw, (g + 1) * gw)
            y_off = _dot(cg, ht.astype(BF16)) * yoff_fac[:, glanes]
            pairs = []
            for j in range(hpg // 2):
                res = []
                for hh in range(2):
                    lane = SSD_HEADS * direction + g * hpg + 2 * j + hh
                    col = pos_s[:, lane:lane + 1]
                    row = pos_t[lane:lane + 1, :]
                    seg = (row - col) if rev else (col - row)
                    decay = jnp.where(keep, jnp.exp(jnp.where(keep, seg, 0.0)), 0.0)
                    m = (gmat * decay).astype(BF16)
                    xp = xc[:, g * gw + j * LANES:g * gw + (j + 1) * LANES].astype(BF16)
                    res.append(_dot(m, xp))
                pairs.append(jnp.where(lane_lo, res[0], res[1]))
            y_groups.append(jnp.concatenate(pairs, axis=1) + y_off)
            xw = (xc[:, glanes] * wgt[:, glanes]).astype(BF16)
            ht_ref[g] = ht * dec_state[:, glanes] + _dot_tn(bg, xw)
        ys[c] = jnp.concatenate(y_groups, axis=1)
    return jnp.concatenate(ys, axis=0)


def _gla_direction(qh, kh, vh, glog, st_ref, rev):
    t = qh.shape[0]
    q = GLA_CHUNK
    ri = _iota2((t, t), 0)
    ci = _iota2((t, t), 1)
    same_chunk = (ri >> 6) == (ci >> 6)
    order_ok = (ri <= ci) if rev else (ri >= ci)
    prefix = (same_chunk & order_ok).astype(F32)
    gc = _dot_hi(prefix, glog)
    q_in = qh * jnp.exp(gc)
    k_in = kh * jnp.exp(-gc)
    rows = GLA_HEADS * q
    sr = _iota2((rows, q), 0) & (q - 1)
    sc = _iota2((rows, q), 1)
    causal = (sc >= sr) if rev else (sc <= sr)
    klane_head = _iota2((1, GLA_KEY_DIM), 1) >> 5
    vlane_head = _iota2((1, GLA_WIDTH), 1) >> 6
    blockdiag = (_iota2((GLA_WIDTH, GLA_KEY_DIM), 0) >> 6) == (_iota2((GLA_WIDTH, GLA_KEY_DIM), 1) >> 5)
    outs = [None] * (t // q)
    order = range(t // q - 1, -1, -1) if rev else range(t // q)
    for c in order:
        sl = slice(c * q, (c + 1) * q)
        gcs = gc[sl]
        g_last = gcs[0:1, :] if rev else gcs[q - 1:q, :]
        kw = kh[sl] * jnp.exp(g_last - gcs)
        qs = q_in[sl]
        qstack = jnp.concatenate([jnp.where(klane_head == h, qs, 0.0) for h in range(GLA_HEADS)], axis=0)
        att = _dot_nt(qstack.astype(BF16), k_in[sl].astype(BF16))
        att = jnp.where(causal, att, 0.0)
        vs = vh[sl].astype(BF16)
        pfull = _dot(att.astype(BF16), vs)
        o_intra = jnp.where(vlane_head == 0, pfull[0:q], 0.0)
        for h in range(1, GLA_HEADS):
            o_intra = o_intra + jnp.where(vlane_head == h, pfull[h * q:(h + 1) * q], 0.0)
        st = st_ref[...]
        o_inter = _dot_nt(qs.astype(BF16), st.astype(BF16))
        outs[c] = o_intra + o_inter
        upd = _dot_tn(vs, kw.astype(BF16))
        st_ref[...] = st * jnp.exp(g_last) + jnp.where(blockdiag, upd, 0.0)
    return jnp.concatenate(outs, axis=0)


def _head_rms(x, w_row):
    n = x.shape[1]
    ones = ((_iota2((n, n), 0) >> 6) == (_iota2((n, n), 1) >> 6)).astype(F32)
    ms = _dot_hi(x * x, ones) * (1.0 / HEAD_DIM)
    return x * lax.rsqrt(ms + EPS) * w_row


def _rope(x, cos, sin_signed):
    n = x.shape[1]
    quarter = HEAD_DIM // 4
    first = (_iota2((1, n), 1) & (2 * quarter - 1)) < quarter
    swapped = jnp.where(first, pltpu.roll(x, n - quarter, 1), pltpu.roll(x, quarter, 1))
    return x * cos + swapped * sin_signed


def _softmax_sink_attend(qg, kcat, vcat, sink_col, mask):
    s = _dot_nt(qg, kcat) * (HEAD_DIM ** -0.5)
    if mask is not None:
        s = jnp.where(mask, s, NEG_INF)
    m = jnp.maximum(jnp.max(s, axis=-1, keepdims=True), sink_col)
    p = jnp.exp(s - m)
    den = jnp.sum(p, axis=-1, keepdims=True) + jnp.exp(sink_col - m)
    return _dot(p.astype(BF16), vcat) / den


def _mixer_common(proj_ref, xprev_ref, xnext_ref, convw_ref, convb_ref, dtb_ref, alog_s_ref, tile_idx, n_tiles):
    t = proj_ref.shape[0]
    prev = jnp.where(tile_idx > 0, xprev_ref[...], 0.0)
    nxt = jnp.where(tile_idx < n_tiles - 1, xnext_ref[...], 0.0)
    xbc = _conv_silu(proj_ref[:, C_XBC:C_XBC + SSD_CONV_DIM], prev, nxt, convw_ref[...], convb_ref[...])
    xs = xbc[:, :SSD_WIDTH]
    bm = xbc[:, SSD_WIDTH:SSD_WIDTH + SSD_BC]
    cm = xbc[:, SSD_WIDTH + SSD_BC:]
    dt_small = _softplus(proj_ref[:, C_DT:C_DT + LANES] + dtb_ref[...])
    a_small_row = -jnp.exp(alog_s_ref[...])
    del t
    return xs, bm, cm, dt_small, a_small_row


def _gla_inputs(proj_ref, gkw_ref, gkb_ref):
    qh = proj_ref[:, C_GQ:C_GQ + GLA_KEY_DIM] * (GLA_DK ** -0.5)
    kh = proj_ref[:, C_GK:C_GK + GLA_KEY_DIM]
    vh = proj_ref[:, C_GV:C_GV + GLA_WIDTH]
    gate = _dot_hi(proj_ref[:, C_GLR:C_GLR + LANES], gkw_ref[...]) + gkb_ref[...]
    glog = _log_sigmoid(gate) * (1.0 / GLA_GATE_NORM)
    return qh, kh, vh, glog


def _mix_bwd_kernel(*refs, is_ctx, n_tiles):
    if is_ctx:
        (proj_ref, xprev_ref, xnext_ref, convw_ref, convb_ref, dtb_ref, alog_s_ref, alog_e_ref, gkw_ref, gkb_ref,
         yb_ref, ob_ref, htf_ref, stf_ref, ht_ref, st_ref) = refs
    else:
        (proj_ref, xprev_ref, xnext_ref, convw_ref, convb_ref, dtb_ref, alog_s_ref, alog_e_ref, gkw_ref, gkb_ref,
         ht0_ref, st0_ref, yb_ref, ob_ref, ht_ref, st_ref) = refs
    step = pl.program_id(1)
    tile_idx = n_tiles - 1 - step

    @pl.when(step == 0)
    def _():
        if is_ctx:
            ht_ref[...] = jnp.zeros(ht_ref.shape, F32)
            st_ref[...] = jnp.zeros(st_ref.shape, F32)
        else:
            ht_ref[...] = ht0_ref[...]
            st_ref[...] = st0_ref[...]

    xs, bm, cm, dt_small, a_small_row = _mixer_common(
        proj_ref, xprev_ref, xnext_ref, convw_ref, convb_ref, dtb_ref, alog_s_ref, tile_idx, n_tiles)
    a_exp_row = -jnp.exp(alog_e_ref[...])
    yb_ref[...] = _ssd_direction(xs, bm, cm, dt_small, a_small_row, a_exp_row, ht_ref, True, 1)
    qh, kh, vh, glog = _gla_inputs(proj_ref, gkw_ref, gkb_ref)
    ob_ref[...] = _gla_direction(qh, kh, vh, glog, st_ref, True)
    if is_ctx:
        htf_ref[...] = ht_ref[...]
        stf_ref[...] = st_ref[...]


def _halo_specs(n_tiles, rev, width, col_block):
    per = TILE // HALO

    def tile_of(s, t):
        return s * n_tiles + ((n_tiles - 1 - t) if rev else t)

    def prev_map(s, t):
        return (jnp.maximum(tile_of(s, t) * per - 1, 0), col_block)

    def next_map(s, t, last):
        return (jnp.minimum((tile_of(s, t) + 1) * per, last), col_block)

    return tile_of, prev_map, next_map


def _mix_bwd(proj, consts, ht0, st0, *, n_seq, n_tiles, is_ctx):
    n = proj.shape[0]
    tile_of, prev_map, next_map = _halo_specs(n_tiles, True, SSD_CONV_DIM, 0)
    last_halo = n // HALO - 1
    const2 = lambda s, t: (0, 0)
    in_specs = [
        pl.BlockSpec((TILE, BWD_COLS), lambda s, t: (tile_of(s, t), 0)),
        pl.BlockSpec((HALO, SSD_CONV_DIM), prev_map),
        pl.BlockSpec((HALO, SSD_CONV_DIM), functools.partial(next_map, last=last_halo)),
        pl.BlockSpec((SUBLANES, SSD_CONV_DIM), const2),
        pl.BlockSpec((1, SSD_CONV_DIM), const2),
        pl.BlockSpec((1, LANES), const2),
        pl.BlockSpec((1, LANES), const2),
        pl.BlockSpec((1, SSD_WIDTH), const2),
        pl.BlockSpec((LANES, GLA_KEY_DIM), const2),
        pl.BlockSpec((1, GLA_KEY_DIM), const2),
    ]
    args = [proj, proj, proj, consts["conv_w"], consts["conv_b"], consts["dt_bias"], consts["a_log_small"],
            consts["a_log_exp"][1], consts["gk_w"][1], consts["gk_b"][1]]
    out_specs = [
        pl.BlockSpec((TILE, SSD_WIDTH), lambda s, t: (tile_of(s, t), 0)),
        pl.BlockSpec((TILE, GLA_WIDTH), lambda s, t: (tile_of(s, t), 0)),
    ]
    out_shape = [jax.ShapeDtypeStruct((n, SSD_WIDTH), F32), jax.ShapeDtypeStruct((n, GLA_WIDTH), F32)]
    ht_shape = (SSD_GROUPS, SSD_STATE, SSD_WIDTH // SSD_GROUPS)
    st_shape = (GLA_WIDTH, GLA_KEY_DIM)
    if is_ctx:
        out_specs += [pl.BlockSpec((None,) + ht_shape, lambda s, t: (s, 0, 0, 0)),
                      pl.BlockSpec((None,) + st_shape, lambda s, t: (s, 0, 0))]
        out_shape += [jax.ShapeDtypeStruct((n_seq,) + ht_shape, F32), jax.ShapeDtypeStruct((n_seq,) + st_shape, F32)]
    else:
        in_specs += [pl.BlockSpec((None,) + ht_shape, lambda s, t: (s, 0, 0, 0)),
                     pl.BlockSpec((None,) + st_shape, lambda s, t: (s, 0, 0))]
        args += [ht0, st0]
    return pl.pallas_call(
        functools.partial(_mix_bwd_kernel, is_ctx=is_ctx, n_tiles=n_tiles),
        grid=(n_seq, n_tiles),
        in_specs=in_specs,
        out_specs=out_specs,
        out_shape=out_shape,
        scratch_shapes=[pltpu.VMEM(ht_shape, F32), pltpu.VMEM(st_shape, F32)],
        compiler_params=pltpu.CompilerParams(dimension_semantics=("arbitrary", "arbitrary"),
                                             vmem_limit_bytes=VMEM_LIMIT_BYTES),
        name="mix_bwd_ctx" if is_ctx else "mix_bwd_lat",
    )(*args)


def _mix_fwd_kernel(*refs, is_ctx, n_tiles):
    it = iter(refs)
    sink_ref = next(it)
    x_ref = next(it)
    gate_ref = next(it)
    proj_ref = next(it)
    xprev_ref = next(it)
    xnext_ref = next(it)
    yb_ref = next(it)
    ob_ref = next(it)
    convw_ref = next(it)
    convb_ref = next(it)
    dtb_ref = next(it)
    alog_s_ref = next(it)
    alog_e_ref = next(it)
    gkw_ref = next(it)
    gkb_ref = next(it)
    dskip_ref = next(it)
    ssdnw_ref = next(it)
    glanw_ref = next(it)
    qnw_ref = next(it)
    knw_ref = next(it)
    wout_ref = next(it)
    if not is_ctx:
        ht0_ref = next(it)
        st0_ref = next(it)
        kvprev_ref = next(it)
        kvnext_ref = next(it)
        cos_ref = next(it)
        sin_ref = next(it)
        cosp_ref = next(it)
        sinp_ref = next(it)
        cosn_ref = next(it)
        sinn_ref = next(it)
        kc_ref = next(it)
        vc_ref = next(it)
    o_ref = next(it)
    if is_ctx:
        kout_ref = next(it)
        vout_ref = next(it)
        htf_ref = next(it)
        stf_ref = next(it)
    ht_ref = next(it)
    st_ref = next(it)

    tile_idx = pl.program_id(1)

    @pl.when(tile_idx == 0)
    def _():
        if is_ctx:
            ht_ref[...] = jnp.zeros(ht_ref.shape, F32)
            st_ref[...] = jnp.zeros(st_ref.shape, F32)
        else:
            ht_ref[...] = ht0_ref[...]
            st_ref[...] = st0_ref[...]

    t = TILE
    xs, bm, cm, dt_small, a_small_row = _mixer_common(
        proj_ref, xprev_ref, xnext_ref, convw_ref, convb_ref, dtb_ref, alog_s_ref, tile_idx, n_tiles)
    a_exp_row = -jnp.exp(alog_e_ref[...])

    y_f = _ssd_direction(xs, bm, cm, dt_small, a_small_row, a_exp_row, ht_ref, False, 0)
    y = y_f + yb_ref[...] + xs * dskip_ref[...]
    y = y * _silu(proj_ref[:, C_Z:C_Z + SSD_WIDTH])
    y_ssd = y * lax.rsqrt(jnp.mean(y * y, axis=-1, keepdims=True) + EPS) * ssdnw_ref[...]

    qh, kh, vh, glog = _gla_inputs(proj_ref, gkw_ref, gkb_ref)
    o = _gla_direction(qh, kh, vh, glog, st_ref, False) + ob_ref[...]
    y_gla = _head_rms(o, glanw_ref[...]) * _silu(proj_ref[:, C_GG:C_GG + GLA_WIDTH])

    qn = _head_rms(proj_ref[:, C_AQ:C_AQ + ATT_WIDTH], qnw_ref[...])
    kn = _head_rms(proj_ref[:, C_AK:C_AK + ATT_KV_DIM], knw_ref[...])
    vv = proj_ref[:, C_AV:C_AV + ATT_KV_DIM]
    head_out = [None] * ATT_HEADS
    if is_ctx:
        kout_ref[...] = kn
        vout_ref[...] = vv
        for kvh in range(ATT_KV_HEADS):
            ks = kn[:, kvh * HEAD_DIM:(kvh + 1) * HEAD_DIM].astype(BF16)
            vs = vv[:, kvh * HEAD_DIM:(kvh + 1) * HEAD_DIM].astype(BF16)
            qg = jnp.concatenate(
                [qn[:, (kvh * ATT_GROUP + g) * HEAD_DIM:(kvh * ATT_GROUP + g + 1) * HEAD_DIM] for g in range(ATT_GROUP)],
                axis=0).astype(BF16)
            sink_col = jnp.concatenate(
                [jnp.full((t, 1), sink_ref[kvh * ATT_GROUP + g], F32) for g in range(ATT_GROUP)], axis=0)
            og = _softmax_sink_attend(qg, ks, vs, sink_col, None)
            for g in range(ATT_GROUP):
                head_out[kvh * ATT_GROUP + g] = og[g * t:(g + 1) * t]
        y_att = jnp.concatenate(head_out, axis=1)
    else:
        cos = cos_ref[...]
        sin = sin_ref[...]
        qr = _rope(qn, jnp.concatenate([cos, cos], axis=1), jnp.concatenate([sin, sin], axis=1))
        k_halo_p = _rope(_head_rms(kvprev_ref[:, :ATT_KV_DIM], knw_ref[...]), cosp_ref[...], sinp_ref[...])
        k_halo_n = _rope(_head_rms(kvnext_ref[:, :ATT_KV_DIM], knw_ref[...]), cosn_ref[...], sinn_ref[...])
        k_loc = jnp.concatenate([k_halo_p, _rope(kn, cos, sin), k_halo_n], axis=0)
        v_loc = jnp.concatenate([kvprev_ref[:, ATT_KV_DIM:], vv, kvnext_ref[:, ATT_KV_DIM:]], axis=0)
        kc = kc_ref[...]
        vc = vc_ref[...]
        blk = ATT_BLOCK
        win = 3 * blk
        rows = ATT_GROUP * blk
        n_ctx = kc.shape[0]
        qi = _iota2((rows, n_ctx + win), 0) & (blk - 1)
        cj = _iota2((rows, n_ctx + win), 1)
        kj = cj - n_ctx
        band = (kj >= qi) & (kj <= qi + 2 * WINDOW)
        seq_len = n_tiles * t
        blocks = []
        for nb in range(t // blk):
            key_pos = tile_idx * t + (nb - 1) * blk + kj
            mask = (cj < n_ctx) | (band & (key_pos >= 0) & (key_pos < seq_len))
            per_head = [None] * ATT_HEADS
            for kvh in range(ATT_KV_HEADS):
                hl = slice(kvh * HEAD_DIM, (kvh + 1) * HEAD_DIM)
                kcat = jnp.concatenate([kc[:, hl], k_loc[nb * blk:nb * blk + win, hl]], axis=0).astype(BF16)
                vcat = jnp.concatenate([vc[:, hl], v_loc[nb * blk:nb * blk + win, hl]], axis=0).astype(BF16)
                qg = jnp.concatenate(
                    [qr[nb * blk:(nb + 1) * blk, (kvh * ATT_GROUP + g) * HEAD_DIM:(kvh * ATT_GROUP + g + 1) * HEAD_DIM]
                     for g in range(ATT_GROUP)], axis=0).astype(BF16)
                sink_col = jnp.concatenate(
                    [jnp.full((blk, 1), sink_ref[kvh * ATT_GROUP + g], F32) for g in range(ATT_GROUP)], axis=0)
                og = _softmax_sink_attend(qg, kcat, vcat, sink_col, mask)
                for g in range(ATT_GROUP):
                    per_head[kvh * ATT_GROUP + g] = og[g * blk:(g + 1) * blk]
            blocks.append(jnp.concatenate(per_head, axis=1))
        y_att = jnp.concatenate(blocks, axis=0)
    y_att = y_att * _silu(proj_ref[:, C_AG:C_AG + ATT_WIDTH])

    ycat = jnp.concatenate([y_ssd, y_gla, y_att], axis=1).astype(BF16)
    o_ref[...] = x_ref[...] + gate_ref[...] * _dot(ycat, wout_ref[...])
    if is_ctx:
        htf_ref[...] = ht_ref[...]
        stf_ref[...] = st_ref[...]


def _mix_fwd(x, gate, gate_row, proj, yb, ob, consts, sink, w_out, lat, *, n_seq, n_tiles, is_ctx):
    n = x.shape[0]
    tile_of, prev_map, next_map = _halo_specs(n_tiles, False, SSD_CONV_DIM, 0)
    last_halo = n // HALO - 1
    const2 = lambda s, t: (0, 0)
    row = lambda s, t: (tile_of(s, t), 0)
    in_specs = [
        pl.BlockSpec(memory_space=pltpu.SMEM),
        pl.BlockSpec((TILE, D_MODEL), row),
        pl.BlockSpec((None, 1, D_MODEL), lambda s, t: (gate_row(s), 0, 0)),
        pl.BlockSpec((TILE, D_PROJ), row),
        pl.BlockSpec((HALO, SSD_CONV_DIM), prev_map),
        pl.BlockSpec((HALO, SSD_CONV_DIM), functools.partial(next_map, last=last_halo)),
        pl.BlockSpec((TILE, SSD_WIDTH), row),
        pl.BlockSpec((TILE, GLA_WIDTH), row),
        pl.BlockSpec((SUBLANES, SSD_CONV_DIM), const2),
        pl.BlockSpec((1, SSD_CONV_DIM), const2),
        pl.BlockSpec((1, LANES), const2),
        pl.BlockSpec((1, LANES), const2),
        pl.BlockSpec((1, SSD_WIDTH), const2),
        pl.BlockSpec((LANES, GLA_KEY_DIM), const2),
        pl.BlockSpec((1, GLA_KEY_DIM), const2),
        pl.BlockSpec((1, SSD_WIDTH), const2),
        pl.BlockSpec((1, SSD_WIDTH), const2),
        pl.BlockSpec((1, GLA_WIDTH), const2),
        pl.BlockSpec((1, ATT_WIDTH), const2),
        pl.BlockSpec((1, ATT_KV_DIM), const2),
        pl.BlockSpec((D_MIX, D_MODEL), const2),
    ]
    args = [sink, x, gate, proj, proj, proj, yb, ob, consts["conv_w"], consts["conv_b"], consts["dt_bias"],
            consts["a_log_small"], consts["a_log_exp"][0], consts["gk_w"][0], consts["gk_b"][0], consts["d_skip"],
            consts["ssd_norm_w"], consts["gla_norm_w"], consts["q_norm_w"], consts["k_norm_w"], w_out]
    ht_shape = (SSD_GROUPS, SSD_STATE, SSD_WIDTH // SSD_GROUPS)
    st_shape = (GLA_WIDTH, GLA_KEY_DIM)
    out_specs = [pl.BlockSpec((TILE, D_MODEL), row)]
    out_shape = [jax.ShapeDtypeStruct((n, D_MODEL), F32)]
    if is_ctx:
        out_specs += [pl.BlockSpec((TILE, ATT_KV_DIM), row), pl.BlockSpec((TILE, ATT_KV_DIM), row),
                      pl.BlockSpec((None,) + ht_shape, lambda s, t: (s, 0, 0, 0)),
                      pl.BlockSpec((None,) + st_shape, lambda s, t: (s, 0, 0))]
        out_shape += [jax.ShapeDtypeStruct((n, ATT_KV_DIM), F32), jax.ShapeDtypeStruct((n, ATT_KV_DIM), F32),
                      jax.ShapeDtypeStruct((n_seq,) + ht_shape, F32), jax.ShapeDtypeStruct((n_seq,) + st_shape, F32)]
    else:
        per_blk = TILE // ATT_BLOCK
        kv_col = C_AK // (2 * ATT_KV_DIM)
        last_blk = n // ATT_BLOCK - 1
        kvprev = lambda s, t: (jnp.maximum(tile_of(s, t) * per_blk - 1, 0), kv_col)
        kvnext = lambda s, t: (jnp.minimum((tile_of(s, t) + 1) * per_blk, last_blk), kv_col)
        last_pos_blk = n_tiles * per_blk - 1
        pos_main = lambda s, t: (t, 0)
        pos_prev = lambda s, t: (jnp.maximum(t * per_blk - 1, 0), 0)
        pos_next = lambda s, t: (jnp.minimum((t + 1) * per_blk, last_pos_blk), 0)
        n_ctx = lat["k_ctx"].shape[1]
        in_specs += [
            pl.BlockSpec((None,) + ht_shape, lambda s, t: (s, 0, 0, 0)),
            pl.BlockSpec((None,) + st_shape, lambda s, t: (s, 0, 0)),
            pl.BlockSpec((ATT_BLOCK, 2 * ATT_KV_DIM), kvprev),
            pl.BlockSpec((ATT_BLOCK, 2 * ATT_KV_DIM), kvnext),
            pl.BlockSpec((TILE, ATT_KV_DIM), pos_main),
            pl.BlockSpec((TILE, ATT_KV_DIM), pos_main),
            pl.BlockSpec((ATT_BLOCK, ATT_KV_DIM), pos_prev),
            pl.BlockSpec((ATT_BLOCK, ATT_KV_DIM), pos_prev),
            pl.BlockSpec((ATT_BLOCK, ATT_KV_DIM), pos_next),
            pl.BlockSpec((ATT_BLOCK, ATT_KV_DIM), pos_next),
            pl.BlockSpec((None, n_ctx, ATT_KV_DIM), lambda s, t: (s, 0, 0)),
            pl.BlockSpec((None, n_ctx, ATT_KV_DIM), lambda s, t: (s, 0, 0)),
        ]
        args += [lat["ht0"], lat["st0"], proj, proj, lat["cos"], lat["sin"], lat["cos"], lat["sin"], lat["cos"],
                 lat["sin"], lat["k_ctx"], lat["v_ctx"]]
    return pl.pallas_call(
        functools.partial(_mix_fwd_kernel, is_ctx=is_ctx, n_tiles=n_tiles),
        grid=(n_seq, n_tiles),
        in_specs=in_specs,
        out_specs=out_specs,
        out_shape=out_shape,
        scratch_shapes=[pltpu.VMEM(ht_shape, F32), pltpu.VMEM(st_shape, F32)],
        compiler_params=pltpu.CompilerParams(dimension_semantics=("arbitrary", "arbitrary"),
                                             vmem_limit_bytes=VMEM_LIMIT_BYTES),
        name="mix_fwd_ctx" if is_ctx else "mix_fwd_lat",
    )(*args)


def _pad_cols(a, width):
    return jnp.pad(a, [(0, 0)] * (a.ndim - 1) + [(0, width - a.shape[-1])])


def _prep_w_in(w_in):
    sizes = (SSD_WIDTH, SSD_CONV_DIM, 2 * SSD_HEADS, GLA_KEY_DIM, GLA_KEY_DIM, GLA_WIDTH, GLA_WIDTH, 2 * GLA_LOWRANK,
             ATT_WIDTH, ATT_KV_DIM, ATT_KV_DIM, ATT_WIDTH)
    parts, start = [], 0
    for size in sizes:
        parts.append(w_in[..., start:start + size])
        start += size
    z, xbc, dt, gq, gk, gv, gg, glr, aq, ak, av, ag = parts
    cols = [xbc, z, _pad_cols(dt, LANES), gq, gk, gv, gg, _pad_cols(glr, LANES), aq, ak, av, ag]
    return jnp.concatenate(cols, axis=-1).astype(BF16)


def _layer_consts(l, conv_w, conv_b, ssd_a_log, ssd_dt_bias, ssd_d, ssd_norm_w, gla_gk_up, gla_gk_b, gla_norm_w,
                  attn_q_norm, attn_k_norm):
    gk_w = []
    for d in range(2):
        w = jnp.zeros((LANES, GLA_KEY_DIM), F32)
        gk_w.append(w.at[d * GLA_LOWRANK:(d + 1) * GLA_LOWRANK].set(gla_gk_up[l, d]))
    return {
        "conv_w": jnp.pad(conv_w[l], ((0, SUBLANES - SSD_CONV), (0, 0))),
        "conv_b": conv_b[l].reshape(1, SSD_CONV_DIM),
        "dt_bias": _pad_cols(ssd_dt_bias[l].reshape(1, 2 * SSD_HEADS), LANES),
        "a_log_small": _pad_cols(ssd_a_log[l].reshape(1, 2 * SSD_HEADS), LANES),
        "a_log_exp": [jnp.repeat(ssd_a_log[l, d], SSD_HEAD_DIM).reshape(1, SSD_WIDTH) for d in range(2)],
        "gk_w": gk_w,
        "gk_b": [gla_gk_b[l, d].reshape(1, GLA_KEY_DIM) for d in range(2)],
        "d_skip": jnp.repeat(ssd_d[l], SSD_HEAD_DIM).reshape(1, SSD_WIDTH),
        "ssd_norm_w": ssd_norm_w[l].reshape(1, SSD_WIDTH),
        "gla_norm_w": jnp.tile(gla_norm_w[l], GLA_HEADS).reshape(1, GLA_WIDTH),
        "q_norm_w": jnp.tile(attn_q_norm[l], ATT_HEADS).reshape(1, ATT_WIDTH),
        "k_norm_w": jnp.tile(attn_k_norm[l], ATT_KV_HEADS).reshape(1, ATT_KV_DIM),
    }


def _ssd_state_to_kernel(s):
    b = s.shape[0]
    hpg = SSD_HEADS // SSD_GROUPS
    s = s.reshape(b, SSD_GROUPS, hpg, SSD_HEAD_DIM, SSD_STATE)
    return jnp.transpose(s, (0, 1, 4, 2, 3)).reshape(b, SSD_GROUPS, SSD_STATE, hpg * SSD_HEAD_DIM)


def _ssd_state_from_kernel(s):
    b = s.shape[0]
    hpg = SSD_HEADS // SSD_GROUPS
    s = s.reshape(b, SSD_GROUPS, SSD_STATE, hpg, SSD_HEAD_DIM)
    return jnp.transpose(s, (0, 1, 3, 4, 2)).reshape(b, SSD_HEADS, SSD_HEAD_DIM, SSD_STATE)


def _gla_state_to_kernel(s):
    b = s.shape[0]
    st = jnp.transpose(s, (0, 1, 3, 2))
    eye = jnp.eye(GLA_HEADS, dtype=s.dtype)
    full = st[:, :, :, None, :] * eye[None, :, None, :, None]
    return full.reshape(b, GLA_WIDTH, GLA_KEY_DIM)


def _gla_state_from_kernel(s):
    b = s.shape[0]
    full = s.reshape(b, GLA_HEADS, GLA_DV, GLA_HEADS, GLA_DK)
    diag = jnp.stack([full[:, h, :, h, :] for h in range(GLA_HEADS)], axis=1)
    return jnp.transpose(diag, (0, 1, 3, 2))


def _rope_tables(seq_len):
    quarter = HEAD_DIM // 4
    rows = seq_len // GRID_W
    row_pos = jnp.repeat(jnp.arange(rows, dtype=F32), GRID_W)
    col_pos = jnp.tile(jnp.arange(GRID_W, dtype=F32), rows)
    inv = ROPE_THETA ** (-jnp.arange(quarter, dtype=F32) / quarter)
    ang_r = row_pos[:, None] * inv[None, :]
    ang_c = col_pos[:, None] * inv[None, :]
    cos = jnp.concatenate([jnp.cos(ang_r), jnp.cos(ang_r), jnp.cos(ang_c), jnp.cos(ang_c)], axis=1)
    sin = jnp.concatenate([-jnp.sin(ang_r), jnp.sin(ang_r), -jnp.sin(ang_c), jnp.sin(ang_c)], axis=1)
    return jnp.tile(cos, (1, ATT_KV_HEADS)), jnp.tile(sin, (1, ATT_KV_HEADS))


def kernel(x_prompt, x_sample, c, cache_k, cache_v, state_ssd, state_gla, c_ctx, w_ada, b_ada, norm_w, w_in, conv_w,
           conv_b, ssd_a_log, ssd_dt_bias, ssd_d, ssd_norm_w, gla_gk_up, gla_gk_b, gla_norm_w, attn_q_norm, attn_k_norm,
           attn_sink, w_out):
    batch, seq, _ = x_prompt.shape
    dec_batch, dec_seq, _ = x_sample.shape
    depth = w_in.shape[0]
    past = cache_k.shape[2]
    assert seq % TILE == 0 and dec_seq % TILE == 0
    assert (batch * seq) % IN_TILE == 0 and dec_seq % IN_TILE == 0

    rows = -(-(dec_batch + 1) // SUBLANES) * SUBLANES
    cond = jnp.zeros((rows, D_MODEL), F32).at[:dec_batch].set(c).at[dec_batch].set(c_ctx)
    mod = _modulation(cond, w_ada, b_ada)

    w_in_p = _prep_w_in(w_in)
    w_out_b = w_out.astype(BF16)
    cos_t, sin_t = _rope_tables(dec_seq)

    x_ctx = x_prompt.reshape(batch * seq, D_MODEL)
    x_lat = x_sample.reshape(dec_batch * dec_seq, D_MODEL)
    ctx_tiles = seq // TILE
    lat_tiles = dec_seq // TILE
    lat_in_per_seq = dec_seq // IN_TILE

    ks_out, vs_out, ssd_out, gla_out = [], [], [], []
    for l in range(depth):
        consts = _layer_consts(l, conv_w, conv_b, ssd_a_log, ssd_dt_bias, ssd_d, ssd_norm_w, gla_gk_up, gla_gk_b,
                               gla_norm_w, attn_q_norm, attn_k_norm)
        shift = mod[l, :, :D_MODEL].reshape(rows, 1, D_MODEL)
        scale = mod[l, :, D_MODEL:2 * D_MODEL].reshape(rows, 1, D_MODEL)
        gate = mod[l, :, 2 * D_MODEL:].reshape(rows, 1, D_MODEL)
        nw = norm_w[l].reshape(1, D_MODEL)

        proj = _in_proj(x_ctx, shift, scale, nw, w_in_p[l], lambda i: dec_batch)
        yb, ob, ht_b, st_b = _mix_bwd(proj, consts, None, None, n_seq=batch, n_tiles=ctx_tiles, is_ctx=True)
        x_ctx, k_l, v_l, ht_f, st_f = _mix_fwd(x_ctx, gate, lambda s: dec_batch, proj, yb, ob, consts, attn_sink[l],
                                               w_out_b[l], None, n_seq=batch, n_tiles=ctx_tiles, is_ctx=True)
        ks_out.append(k_l.reshape(batch, seq, ATT_KV_HEADS, HEAD_DIM))
        vs_out.append(v_l.reshape(batch, seq, ATT_KV_HEADS, HEAD_DIM))
        ssd_out.append(jnp.stack([_ssd_state_from_kernel(ht_f), _ssd_state_from_kernel(ht_b)], axis=1))
        gla_out.append(jnp.stack([_gla_state_from_kernel(st_f), _gla_state_from_kernel(st_b)], axis=1))

        proj = _in_proj(x_lat, shift, scale, nw, w_in_p[l], lambda i: i // lat_in_per_seq)
        ht0 = [_ssd_state_to_kernel(state_ssd[:, l, d]) for d in range(2)]
        st0 = [_gla_state_to_kernel(state_gla[:, l, d]) for d in range(2)]
        yb, ob = _mix_bwd(proj, consts, ht0[1], st0[1], n_seq=dec_batch, n_tiles=lat_tiles, is_ctx=False)
        lat = {"ht0": ht0[0], "st0": st0[0], "cos": cos_t, "sin": sin_t,
               "k_ctx": cache_k[:, l].reshape(dec_batch, past, ATT_KV_DIM),
               "v_ctx": cache_v[:, l].reshape(dec_batch, past, ATT_KV_DIM)}
        (x_lat,) = _mix_fwd(x_lat, gate, lambda s: s, proj, yb, ob, consts, attn_sink[l], w_out_b[l], lat,
                            n_seq=dec_batch, n_tiles=lat_tiles, is_ctx=False)

    return (x_ctx.reshape(batch, seq, D_MODEL), x_lat.reshape(dec_batch, dec_seq, D_MODEL),
            jnp.stack(ks_out, axis=1), jnp.stack(vs_out, axis=1), jnp.stack(ssd_out, axis=1),
            jnp.stack(gla_out, axis=1))
```

```python
import functools
import math

import jax
import jax.numpy as jnp
from jax import lax
from jax.experimental import pallas as pl
from jax.experimental.pallas import tpu as pltpu

D_MODEL = 1024
GRID_W = 64
SSD_HEADS = 8
SSD_HEAD_DIM = 64
SSD_WIDTH = 512
SSD_GROUPS = 2
SSD_STATE = 128
SSD_BC = SSD_GROUPS * SSD_STATE
SSD_CONV = 5
SSD_CONV_DIM = SSD_WIDTH + 2 * SSD_BC
SSD_CHUNK = 128
GLA_HEADS = 4
GLA_DK = 32
GLA_DV = 64
GLA_KEY_DIM = GLA_HEADS * GLA_DK
GLA_WIDTH = GLA_HEADS * GLA_DV
GLA_LOWRANK = 16
GLA_GATE_NORM = 16.0
GLA_CHUNK = 64
ATT_HEADS = 4
ATT_KV_HEADS = 2
ATT_GROUP = ATT_HEADS // ATT_KV_HEADS
HEAD_DIM = 64
ATT_WIDTH = ATT_HEADS * HEAD_DIM
ATT_KV_DIM = ATT_KV_HEADS * HEAD_DIM
WINDOW = 128
ATT_BLOCK = 128
ROPE_THETA = 10000.0
D_MIX = SSD_WIDTH + GLA_WIDTH + ATT_WIDTH
EPS = 1e-6
NEG_INF = -1e30
LOG2E = math.log2(math.e)

LANES = 128
SUBLANES = 8
VMEM_LIMIT_BYTES = 56 * 1024 * 1024

P_DT = 0
P_GQ = P_DT + LANES
P_GK = P_GQ + GLA_KEY_DIM
P_GLR = P_GK + GLA_KEY_DIM
P_GV = P_GLR + LANES
P_AK = P_GV + GLA_WIDTH
P_AV = P_AK + ATT_KV_DIM
P_Z = P_AV + ATT_KV_DIM
P_GG = P_Z + SSD_WIDTH
P_AQ = P_GG + GLA_WIDTH
P_AG = P_AQ + ATT_WIDTH
D_REST = P_AG + ATT_WIDTH
D_PROJ = SSD_CONV_DIM + D_REST
BWD_COLS = P_AK

TILE = 256
IN_TILE = 512
HALO = SUBLANES

HI = lax.Precision.HIGHEST
F32 = jnp.float32
BF16 = jnp.bfloat16

NT_DIMS = (((1,), (1,)), ((), ()))
TN_DIMS = (((0,), (0,)), ((), ()))


def _dot(a, b):
    return jnp.dot(a, b, preferred_element_type=F32)


def _dot_nt(a, b):
    return lax.dot_general(a, b, NT_DIMS, preferred_element_type=F32)


def _dot_tn(a, b):
    return lax.dot_general(a, b, TN_DIMS, preferred_element_type=F32)


def _split(x, parts):
    out = []
    for _ in range(parts - 1):
        piece = x.astype(BF16)
        out.append(piece)
        x = x - piece.astype(F32)
    out.append(x.astype(BF16))
    return out


def _dot_split_lhs(x, w_bf16, parts):
    acc = None
    for piece in _split(x, parts):
        term = _dot(piece, w_bf16)
        acc = term if acc is None else acc + term
    return acc


def _dot_split_rhs(w_bf16, x, parts):
    acc = None
    for piece in _split(x, parts):
        term = _dot(w_bf16, piece)
        acc = term if acc is None else acc + term
    return acc


def _sigmoid(x):
    return 1.0 / (1.0 + jnp.exp(-x))


def _silu(x):
    return x * _sigmoid(x)


def _softplus(x):
    return jnp.maximum(x, 0.0) + jnp.log(1.0 + jnp.exp(-jnp.abs(x)))


def _log_sigmoid(x):
    return jnp.minimum(x, 0.0) - jnp.log(1.0 + jnp.exp(-jnp.abs(x)))


def _iota2(shape, dim):
    return lax.broadcasted_iota(jnp.int32, shape, dim)


def _mod_kernel(cond_ref, w_ref, b_ref, o_ref):
    o_ref[...] = jnp.dot(_silu(cond_ref[...]), w_ref[...], preferred_element_type=F32, precision=HI) + b_ref[...]


def _modulation(cond, w_ada, b_ada):
    depth = w_ada.shape[0]
    rows = cond.shape[0]
    return pl.pallas_call(
        _mod_kernel,
        grid=(depth, 3),
        in_specs=[
            pl.BlockSpec((rows, D_MODEL), lambda l, j: (0, 0)),
            pl.BlockSpec((None, D_MODEL, D_MODEL), lambda l, j: (l, 0, j)),
            pl.BlockSpec((None, 1, D_MODEL), lambda l, j: (l, 0, j)),
        ],
        out_specs=pl.BlockSpec((None, rows, D_MODEL), lambda l, j: (l, 0, j)),
        out_shape=jax.ShapeDtypeStruct((depth, rows, 3 * D_MODEL), F32),
        compiler_params=pltpu.CompilerParams(dimension_semantics=("arbitrary", "arbitrary")),
        name="modulation",
    )(cond, w_ada, b_ada.reshape(depth, 1, 3 * D_MODEL))


def _in_proj_kernel(x_ref, shift_ref, scale_ref, nw_ref, w_ref, xbc_ref, rest_ref):
    x = x_ref[...]
    xn = x * lax.rsqrt(jnp.mean(x * x, axis=-1, keepdims=True) + EPS) * nw_ref[...]
    h = (xn * (1.0 + scale_ref[...]) + shift_ref[...]).astype(BF16)
    xbc_ref[...] = _dot(h, w_ref[:, :SSD_CONV_DIM])
    rest_ref[...] = _dot(h, w_ref[:, SSD_CONV_DIM:])


def _in_proj(x, shift, scale, norm_w, w_in_p, mod_row):
    n = x.shape[0]
    return pl.pallas_call(
        _in_proj_kernel,
        grid=(n // IN_TILE,),
        in_specs=[
            pl.BlockSpec((IN_TILE, D_MODEL), lambda i: (i, 0)),
            pl.BlockSpec((None, 1, D_MODEL), lambda i: (mod_row(i), 0, 0)),
            pl.BlockSpec((None, 1, D_MODEL), lambda i: (mod_row(i), 0, 0)),
            pl.BlockSpec((1, D_MODEL), lambda i: (0, 0)),
            pl.BlockSpec((D_MODEL, D_PROJ), lambda i: (0, 0)),
        ],
        out_specs=[pl.BlockSpec((IN_TILE, SSD_CONV_DIM), lambda i: (i, 0)),
                   pl.BlockSpec((IN_TILE, D_REST), lambda i: (i, 0))],
        out_shape=[jax.ShapeDtypeStruct((n, SSD_CONV_DIM), F32), jax.ShapeDtypeStruct((n, D_REST), F32)],
        compiler_params=pltpu.CompilerParams(dimension_semantics=("arbitrary",),
                                             vmem_limit_bytes=VMEM_LIMIT_BYTES),
        name="in_proj",
    )(x, shift, scale, norm_w, w_in_p)


def _conv_silu(main, prev, nxt, conv_w, conv_b):
    t = main.shape[0]
    up = jnp.concatenate([prev, main, nxt], axis=0)
    rows = t + 2 * HALO
    pad = (SSD_CONV - 1) // 2
    acc = conv_b
    for k in range(SSD_CONV):
        off = k - pad
        shifted = up if off == 0 else pltpu.roll(up, (rows - off) % rows, 0)
        acc = acc + conv_w[k:k + 1, :] * shifted[HALO:HALO + t, :]
    return _silu(acc)


def _ssd_direction(xs, bc, dt_small, a_small_row, ht_ref, rev, direction):
    t = xs.shape[0]
    q = SSD_CHUNK
    hpg = SSD_HEADS // SSD_GROUPS
    gw = hpg * SSD_HEAD_DIM
    er = _iota2((LANES, SSD_WIDTH), 0)
    ec = _iota2((LANES, SSD_WIDTH), 1)
    expand = (er == (ec >> 6) + SSD_HEADS * direction).astype(BF16)
    a_small = dt_small * a_small_row
    ri = _iota2((q, q), 0)
    ci = _iota2((q, q), 1)
    tri_incl = (ri >= ci).astype(BF16)
    keep = (ri <= ci) if rev else (ri >= ci)
    lane_lo = _iota2((q, LANES), 1) < SSD_HEAD_DIM
    ys = [None] * (t // q)
    order = range(t // q - 1, -1, -1) if rev else range(t // q)
    for c in order:
        sl = slice(c * q, (c + 1) * q)
        a_s = a_small[sl]
        dts = dt_small[sl]
        cs_s = _dot_split_rhs(tri_incl, a_s, 3)
        tot_s = cs_s[q - 1:q, :]
        pos_s = (cs_s - a_s) if rev else cs_s
        if rev:
            fac_s = jnp.exp(tot_s - pos_s)
            wgt_s = jnp.exp(pos_s)
        else:
            fac_s = jnp.exp(pos_s)
            wgt_s = jnp.exp(tot_s - pos_s)
        both = _dot(jnp.concatenate([dts, dts * wgt_s], axis=0).astype(BF16), expand)
        dec_state = _dot_split_lhs(jnp.broadcast_to(jnp.exp(tot_s), (SUBLANES, LANES)), expand, 3)[0:1, :]
        xc = xs[sl]
        xdt = (xc * both[:q]).astype(BF16)
        xw = (xc * both[q:]).astype(BF16)
        pos_t = pos_s.T
        y_groups = []
        for g in range(SSD_GROUPS):
            bg = bc[sl, g * SSD_STATE:(g + 1) * SSD_STATE]
            cg = bc[sl, SSD_BC + g * SSD_STATE:SSD_BC + (g + 1) * SSD_STATE]
            cg32 = cg.astype(F32)
            gmat = _dot_nt(cg, bg)
            ht = ht_ref[g]
            htb = ht.astype(BF16)
            glanes = slice(g * gw, (g + 1) * gw)
            pairs = []
            for j in range(hpg // 2):
                plo = g * gw + j * LANES
                rhs = jnp.concatenate([xdt[:, plo:plo + LANES], htb[:, j * LANES:(j + 1) * LANES]], axis=0)
                res = []
                for hh in range(2):
                    lane = SSD_HEADS * direction + g * hpg + 2 * j + hh
                    col = pos_s[:, lane:lane + 1]
                    row = pos_t[lane:lane + 1, :]
                    seg = (row - col) if rev else (col - row)
                    decay = jnp.where(keep, jnp.exp(jnp.where(keep, seg, 0.0)), 0.0)
                    lhs = jnp.concatenate([(gmat * decay).astype(BF16),
                                           (cg32 * fac_s[:, lane:lane + 1]).astype(BF16)], axis=1)
                    res.append(_dot(lhs, rhs))
                pairs.append(jnp.where(lane_lo, res[0], res[1]))
            y_groups.append(jnp.concatenate(pairs, axis=1))
            ht_ref[g] = ht * dec_state[:, glanes] + _dot_tn(bg, xw[:, glanes])
        ys[c] = jnp.concatenate(y_groups, axis=1)
    return jnp.concatenate(ys, axis=0)


def _gla_direction(qh, kh, vh, glog, s_ref, rev):
    t = qh.shape[0]
    q = GLA_CHUNK
    pair = 2 * q
    ri = _iota2((t, t), 0)
    ci = _iota2((t, t), 1)
    order_ok = (ri <= ci) if rev else (ri >= ci)
    prefix = (((ri >> 6) == (ci >> 6)) & order_ok).astype(BF16)
    gc = _dot_split_rhs(prefix, glog, 3)
    q_in = qh * jnp.exp(gc)
    k_in = kh * jnp.exp(-gc)
    pr = _iota2((pair, GLA_HEADS * pair), 0)
    pc = _iota2((pair, GLA_HEADS * pair), 1) & (pair - 1)
    causal = ((pr >> 6) == (pc >> 6)) & ((pc >= pr) if rev else (pc <= pr))
    khead = _iota2((GLA_KEY_DIM, pair), 0) >> 5
    vhead_rows = _iota2((GLA_HEADS * pair, GLA_WIDTH), 0) >> 7
    vhead_cols = _iota2((GLA_HEADS * pair, GLA_WIDTH), 1) >> 6
    blockdiag = (_iota2((GLA_KEY_DIM, GLA_WIDTH), 0) >> 5) == (_iota2((GLA_KEY_DIM, GLA_WIDTH), 1) >> 6)
    outs = [None] * (t // q)
    pair_order = range(t // pair - 1, -1, -1) if rev else range(t // pair)
    for pi in pair_order:
        psl = slice(pi * pair, (pi + 1) * pair)
        k_t = k_in[psl].T
        gc_t = gc[psl].T
        kbd = jnp.concatenate([jnp.where(khead == h, k_t, 0.0) for h in range(GLA_HEADS)], axis=1).astype(BF16)
        att = _dot(q_in[psl].astype(BF16), kbd)
        att = jnp.where(causal, att, 0.0).astype(BF16)
        vp = vh[psl].astype(BF16)
        vbd = jnp.where(vhead_rows == vhead_cols, jnp.concatenate([vp] * GLA_HEADS, axis=0), 0.0)
        o_intra = _dot(att, vbd)
        chunk_order = (1, 0) if rev else (0, 1)
        for cc in chunk_order:
            c = 2 * pi + cc
            sl = slice(c * q, (c + 1) * q)
            edge = cc * q if rev else cc * q + q - 1
            egl_col = jnp.exp(gc_t[:, edge:edge + 1])
            st = s_ref[...]
            o_inter = _dot(q_in[sl].astype(BF16), st.astype(BF16))
            outs[c] = o_intra[cc * q:(cc + 1) * q] + o_inter
            kw_t = (k_t[:, cc * q:(cc + 1) * q] * egl_col).astype(BF16)
            upd = _dot(kw_t, vh[sl].astype(BF16))
            s_ref[...] = st * egl_col + jnp.where(blockdiag, upd, 0.0)
    return jnp.concatenate(outs, axis=0)


def _head_rms(x, w_row):
    n = x.shape[1]
    ones = ((_iota2((n, n), 0) >> 6) == (_iota2((n, n), 1) >> 6)).astype(BF16)
    ms = _dot_split_lhs(x * x, ones, 2) * (1.0 / HEAD_DIM)
    return x * lax.rsqrt(ms + EPS) * w_row


def _rope(x, cos, sin_signed):
    n = x.shape[1]
    quarter = HEAD_DIM // 4
    first = (_iota2((1, n), 1) & (2 * quarter - 1)) < quarter
    swapped = jnp.where(first, pltpu.roll(x, n - quarter, 1), pltpu.roll(x, quarter, 1))
    return x * cos + swapped * sin_signed


def _softmax_sink_attend(s2, vcat, sink2_col):
    m = jnp.maximum(jnp.max(s2, axis=-1, keepdims=True), sink2_col)
    p = jnp.exp2(s2 - m)
    den = jnp.sum(p, axis=-1, keepdims=True) + jnp.exp2(sink2_col - m)
    return _dot(p.astype(BF16), vcat) / den


def _gla_inputs(rest_ref, gkw_ref, gkb_ref):
    qh = rest_ref[:, P_GQ:P_GQ + GLA_KEY_DIM] * (GLA_DK ** -0.5)
    kh = rest_ref[:, P_GK:P_GK + GLA_KEY_DIM]
    vh = rest_ref[:, P_GV:P_GV + GLA_WIDTH]
    lr_hi, lr_lo = _split(rest_ref[:, P_GLR:P_GLR + LANES], 2)
    w_hi, w_lo = _split(gkw_ref[...], 2)
    gate = _dot(lr_hi, w_hi) + _dot(lr_lo, w_hi) + _dot(lr_hi, w_lo) + gkb_ref[...]
    glog = _log_sigmoid(gate) * (1.0 / GLA_GATE_NORM)
    return qh, kh, vh, glog


def _load_states(ht_ref, s_ref, ht0_ref, s0_ref):
    hpg = SSD_HEADS // SSD_GROUPS
    for g in range(SSD_GROUPS):
        blk = ht0_ref[g * hpg:(g + 1) * hpg].reshape(hpg * SSD_HEAD_DIM, SSD_STATE)
        ht_ref[g] = blk.T
    rows = []
    for h in range(GLA_HEADS):
        pieces = []
        if h > 0:
            pieces.append(jnp.zeros((GLA_DK, h * GLA_DV), F32))
        pieces.append(s0_ref[h])
        if h < GLA_HEADS - 1:
            pieces.append(jnp.zeros((GLA_DK, (GLA_HEADS - 1 - h) * GLA_DV), F32))
        rows.append(jnp.concatenate(pieces, axis=1))
    s_ref[...] = jnp.concatenate(rows, axis=0)


def _store_states(ht_ref, s_ref, htf_ref, sf_ref):
    hpg = SSD_HEADS // SSD_GROUPS
    for g in range(SSD_GROUPS):
        htf_ref[g * hpg:(g + 1) * hpg] = ht_ref[g].T.reshape(hpg, SSD_HEAD_DIM, SSD_STATE)
    st = s_ref[...]
    for h in range(GLA_HEADS):
        sf_ref[h] = st[h * GLA_DK:(h + 1) * GLA_DK, h * GLA_DV:(h + 1) * GLA_DV]


HT_SHAPE = (SSD_GROUPS, SSD_STATE, SSD_WIDTH // SSD_GROUPS)
S_SHAPE = (GLA_KEY_DIM, GLA_WIDTH)
SSD_STATE_SHAPE = (SSD_HEADS, SSD_HEAD_DIM, SSD_STATE)
GLA_STATE_SHAPE = (GLA_HEADS, GLA_DK, GLA_DV)


def _mix_bwd_kernel(*refs, is_ctx, n_tiles):
    it = iter(refs)
    xbc_ref = next(it)
    xprev_ref = next(it)
    xnext_ref = next(it)
    rest_ref = next(it)
    convw_ref = next(it)
    convb_ref = next(it)
    dtb_ref = next(it)
    alog_ref = next(it)
    gkw_ref = next(it)
    gkb_ref = next(it)
    if not is_ctx:
        ht0_ref = next(it)
        s0_ref = next(it)
    yb_ref = next(it)
    ob_ref = next(it)
    xs_ref = next(it)
    bc_ref = next(it)
    dts_ref = next(it)
    if is_ctx:
        htf_ref = next(it)
        sf_ref = next(it)
    ht_ref = next(it)
    s_ref = next(it)

    step = pl.program_id(1)
    tile_idx = n_tiles - 1 - step

    @pl.when(step == 0)
    def _():
        if is_ctx:
            ht_ref[...] = jnp.zeros(ht_ref.shape, F32)
            s_ref[...] = jnp.zeros(s_ref.shape, F32)
        else:
            _load_states(ht_ref, s_ref, ht0_ref, s0_ref)

    prev = jnp.where(tile_idx > 0, xprev_ref[...], 0.0)
    nxt = jnp.where(tile_idx < n_tiles - 1, xnext_ref[...], 0.0)
    xbc = _conv_silu(xbc_ref[...], prev, nxt, convw_ref[...], convb_ref[...])
    xs = xbc[:, :SSD_WIDTH]
    bc = xbc[:, SSD_WIDTH:].astype(BF16)
    dt_small = _softplus(rest_ref[:, P_DT:P_DT + LANES] + dtb_ref[...])
    xs_ref[...] = xs
    bc_ref[...] = bc
    dts_ref[...] = dt_small
    a_small_row = -jnp.exp(alog_ref[...])
    yb_ref[...] = _ssd_direction(xs, bc, dt_small, a_small_row, ht_ref, True, 1)
    qh, kh, vh, glog = _gla_inputs(rest_ref, gkw_ref, gkb_ref)
    ob_ref[...] = _gla_direction(qh, kh, vh, glog, s_ref, True)
    if is_ctx:
        _store_states(ht_ref, s_ref, htf_ref, sf_ref)


def _mix_bwd(xbc, rest, consts, ht0, s0, *, n_seq, n_tiles, is_ctx):
    n = xbc.shape[0]
    per = TILE // HALO
    last_halo = n // HALO - 1
    tile_of = lambda s, t: s * n_tiles + (n_tiles - 1 - t)
    row = lambda s, t: (tile_of(s, t), 0)
    const2 = lambda s, t: (0, 0)
    in_specs = [
        pl.BlockSpec((TILE, SSD_CONV_DIM), row),
        pl.BlockSpec((HALO, SSD_CONV_DIM), lambda s, t: (jnp.maximum(tile_of(s, t) * per - 1, 0), 0)),
        pl.BlockSpec((HALO, SSD_CONV_DIM), lambda s, t: (jnp.minimum((tile_of(s, t) + 1) * per, last_halo), 0)),
        pl.BlockSpec((TILE, BWD_COLS), row),
        pl.BlockSpec((SUBLANES, SSD_CONV_DIM), const2),
        pl.BlockSpec((1, SSD_CONV_DIM), const2),
        pl.BlockSpec((1, LANES), const2),
        pl.BlockSpec((1, LANES), const2),
        pl.BlockSpec((LANES, GLA_KEY_DIM), const2),
        pl.BlockSpec((1, GLA_KEY_DIM), const2),
    ]
    args = [xbc, xbc, xbc, rest, consts["conv_w"], consts["conv_b"], consts["dt_bias"], consts["a_log_small"],
            consts["gk_w"][1], consts["gk_b"][1]]
    if not is_ctx:
        in_specs += [pl.BlockSpec((None,) + SSD_STATE_SHAPE, lambda s, t: (s, 0, 0, 0)),
                     pl.BlockSpec((None,) + GLA_STATE_SHAPE, lambda s, t: (s, 0, 0, 0))]
        args += [ht0, s0]
    out_specs = [
        pl.BlockSpec((TILE, SSD_WIDTH), row),
        pl.BlockSpec((TILE, GLA_WIDTH), row),
        pl.BlockSpec((TILE, SSD_WIDTH), row),
        pl.BlockSpec((TILE, 2 * SSD_BC), row),
        pl.BlockSpec((TILE, LANES), row),
    ]
    out_shape = [jax.ShapeDtypeStruct((n, SSD_WIDTH), F32), jax.ShapeDtypeStruct((n, GLA_WIDTH), F32),
                 jax.ShapeDtypeStruct((n, SSD_WIDTH), F32), jax.ShapeDtypeStruct((n, 2 * SSD_BC), BF16),
                 jax.ShapeDtypeStruct((n, LANES), F32)]
    if is_ctx:
        out_specs += [pl.BlockSpec((None,) + SSD_STATE_SHAPE, lambda s, t: (s, 0, 0, 0)),
                      pl.BlockSpec((None,) + GLA_STATE_SHAPE, lambda s, t: (s, 0, 0, 0))]
        out_shape += [jax.ShapeDtypeStruct((n_seq,) + SSD_STATE_SHAPE, F32),
                      jax.ShapeDtypeStruct((n_seq,) + GLA_STATE_SHAPE, F32)]
    return pl.pallas_call(
        functools.partial(_mix_bwd_kernel, is_ctx=is_ctx, n_tiles=n_tiles),
        grid=(n_seq, n_tiles),
        in_specs=in_specs,
        out_specs=out_specs,
        out_shape=out_shape,
        scratch_shapes=[pltpu.VMEM(HT_SHAPE, F32), pltpu.VMEM(S_SHAPE, F32)],
        compiler_params=pltpu.CompilerParams(dimension_semantics=("arbitrary", "arbitrary"),
                                             vmem_limit_bytes=VMEM_LIMIT_BYTES),
        name="mix_bwd_ctx" if is_ctx else "mix_bwd_lat",
    )(*args)


def _mix_fwd_kernel(*refs, is_ctx, n_tiles):
    it = iter(refs)
    sink_ref = next(it)
    x_ref = next(it)
    gate_ref = next(it)
    rest_ref = next(it)
    xs_ref = next(it)
    bc_ref = next(it)
    dts_ref = next(it)
    yb_ref = next(it)
    ob_ref = next(it)
    alog_ref = next(it)
    gkw_ref = next(it)
    gkb_ref = next(it)
    dskip_ref = next(it)
    ssdnw_ref = next(it)
    glanw_ref = next(it)
    qnw_ref = next(it)
    knw_ref = next(it)
    wout_ref = next(it)
    if not is_ctx:
        ht0_ref = next(it)
        s0_ref = next(it)
        kvprev_ref = next(it)
        kvnext_ref = next(it)
        cos_ref = next(it)
        sin_ref = next(it)
        cosp_ref = next(it)
        sinp_ref = next(it)
        cosn_ref = next(it)
        sinn_ref = next(it)
        kc_ref = next(it)
        vc_ref = next(it)
    o_ref = next(it)
    if is_ctx:
        kout_ref = next(it)
        vout_ref = next(it)
        htf_ref = next(it)
        sf_ref = next(it)
    ht_ref = next(it)
    s_ref = next(it)

    tile_idx = pl.program_id(1)

    @pl.when(tile_idx == 0)
    def _():
        if is_ctx:
            ht_ref[...] = jnp.zeros(ht_ref.shape, F32)
            s_ref[...] = jnp.zeros(s_ref.shape, F32)
        else:
            _load_states(ht_ref, s_ref, ht0_ref, s0_ref)

    t = TILE
    xs = xs_ref[...]
    a_small_row = -jnp.exp(alog_ref[...])

    y_f = _ssd_direction(xs, bc_ref[...], dts_ref[...], a_small_row, ht_ref, False, 0)
    y = y_f + yb_ref[...] + xs * dskip_ref[...]
    y = y * _silu(rest_ref[:, P_Z:P_Z + SSD_WIDTH])
    y_ssd = y * lax.rsqrt(jnp.mean(y * y, axis=-1, keepdims=True) + EPS) * ssdnw_ref[...]

    qh, kh, vh, glog = _gla_inputs(rest_ref, gkw_ref, gkb_ref)
    o = _gla_direction(qh, kh, vh, glog, s_ref, False) + ob_ref[...]
    y_gla = _head_rms(o, glanw_ref[...]) * _silu(rest_ref[:, P_GG:P_GG + GLA_WIDTH])

    qn = _head_rms(rest_ref[:, P_AQ:P_AQ + ATT_WIDTH], qnw_ref[...])
    kn = _head_rms(rest_ref[:, P_AK:P_AK + ATT_KV_DIM], knw_ref[...])
    vv = rest_ref[:, P_AV:P_AV + ATT_KV_DIM]
    qscale = (HEAD_DIM ** -0.5) * LOG2E
    head_out = [None] * ATT_HEADS
    if is_ctx:
        kout_ref[...] = kn
        vout_ref[...] = vv
        qb = (qn * qscale).astype(BF16)
        kb = kn.astype(BF16)
        vb = vv.astype(BF16)
        for kvh in range(ATT_KV_HEADS):
            hl = slice(kvh * HEAD_DIM, (kvh + 1) * HEAD_DIM)
            qg = jnp.concatenate(
                [qb[:, (kvh * ATT_GROUP + g) * HEAD_DIM:(kvh * ATT_GROUP + g + 1) * HEAD_DIM] for g in range(ATT_GROUP)],
                axis=0)
            sink2 = jnp.concatenate(
                [jnp.full((t, 1), sink_ref[kvh * ATT_GROUP + g] * LOG2E, F32) for g in range(ATT_GROUP)], axis=0)
            og = _softmax_sink_attend(_dot_nt(qg, kb[:, hl]), vb[:, hl], sink2)
            for g in range(ATT_GROUP):
                head_out[kvh * ATT_GROUP + g] = og[g * t:(g + 1) * t]
        y_att = jnp.concatenate(head_out, axis=1)
    else:
        cos = cos_ref[...]
        sin = sin_ref[...]
        qr = _rope(qn, jnp.concatenate([cos, cos], axis=1), jnp.concatenate([sin, sin], axis=1))
        qb = (qr * qscale).astype(BF16)
        k_halo_p = _rope(_head_rms(kvprev_ref[:, :ATT_KV_DIM], knw_ref[...]), cosp_ref[...], sinp_ref[...])
        k_halo_n = _rope(_head_rms(kvnext_ref[:, :ATT_KV_DIM], knw_ref[...]), cosn_ref[...], sinn_ref[...])
        k_loc = jnp.concatenate([k_halo_p, _rope(kn, cos, sin), k_halo_n], axis=0).astype(BF16)
        v_loc = jnp.concatenate([kvprev_ref[:, ATT_KV_DIM:], vv, kvnext_ref[:, ATT_KV_DIM:]], axis=0).astype(BF16)
        kc = kc_ref[...].astype(BF16)
        vc = vc_ref[...].astype(BF16)
        n_ctx = kc.shape[0]
        blk = ATT_BLOCK
        rows = ATT_GROUP * blk
        qi = _iota2((rows, blk), 0) & (blk - 1)
        kj = _iota2((rows, blk), 1)
        n_blk = t // blk
        blocks = []
        for nb in range(n_blk):
            off_p = jnp.where(tile_idx > 0, 0, 2 * blk) if nb == 0 else 0
            off_n = jnp.where(tile_idx < n_tiles - 1, 0, 2 * blk) if nb == n_blk - 1 else 0
            mask_p = kj >= qi + off_p
            mask_n = kj <= qi - off_n
            per_head = [None] * ATT_HEADS
            for kvh in range(ATT_KV_HEADS):
                hl = slice(kvh * HEAD_DIM, (kvh + 1) * HEAD_DIM)
                kcat = jnp.concatenate([kc[:, hl], k_loc[nb * blk:(nb + 3) * blk, hl]], axis=0)
                vcat = jnp.concatenate([vc[:, hl], v_loc[nb * blk:(nb + 3) * blk, hl]], axis=0)
                qg = jnp.concatenate(
                    [qb[nb * blk:(nb + 1) * blk, (kvh * ATT_GROUP + g) * HEAD_DIM:(kvh * ATT_GROUP + g + 1) * HEAD_DIM]
                     for g in range(ATT_GROUP)], axis=0)
                sink2 = jnp.concatenate(
                    [jnp.full((blk, 1), sink_ref[kvh * ATT_GROUP + g] * LOG2E, F32) for g in range(ATT_GROUP)], axis=0)
                s2 = _dot_nt(qg, kcat)
                s2 = jnp.concatenate([s2[:, :n_ctx + 0 * blk],
                                      jnp.where(mask_p, s2[:, n_ctx:n_ctx + blk], NEG_INF),
                                      s2[:, n_ctx + blk:n_ctx + 2 * blk],
                                      jnp.where(mask_n, s2[:, n_ctx + 2 * blk:], NEG_INF)], axis=1)
                og = _softmax_sink_attend(s2, vcat, sink2)
                for g in range(ATT_GROUP):
                    per_head[kvh * ATT_GROUP + g] = og[g * blk:(g + 1) * blk]
            blocks.append(jnp.concatenate(per_head, axis=1))
        y_att = jnp.concatenate(blocks, axis=0)
    y_att = y_att * _silu(rest_ref[:, P_AG:P_AG + ATT_WIDTH])

    ycat = jnp.concatenate([y_ssd, y_gla, y_att], axis=1).astype(BF16)
    o_ref[...] = x_ref[...] + gate_ref[...] * _dot(ycat, wout_ref[...])
    if is_ctx:
        _store_states(ht_ref, s_ref, htf_ref, sf_ref)


def _mix_fwd(x, gate, gate_row, rest, bwd, consts, sink, w_out, lat, *, n_seq, n_tiles, is_ctx):
    n = x.shape[0]
    yb, ob, xs, bc, dts = bwd
    tile_of = lambda s, t: s * n_tiles + t
    const2 = lambda s, t: (0, 0)
    row = lambda s, t: (tile_of(s, t), 0)
    in_specs = [
        pl.BlockSpec(memory_space=pltpu.SMEM),
        pl.BlockSpec((TILE, D_MODEL), row),
        pl.BlockSpec((None, 1, D_MODEL), lambda s, t: (gate_row(s), 0, 0)),
        pl.BlockSpec((TILE, D_REST), row),
        pl.BlockSpec((TILE, SSD_WIDTH), row),
        pl.BlockSpec((TILE, 2 * SSD_BC), row),
        pl.BlockSpec((TILE, LANES), row),
        pl.BlockSpec((TILE, SSD_WIDTH), row),
        pl.BlockSpec((TILE, GLA_WIDTH), row),
        pl.BlockSpec((1, LANES), const2),
        pl.BlockSpec((LANES, GLA_KEY_DIM), const2),
        pl.BlockSpec((1, GLA_KEY_DIM), const2),
        pl.BlockSpec((1, SSD_WIDTH), const2),
        pl.BlockSpec((1, SSD_WIDTH), const2),
        pl.BlockSpec((1, GLA_WIDTH), const2),
        pl.BlockSpec((1, ATT_WIDTH), const2),
        pl.BlockSpec((1, ATT_KV_DIM), const2),
        pl.BlockSpec((D_MIX, D_MODEL), const2),
    ]
    args = [sink, x, gate, rest, xs, bc, dts, yb, ob, consts["a_log_small"], consts["gk_w"][0], consts["gk_b"][0],
            consts["d_skip"], consts["ssd_norm_w"], consts["gla_norm_w"], consts["q_norm_w"], consts["k_norm_w"], w_out]
    out_specs = [pl.BlockSpec((TILE, D_MODEL), row)]
    out_shape = [jax.ShapeDtypeStruct((n, D_MODEL), F32)]
    if is_ctx:
        out_specs += [pl.BlockSpec((TILE, ATT_KV_DIM), row), pl.BlockSpec((TILE, ATT_KV_DIM), row),
                      pl.BlockSpec((None,) + SSD_STATE_SHAPE, lambda s, t: (s, 0, 0, 0)),
                      pl.BlockSpec((None,) + GLA_STATE_SHAPE, lambda s, t: (s, 0, 0, 0))]
        out_shape += [jax.ShapeDtypeStruct((n, ATT_KV_DIM), F32), jax.ShapeDtypeStruct((n, ATT_KV_DIM), F32),
                      jax.ShapeDtypeStruct((n_seq,) + SSD_STATE_SHAPE, F32),
                      jax.ShapeDtypeStruct((n_seq,) + GLA_STATE_SHAPE, F32)]
    else:
        per_blk = TILE // ATT_BLOCK
        kv_col = P_AK // (2 * ATT_KV_DIM)
        last_blk = n // ATT_BLOCK - 1
        kvprev = lambda s, t: (jnp.maximum(tile_of(s, t) * per_blk - 1, 0), kv_col)
        kvnext = lambda s, t: (jnp.minimum((tile_of(s, t) + 1) * per_blk, last_blk), kv_col)
        last_pos_blk = n_tiles * per_blk - 1
        pos_main = lambda s, t: (t, 0)
        pos_prev = lambda s, t: (jnp.maximum(t * per_blk - 1, 0), 0)
        pos_next = lambda s, t: (jnp.minimum((t + 1) * per_blk, last_pos_blk), 0)
        n_ctx = lat["k_ctx"].shape[1]
        in_specs += [
            pl.BlockSpec((None,) + SSD_STATE_SHAPE, lambda s, t: (s, 0, 0, 0)),
            pl.BlockSpec((None,) + GLA_STATE_SHAPE, lambda s, t: (s, 0, 0, 0)),
            pl.BlockSpec((ATT_BLOCK, 2 * ATT_KV_DIM), kvprev),
            pl.BlockSpec((ATT_BLOCK, 2 * ATT_KV_DIM), kvnext),
            pl.BlockSpec((TILE, ATT_KV_DIM), pos_main),
            pl.BlockSpec((TILE, ATT_KV_DIM), pos_main),
            pl.BlockSpec((ATT_BLOCK, ATT_KV_DIM), pos_prev),
            pl.BlockSpec((ATT_BLOCK, ATT_KV_DIM), pos_prev),
            pl.BlockSpec((ATT_BLOCK, ATT_KV_DIM), pos_next),
            pl.BlockSpec((ATT_BLOCK, ATT_KV_DIM), pos_next),
            pl.BlockSpec((None, n_ctx, ATT_KV_DIM), lambda s, t: (s, 0, 0)),
            pl.BlockSpec((None, n_ctx, ATT_KV_DIM), lambda s, t: (s, 0, 0)),
        ]
        args += [lat["ht0"], lat["s0"], rest, rest, lat["cos"], lat["sin"], lat["cos"], lat["sin"], lat["cos"],
                 lat["sin"], lat["k_ctx"], lat["v_ctx"]]
    return pl.pallas_call(
        functools.partial(_mix_fwd_kernel, is_ctx=is_ctx, n_tiles=n_tiles),
        grid=(n_seq, n_tiles),
        in_specs=in_specs,
        out_specs=out_specs,
        out_shape=out_shape,
        scratch_shapes=[pltpu.VMEM(HT_SHAPE, F32), pltpu.VMEM(S_SHAPE, F32)],
        compiler_params=pltpu.CompilerParams(dimension_semantics=("arbitrary", "arbitrary"),
                                             vmem_limit_bytes=VMEM_LIMIT_BYTES),
        name="mix_fwd_ctx" if is_ctx else "mix_fwd_lat",
    )(*args)


def _pad_cols(a, width):
    return jnp.pad(a, [(0, 0)] * (a.ndim - 1) + [(0, width - a.shape[-1])])


def _prep_w_in(w_in):
    sizes = (SSD_WIDTH, SSD_CONV_DIM, 2 * SSD_HEADS, GLA_KEY_DIM, GLA_KEY_DIM, GLA_WIDTH, GLA_WIDTH, 2 * GLA_LOWRANK,
             ATT_WIDTH, ATT_KV_DIM, ATT_KV_DIM, ATT_WIDTH)
    parts, start = [], 0
    for size in sizes:
        parts.append(w_in[..., start:start + size])
        start += size
    z, xbc, dt, gq, gk, gv, gg, glr, aq, ak, av, ag = parts
    cols = [xbc, _pad_cols(dt, LANES), gq, gk, _pad_cols(glr, LANES), gv, ak, av, z, gg, aq, ag]
    return jnp.concatenate(cols, axis=-1).astype(BF16)


def _layer_consts(l, conv_w, conv_b, ssd_a_log, ssd_dt_bias, ssd_d, ssd_norm_w, gla_gk_up, gla_gk_b, gla_norm_w,
                  attn_q_norm, attn_k_norm):
    gk_w = []
    for d in range(2):
        w = jnp.zeros((LANES, GLA_KEY_DIM), F32)
        gk_w.append(w.at[d * GLA_LOWRANK:(d + 1) * GLA_LOWRANK].set(gla_gk_up[l, d]))
    return {
        "conv_w": jnp.pad(conv_w[l], ((0, SUBLANES - SSD_CONV), (0, 0))),
        "conv_b": conv_b[l].reshape(1, SSD_CONV_DIM),
        "dt_bias": _pad_cols(ssd_dt_bias[l].reshape(1, 2 * SSD_HEADS), LANES),
        "a_log_small": _pad_cols(ssd_a_log[l].reshape(1, 2 * SSD_HEADS), LANES),
        "gk_w": gk_w,
        "gk_b": [gla_gk_b[l, d].reshape(1, GLA_KEY_DIM) for d in range(2)],
        "d_skip": jnp.repeat(ssd_d[l], SSD_HEAD_DIM).reshape(1, SSD_WIDTH),
        "ssd_norm_w": ssd_norm_w[l].reshape(1, SSD_WIDTH),
        "gla_norm_w": jnp.tile(gla_norm_w[l], GLA_HEADS).reshape(1, GLA_WIDTH),
        "q_norm_w": jnp.tile(attn_q_norm[l], ATT_HEADS).reshape(1, ATT_WIDTH),
        "k_norm_w": jnp.tile(attn_k_norm[l], ATT_KV_HEADS).reshape(1, ATT_KV_DIM),
    }


def _rope_tables(seq_len):
    quarter = HEAD_DIM // 4
    rows = seq_len // GRID_W
    row_pos = jnp.repeat(jnp.arange(rows, dtype=F32), GRID_W)
    col_pos = jnp.tile(jnp.arange(GRID_W, dtype=F32), rows)
    inv = ROPE_THETA ** (-jnp.arange(quarter, dtype=F32) / quarter)
    ang_r = row_pos[:, None] * inv[None, :]
    ang_c = col_pos[:, None] * inv[None, :]
    cos = jnp.concatenate([jnp.cos(ang_r), jnp.cos(ang_r), jnp.cos(ang_c), jnp.cos(ang_c)], axis=1)
    sin = jnp.concatenate([-jnp.sin(ang_r), jnp.sin(ang_r), -jnp.sin(ang_c), jnp.sin(ang_c)], axis=1)
    return jnp.tile(cos, (1, ATT_KV_HEADS)), jnp.tile(sin, (1, ATT_KV_HEADS))


def kernel(x_prompt, x_sample, c, cache_k, cache_v, state_ssd, state_gla, c_ctx, w_ada, b_ada, norm_w, w_in, conv_w,
           conv_b, ssd_a_log, ssd_dt_bias, ssd_d, ssd_norm_w, gla_gk_up, gla_gk_b, gla_norm_w, attn_q_norm, attn_k_norm,
           attn_sink, w_out):
    batch, seq, _ = x_prompt.shape
    dec_batch, dec_seq, _ = x_sample.shape
    depth = w_in.shape[0]
    past = cache_k.shape[2]
    assert seq % TILE == 0 and dec_seq % TILE == 0
    assert (batch * seq) % IN_TILE == 0 and dec_seq % IN_TILE == 0

    rows = -(-(dec_batch + 1) // SUBLANES) * SUBLANES
    cond = jnp.zeros((rows, D_MODEL), F32).at[:dec_batch].set(c).at[dec_batch].set(c_ctx)
    mod = _modulation(cond, w_ada, b_ada)

    w_in_p = _prep_w_in(w_in)
    w_out_b = w_out.astype(BF16)
    cos_t, sin_t = _rope_tables(dec_seq)

    x_ctx = x_prompt.reshape(batch * seq, D_MODEL)
    x_lat = x_sample.reshape(dec_batch * dec_seq, D_MODEL)
    ctx_tiles = seq // TILE
    lat_tiles = dec_seq // TILE
    lat_in_per_seq = dec_seq // IN_TILE

    ks_out, vs_out, ssd_out, gla_out = [], [], [], []
    for l in range(depth):
        consts = _layer_consts(l, conv_w, conv_b, ssd_a_log, ssd_dt_bias, ssd_d, ssd_norm_w, gla_gk_up, gla_gk_b,
                               gla_norm_w, attn_q_norm, attn_k_norm)
        shift = mod[l, :, :D_MODEL].reshape(rows, 1, D_MODEL)
        scale = mod[l, :, D_MODEL:2 * D_MODEL].reshape(rows, 1, D_MODEL)
        gate = mod[l, :, 2 * D_MODEL:].reshape(rows, 1, D_MODEL)
        nw = norm_w[l].reshape(1, D_MODEL)

        xbc, rest = _in_proj(x_ctx, shift, scale, nw, w_in_p[l], lambda i: dec_batch)
        *bwd, ht_b, s_b = _mix_bwd(xbc, rest, consts, None, None, n_seq=batch, n_tiles=ctx_tiles, is_ctx=True)
        x_ctx, k_l, v_l, ht_f, s_f = _mix_fwd(x_ctx, gate, lambda s: dec_batch, rest, bwd, consts, attn_sink[l],
                                              w_out_b[l], None, n_seq=batch, n_tiles=ctx_tiles, is_ctx=True)
        ks_out.append(k_l.reshape(batch, seq, ATT_KV_HEADS, HEAD_DIM))
        vs_out.append(v_l.reshape(batch, seq, ATT_KV_HEADS, HEAD_DIM))
        ssd_out.append(jnp.stack([ht_f, ht_b], axis=1))
        gla_out.append(jnp.stack([s_f, s_b], axis=1))

        xbc, rest = _in_proj(x_lat, shift, scale, nw, w_in_p[l], lambda i: i // lat_in_per_seq)
        bwd = _mix_bwd(xbc, rest, consts, state_ssd[:, l, 1], state_gla[:, l, 1], n_seq=dec_batch, n_tiles=lat_tiles,
                       is_ctx=False)
        lat = {"ht0": state_ssd[:, l, 0], "s0": state_gla[:, l, 0], "cos": cos_t, "sin": sin_t,
               "k_ctx": cache_k[:, l].reshape(dec_batch, past, ATT_KV_DIM),
               "v_ctx": cache_v[:, l].reshape(dec_batch, past, ATT_KV_DIM)}
        (x_lat,) = _mix_fwd(x_lat, gate, lambda s: s, rest, bwd, consts, attn_sink[l], w_out_b[l], lat,
                            n_seq=dec_batch, n_tiles=lat_tiles, is_ctx=False)

    return (x_ctx.reshape(batch, seq, D_MODEL), x_lat.reshape(dec_batch, dec_seq, D_MODEL),
            jnp.stack(ks_out, axis=1), jnp.stack(vs_out, axis=1), jnp.stack(ssd_out, axis=1),
            jnp.stack(gla_out, axis=1))
```

```python
import functools
import math

import jax
import jax.numpy as jnp
from jax import lax
from jax.experimental import pallas as pl
from jax.experimental.pallas import tpu as pltpu

D_MODEL = 1024
GRID_W = 64
SSD_HEADS = 8
SSD_HEAD_DIM = 64
SSD_WIDTH = 512
SSD_GROUPS = 2
SSD_STATE = 128
SSD_BC = SSD_GROUPS * SSD_STATE
SSD_CONV = 5
SSD_CONV_DIM = SSD_WIDTH + 2 * SSD_BC
SSD_CHUNK = 128
GLA_HEADS = 4
GLA_DK = 32
GLA_DV = 64
GLA_KEY_DIM = GLA_HEADS * GLA_DK
GLA_WIDTH = GLA_HEADS * GLA_DV
GLA_LOWRANK = 16
GLA_GATE_NORM = 16.0
GLA_CHUNK = 64
ATT_HEADS = 4
ATT_KV_HEADS = 2
ATT_GROUP = ATT_HEADS // ATT_KV_HEADS
HEAD_DIM = 64
ATT_WIDTH = ATT_HEADS * HEAD_DIM
ATT_KV_DIM = ATT_KV_HEADS * HEAD_DIM
WINDOW = 128
ATT_BLOCK = 128
ROPE_THETA = 10000.0
D_MIX = SSD_WIDTH + GLA_WIDTH + ATT_WIDTH
EPS = 1e-6
NEG_INF = -1e30
LOG2E = math.log2(math.e)

LANES = 128
SUBLANES = 8
VMEM_LIMIT_BYTES = 56 * 1024 * 1024

P_DT = 0
P_GQ = P_DT + LANES
P_GK = P_GQ + GLA_KEY_DIM
P_GLR = P_GK + GLA_KEY_DIM
P_GV = P_GLR + LANES
P_AK = P_GV + GLA_WIDTH
P_AV = P_AK + ATT_KV_DIM
P_Z = P_AV + ATT_KV_DIM
P_GG = P_Z + SSD_WIDTH
P_AQ = P_GG + GLA_WIDTH
P_AG = P_AQ + ATT_WIDTH
D_LIN = P_AG + ATT_WIDTH
D_PROJ = SSD_CONV_DIM + D_LIN
P_GLF = D_LIN
D_REST = D_LIN + GLA_KEY_DIM
BWD_COLS = P_AK

CP_CONVW = 0
CP_CONVB = 8
CP_DTB = 9
CP_ALOG = 10
CP_GKB = 11
CP_DSKIP = 13
CP_SSDNW = 14
CP_GLANW = 15
CP_QNW = 16
CP_KNW = 17
CP_NW = 18
CP_ROWS = 24

TILE = 256
IN_TILE = 512
HALO = SUBLANES
CONV_COLS = 256
CONV_ROWS = 64

HI = lax.Precision.HIGHEST
F32 = jnp.float32
BF16 = jnp.bfloat16

NT_DIMS = (((1,), (1,)), ((), ()))
TN_DIMS = (((0,), (0,)), ((), ()))


def _dot(a, b):
    return jnp.dot(a, b, preferred_element_type=F32)


def _dot_nt(a, b):
    return lax.dot_general(a, b, NT_DIMS, preferred_element_type=F32)


def _dot_tn(a, b):
    return lax.dot_general(a, b, TN_DIMS, preferred_element_type=F32)


def _split(x, parts):
    out = []
    for _ in range(parts - 1):
        piece = x.astype(BF16)
        out.append(piece)
        x = x - piece.astype(F32)
    out.append(x.astype(BF16))
    return out


def _dot_split_lhs(x, w_bf16, parts):
    acc = None
    for piece in _split(x, parts):
        term = _dot(piece, w_bf16)
        acc = term if acc is None else acc + term
    return acc


def _dot_split_rhs(w_bf16, x, parts):
    acc = None
    for piece in _split(x, parts):
        term = _dot(w_bf16, piece)
        acc = term if acc is None else acc + term
    return acc


def _sigmoid(x):
    return 1.0 / (1.0 + jnp.exp(-x))


def _silu(x):
    return x * _sigmoid(x)


def _softplus(x):
    return jnp.maximum(x, 0.0) + jnp.log(1.0 + jnp.exp(-jnp.abs(x)))


def _log_sigmoid(x):
    return jnp.minimum(x, 0.0) - jnp.log(1.0 + jnp.exp(-jnp.abs(x)))


def _iota2(shape, dim):
    return lax.broadcasted_iota(jnp.int32, shape, dim)


def _mod_kernel(cond_ref, w_ref, b_ref, o_ref):
    o_ref[...] = jnp.dot(_silu(cond_ref[...]), w_ref[...], preferred_element_type=F32, precision=HI) + b_ref[...]


def _modulation(cond, w_ada, b_ada):
    depth = w_ada.shape[0]
    rows = cond.shape[0]
    return pl.pallas_call(
        _mod_kernel,
        grid=(depth, 3),
        in_specs=[
            pl.BlockSpec((rows, D_MODEL), lambda l, j: (0, 0)),
            pl.BlockSpec((None, D_MODEL, D_MODEL), lambda l, j: (l, 0, j)),
            pl.BlockSpec((None, 1, D_MODEL), lambda l, j: (l, 0, j)),
        ],
        out_specs=pl.BlockSpec((None, rows, D_MODEL), lambda l, j: (l, 0, j)),
        out_shape=jax.ShapeDtypeStruct((depth, rows, 3 * D_MODEL), F32),
        compiler_params=pltpu.CompilerParams(dimension_semantics=("arbitrary", "arbitrary")),
        name="modulation",
    )(cond, w_ada, b_ada.reshape(depth, 1, 3 * D_MODEL))


def _in_proj_kernel(x_ref, xprev_ref, xnext_ref, shift_ref, scale_ref, cp_ref, w_ref, gkw_ref, xs_ref, bc_ref,
                    rest_ref, *, seq_len):
    nw = cp_ref[CP_NW:CP_NW + 1, :]
    scale1 = 1.0 + scale_ref[...]
    shift = shift_ref[...]

    def modnorm(x):
        xn = x * lax.rsqrt(jnp.mean(x * x, axis=-1, keepdims=True) + EPS) * nw
        return (xn * scale1 + shift).astype(BF16)

    h = modnorm(x_ref[...])
    t = h.shape[0]
    h_up = jnp.concatenate([modnorm(xprev_ref[...]), h, modnorm(xnext_ref[...])], axis=0)
    seg = min(seq_len, t)
    pos0 = (pl.program_id(0) * t) & (seq_len - 1)
    zero_halo = jnp.zeros((HALO, CONV_COLS), F32)
    for c0 in range(0, SSD_CONV_DIM, CONV_COLS):
        up = _dot(h_up, w_ref[:, c0:c0 + CONV_COLS])
        conv_w = cp_ref[CP_CONVW:CP_CONVW + SUBLANES, c0:c0 + CONV_COLS]
        conv_b = cp_ref[CP_CONVB:CP_CONVB + 1, c0:c0 + CONV_COLS]
        for a in range(0, t, CONV_ROWS):
            if a % seq_len == 0 and seq_len <= t:
                prev = zero_halo
            elif a % seg == 0:
                prev = jnp.where(pos0 + a > 0, up[a:a + HALO, :], 0.0)
            else:
                prev = up[a:a + HALO, :]
            b = a + CONV_ROWS
            if b % seq_len == 0 and seq_len <= t:
                nxt = zero_halo
            elif b % seg == 0:
                nxt = jnp.where(pos0 + b < seq_len, up[HALO + b:2 * HALO + b, :], 0.0)
            else:
                nxt = up[HALO + b:2 * HALO + b, :]
            xbc = _conv_silu(up[HALO + a:HALO + b, :], prev, nxt, conv_w, conv_b)
            if c0 < SSD_WIDTH:
                xs_ref[a:b, c0:c0 + CONV_COLS] = xbc
            else:
                bc_ref[a:b, c0 - SSD_WIDTH:c0 - SSD_WIDTH + CONV_COLS] = xbc.astype(BF16)

    res = _dot(h, w_ref[:, SSD_CONV_DIM:])
    rest_ref[:, P_DT:P_DT + LANES] = _softplus(res[:, P_DT:P_DT + LANES] + cp_ref[CP_DTB:CP_DTB + 1, :LANES])
    rest_ref[:, P_GQ:P_GLR] = res[:, P_GQ:P_GLR]
    rest_ref[:, P_GV:D_LIN] = res[:, P_GV:]
    lr_hi, lr_lo = _split(res[:, P_GLR:P_GLR + LANES], 2)
    for d, col in ((0, P_GLF), (1, P_GLR)):
        w_hi, w_lo = _split(gkw_ref[d], 2)
        gate = (_dot(lr_hi, w_hi) + _dot(lr_lo, w_hi) + _dot(lr_hi, w_lo)
                + cp_ref[CP_GKB + d:CP_GKB + d + 1, :GLA_KEY_DIM])
        rest_ref[:, col:col + GLA_KEY_DIM] = _log_sigmoid(gate) * (1.0 / GLA_GATE_NORM)


def _mod_spec(l, part, row_of):
    return pl.BlockSpec((None, None, None, 1, D_MODEL), lambda *ids: (l, row_of(*ids), part, 0, 0))


def _in_proj(x, mod5, cpack, w_in_p, gk_w, l, mod_row, seq_len):
    n = x.shape[0]
    per = IN_TILE // HALO
    last_halo = n // HALO - 1
    assert seq_len & (seq_len - 1) == 0
    return pl.pallas_call(
        functools.partial(_in_proj_kernel, seq_len=seq_len),
        grid=(n // IN_TILE,),
        in_specs=[
            pl.BlockSpec((IN_TILE, D_MODEL), lambda i: (i, 0)),
            pl.BlockSpec((HALO, D_MODEL), lambda i: (jnp.maximum(i * per - 1, 0), 0)),
            pl.BlockSpec((HALO, D_MODEL), lambda i: (jnp.minimum((i + 1) * per, last_halo), 0)),
            _mod_spec(l, 0, mod_row),
            _mod_spec(l, 1, mod_row),
            pl.BlockSpec((None, CP_ROWS, D_MODEL), lambda i: (l, 0, 0)),
            pl.BlockSpec((None, D_MODEL, D_PROJ), lambda i: (l, 0, 0)),
            pl.BlockSpec((None, 2, LANES, GLA_KEY_DIM), lambda i: (l, 0, 0, 0)),
        ],
        out_specs=[pl.BlockSpec((IN_TILE, SSD_WIDTH), lambda i: (i, 0)),
                   pl.BlockSpec((IN_TILE, 2 * SSD_BC), lambda i: (i, 0)),
                   pl.BlockSpec((IN_TILE, D_REST), lambda i: (i, 0))],
        out_shape=[jax.ShapeDtypeStruct((n, SSD_WIDTH), F32), jax.ShapeDtypeStruct((n, 2 * SSD_BC), BF16),
                   jax.ShapeDtypeStruct((n, D_REST), F32)],
        compiler_params=pltpu.CompilerParams(dimension_semantics=("arbitrary",),
                                             vmem_limit_bytes=VMEM_LIMIT_BYTES),
        name="in_proj",
    )(x, x, x, mod5, mod5, cpack, w_in_p, gk_w)


def _conv_silu(main, prev, nxt, conv_w, conv_b):
    t = main.shape[0]
    up = jnp.concatenate([prev, main, nxt], axis=0)
    rows = t + 2 * HALO
    pad = (SSD_CONV - 1) // 2
    acc = conv_b
    for k in range(SSD_CONV):
        off = k - pad
        shifted = up if off == 0 else pltpu.roll(up, (rows - off) % rows, 0)
        acc = acc + conv_w[k:k + 1, :] * shifted[HALO:HALO + t, :]
    return _silu(acc)


def _ssd_direction(xs, bc, dt_small, a_small_row, ht_ref, rev, direction):
    t = xs.shape[0]
    q = SSD_CHUNK
    hpg = SSD_HEADS // SSD_GROUPS
    gw = hpg * SSD_HEAD_DIM
    er = _iota2((LANES, SSD_WIDTH), 0)
    ec = _iota2((LANES, SSD_WIDTH), 1)
    expand = (er == (ec >> 6) + SSD_HEADS * direction).astype(BF16)
    a_small = dt_small * a_small_row
    ri = _iota2((q, q), 0)
    ci = _iota2((q, q), 1)
    tri_incl = (ri >= ci).astype(BF16)
    keep = (ri <= ci) if rev else (ri >= ci)
    lane_lo = _iota2((q, LANES), 1) < SSD_HEAD_DIM
    ys = [None] * (t // q)
    order = range(t // q - 1, -1, -1) if rev else range(t // q)
    for c in order:
        sl = slice(c * q, (c + 1) * q)
        a_s = a_small[sl]
        dts = dt_small[sl]
        cs_s = _dot_split_rhs(tri_incl, a_s, 2)
        tot_s = cs_s[q - 1:q, :]
        pos_s = (cs_s - a_s) if rev else cs_s
        if rev:
            fac_s = jnp.exp(tot_s - pos_s)
            wgt_s = jnp.exp(pos_s)
        else:
            fac_s = jnp.exp(pos_s)
            wgt_s = jnp.exp(tot_s - pos_s)
        dtw_exp = _dot((dts * wgt_s).astype(BF16), expand)
        dec_state = _dot_split_lhs(jnp.broadcast_to(jnp.exp(tot_s), (SUBLANES, LANES)), expand, 3)[0:1, :]
        xc = xs[sl]
        xb = xc.astype(BF16)
        xw = (xc * dtw_exp).astype(BF16)
        pos_t = pos_s.T
        dt_t = dts.T
        y_groups = []
        for g in range(SSD_GROUPS):
            bg = bc[sl, g * SSD_STATE:(g + 1) * SSD_STATE]
            cg = bc[sl, SSD_BC + g * SSD_STATE:SSD_BC + (g + 1) * SSD_STATE]
            cg32 = cg.astype(F32)
            gmat = _dot_nt(cg, bg)
            ht = ht_ref[g]
            htb = ht.astype(BF16)
            glanes = slice(g * gw, (g + 1) * gw)
            pairs = []
            for j in range(hpg // 2):
                plo = g * gw + j * LANES
                rhs = jnp.concatenate([xb[:, plo:plo + LANES], htb[:, j * LANES:(j + 1) * LANES]], axis=0)
                lhs = []
                for hh in range(2):
                    lane = SSD_HEADS * direction + g * hpg + 2 * j + hh
                    col = pos_s[:, lane:lane + 1]
                    row = pos_t[lane:lane + 1, :]
                    seg = (row - col) if rev else (col - row)
                    decay = jnp.where(keep, jnp.exp(seg), 0.0) * dt_t[lane:lane + 1, :]
                    lhs.append(jnp.concatenate([(gmat * decay).astype(BF16),
                                                (cg32 * fac_s[:, lane:lane + 1]).astype(BF16)], axis=1))
                res = _dot(jnp.concatenate(lhs, axis=0), rhs)
                pairs.append(jnp.where(lane_lo, res[:q], res[q:]))
            y_groups.append(jnp.concatenate(pairs, axis=1))
            ht_ref[g] = ht * dec_state[:, glanes] + _dot_tn(bg, xw[:, glanes])
        ys[c] = jnp.concatenate(y_groups, axis=1)
    return jnp.concatenate(ys, axis=0)


def _gla_direction(qh, kh, vh, glog, s_ref, rev):
    t = qh.shape[0]
    q = GLA_CHUNK
    pair = 2 * q
    ri = _iota2((t, t), 0)
    ci = _iota2((t, t), 1)
    order_ok = (ri <= ci) if rev else (ri >= ci)
    prefix = (((ri >> 6) == (ci >> 6)) & order_ok).astype(BF16)
    gc = _dot_split_rhs(prefix, glog, 3)
    q_in = qh * jnp.exp(gc)
    k_in = kh * jnp.exp(-gc)
    pr = _iota2((pair, GLA_HEADS * pair), 0)
    pc = _iota2((pair, GLA_HEADS * pair), 1) & (pair - 1)
    causal = ((pr >> 6) == (pc >> 6)) & ((pc >= pr) if rev else (pc <= pr))
    khead = _iota2((GLA_KEY_DIM, pair), 0) >> 5
    vhead_rows = _iota2((GLA_HEADS * pair, GLA_WIDTH), 0) >> 7
    vhead_cols = _iota2((GLA_HEADS * pair, GLA_WIDTH), 1) >> 6
    blockdiag = (_iota2((GLA_KEY_DIM, GLA_WIDTH), 0) >> 5) == (_iota2((GLA_KEY_DIM, GLA_WIDTH), 1) >> 6)
    outs = [None] * (t // q)
    pair_order = range(t // pair - 1, -1, -1) if rev else range(t // pair)
    for pi in pair_order:
        psl = slice(pi * pair, (pi + 1) * pair)
        k_t = k_in[psl].T
        gc_t = gc[psl].T
        kbd = jnp.concatenate([jnp.where(khead == h, k_t, 0.0) for h in range(GLA_HEADS)], axis=1).astype(BF16)
        att = _dot(q_in[psl].astype(BF16), kbd)
        att = jnp.where(causal, att, 0.0).astype(BF16)
        vp = vh[psl].astype(BF16)
        vbd = jnp.where(vhead_rows == vhead_cols, jnp.concatenate([vp] * GLA_HEADS, axis=0), 0.0)
        o_intra = _dot(att, vbd)
        chunk_order = (1, 0) if rev else (0, 1)
        for cc in chunk_order:
            c = 2 * pi + cc
            sl = slice(c * q, (c + 1) * q)
            edge = cc * q if rev else cc * q + q - 1
            egl_col = jnp.exp(gc_t[:, edge:edge + 1])
            st = s_ref[...]
            o_inter = _dot(q_in[sl].astype(BF16), st.astype(BF16))
            outs[c] = o_intra[cc * q:(cc + 1) * q] + o_inter
            kw_t = (k_t[:, cc * q:(cc + 1) * q] * egl_col).astype(BF16)
            upd = _dot(kw_t, vh[sl].astype(BF16))
            s_ref[...] = st * egl_col + jnp.where(blockdiag, upd, 0.0)
    return jnp.concatenate(outs, axis=0)


def _head_rms(x, w_row):
    n = x.shape[1]
    ones = ((_iota2((n, n), 0) >> 6) == (_iota2((n, n), 1) >> 6)).astype(BF16)
    ms = _dot_split_lhs(x * x, ones, 2) * (1.0 / HEAD_DIM)
    return x * lax.rsqrt(ms + EPS) * w_row


def _rope(x, cos, sin_signed):
    n = x.shape[1]
    quarter = HEAD_DIM // 4
    first = (_iota2((1, n), 1) & (2 * quarter - 1)) < quarter
    swapped = jnp.where(first, pltpu.roll(x, n - quarter, 1), pltpu.roll(x, quarter, 1))
    return x * cos + swapped * sin_signed


def _attend_block(q_blk, kcat, vt, sink2, edge_masks):
    assert ATT_GROUP == 2 and ATT_KV_HEADS == 2
    blk = q_blk.shape[0]
    lane = _iota2((blk, LANES), 1)
    pieces = []
    for kvh in range(ATT_KV_HEADS):
        qt = q_blk[:, kvh * LANES:(kvh + 1) * LANES]
        rolled = pltpu.roll(qt, HEAD_DIM, 1)
        own = (lane < HEAD_DIM) if kvh == 0 else (lane >= HEAD_DIM)
        for g in range(ATT_GROUP):
            pieces.append(jnp.where(own, qt if g == kvh else rolled, 0.0))
    q_all = jnp.concatenate(pieces, axis=0).astype(BF16)
    st = _dot_nt(kcat, q_all)
    if edge_masks is not None:
        row, keep_before, keep_after = edge_masks
        st = jnp.concatenate([st[:row],
                              jnp.where(keep_before, st[row:row + blk], NEG_INF),
                              st[row + blk:row + 2 * blk],
                              jnp.where(keep_after, st[row + 2 * blk:], NEG_INF)], axis=0)
    m = jnp.maximum(jnp.max(st, axis=0, keepdims=True), sink2)
    p = jnp.exp2(st - m)
    den = jnp.sum(p, axis=0, keepdims=True) + jnp.exp2(sink2 - m)
    pb = p.astype(BF16)
    z = []
    for kvh in range(ATT_KV_HEADS):
        qcols = slice(kvh * ATT_GROUP * blk, (kvh + 1) * ATT_GROUP * blk)
        ot = _dot(vt[kvh * HEAD_DIM:(kvh + 1) * HEAD_DIM], pb[:, qcols]) / den[:, qcols]
        for g in range(ATT_GROUP):
            z.append(ot[:, g * blk:(g + 1) * blk])
    return jnp.concatenate(z, axis=0).T


def _gla_inputs(rest_ref, glog_col):
    qh = rest_ref[:, P_GQ:P_GQ + GLA_KEY_DIM] * (GLA_DK ** -0.5)
    kh = rest_ref[:, P_GK:P_GK + GLA_KEY_DIM]
    vh = rest_ref[:, P_GV:P_GV + GLA_WIDTH]
    return qh, kh, vh, rest_ref[:, glog_col:glog_col + GLA_KEY_DIM]


def _load_states(ht_ref, s_ref, ht0_ref, s0_ref):
    hpg = SSD_HEADS // SSD_GROUPS
    for g in range(SSD_GROUPS):
        blk = ht0_ref[g * hpg:(g + 1) * hpg].reshape(hpg * SSD_HEAD_DIM, SSD_STATE)
        ht_ref[g] = blk.T
    rows = []
    for h in range(GLA_HEADS):
        pieces = []
        if h > 0:
            pieces.append(jnp.zeros((GLA_DK, h * GLA_DV), F32))
        pieces.append(s0_ref[h])
        if h < GLA_HEADS - 1:
            pieces.append(jnp.zeros((GLA_DK, (GLA_HEADS - 1 - h) * GLA_DV), F32))
        rows.append(jnp.concatenate(pieces, axis=1))
    s_ref[...] = jnp.concatenate(rows, axis=0)


def _store_states(ht_ref, s_ref, htf_ref, sf_ref):
    hpg = SSD_HEADS // SSD_GROUPS
    for g in range(SSD_GROUPS):
        htf_ref[g * hpg:(g + 1) * hpg] = ht_ref[g].T.reshape(hpg, SSD_HEAD_DIM, SSD_STATE)
    st = s_ref[...]
    for h in range(GLA_HEADS):
        sf_ref[h] = st[h * GLA_DK:(h + 1) * GLA_DK, h * GLA_DV:(h + 1) * GLA_DV]


HT_SHAPE = (SSD_GROUPS, SSD_STATE, SSD_WIDTH // SSD_GROUPS)
S_SHAPE = (GLA_KEY_DIM, GLA_WIDTH)
SSD_STATE_SHAPE = (SSD_HEADS, SSD_HEAD_DIM, SSD_STATE)
GLA_STATE_SHAPE = (GLA_HEADS, GLA_DK, GLA_DV)


def _mix_bwd_kernel(*refs, is_ctx, n_tiles):
    it = iter(refs)
    xs_ref = next(it)
    bc_ref = next(it)
    rest_ref = next(it)
    cp_ref = next(it)
    if is_ctx:
        next(it)
        next(it)
    else:
        ht0_ref = next(it)
        s0_ref = next(it)
    yb_ref = next(it)
    ob_ref = next(it)
    if is_ctx:
        htf_ref = next(it)
        sf_ref = next(it)
    ht_ref = next(it)
    s_ref = next(it)

    @pl.when(pl.program_id(1) == 0)
    def _():
        if is_ctx:
            ht_ref[...] = jnp.zeros(ht_ref.shape, F32)
            s_ref[...] = jnp.zeros(s_ref.shape, F32)
        else:
            _load_states(ht_ref, s_ref, ht0_ref, s0_ref)

    a_small_row = -jnp.exp(cp_ref[CP_ALOG:CP_ALOG + 1, :LANES])
    yb_ref[...] = _ssd_direction(xs_ref[...], bc_ref[...], rest_ref[:, P_DT:P_DT + LANES], a_small_row, ht_ref,
                                 True, 1)
    qh, kh, vh, glog = _gla_inputs(rest_ref, P_GLR)
    ob_ref[...] = _gla_direction(qh, kh, vh, glog, s_ref, True)
    if is_ctx:
        _store_states(ht_ref, s_ref, htf_ref, sf_ref)


def _state_specs(l, direction):
    return [pl.BlockSpec((None, None, None) + SSD_STATE_SHAPE, lambda s, t: (s, l, direction, 0, 0, 0)),
            pl.BlockSpec((None, None, None) + GLA_STATE_SHAPE, lambda s, t: (s, l, direction, 0, 0, 0))]


def _mix_bwd(xs, bc, rest, cpack, l, states, *, n_seq, n_tiles, is_ctx):
    n = xs.shape[0]
    row = lambda s, t: (s * n_tiles + (n_tiles - 1 - t), 0)
    in_specs = [
        pl.BlockSpec((TILE, SSD_WIDTH), row),
        pl.BlockSpec((TILE, 2 * SSD_BC), row),
        pl.BlockSpec((TILE, BWD_COLS), row),
        pl.BlockSpec((None, CP_ROWS, D_MODEL), lambda s, t: (l, 0, 0)),
    ]
    args = [xs, bc, rest, cpack]
    out_specs = [pl.BlockSpec((TILE, SSD_WIDTH), row), pl.BlockSpec((TILE, GLA_WIDTH), row)]
    out_shape = [jax.ShapeDtypeStruct((n, SSD_WIDTH), F32), jax.ShapeDtypeStruct((n, GLA_WIDTH), F32)]
    aliases = {}
    if is_ctx:
        in_specs += [pl.BlockSpec(memory_space=pl.ANY), pl.BlockSpec(memory_space=pl.ANY)]
        out_specs += _state_specs(l, 1)
        out_shape += [jax.ShapeDtypeStruct(a.shape, a.dtype) for a in states]
        aliases = {len(args): len(out_shape) - 2, len(args) + 1: len(out_shape) - 1}
    else:
        in_specs += _state_specs(l, 1)
    args += list(states)
    return pl.pallas_call(
        functools.partial(_mix_bwd_kernel, is_ctx=is_ctx, n_tiles=n_tiles),
        grid=(n_seq, n_tiles),
        in_specs=in_specs,
        out_specs=out_specs,
        out_shape=out_shape,
        input_output_aliases=aliases,
        scratch_shapes=[pltpu.VMEM(HT_SHAPE, F32), pltpu.VMEM(S_SHAPE, F32)],
        compiler_params=pltpu.CompilerParams(dimension_semantics=("arbitrary", "arbitrary"),
                                             vmem_limit_bytes=VMEM_LIMIT_BYTES),
        name="mix_bwd_ctx" if is_ctx else "mix_bwd_lat",
    )(*args)


def _mix_fwd_kernel(*refs, is_ctx, n_tiles, layer):
    it = iter(refs)
    sink_ref = next(it)
    x_ref = next(it)
    gate_ref = next(it)
    rest_ref = next(it)
    xs_ref = next(it)
    bc_ref = next(it)
    yb_ref = next(it)
    ob_ref = next(it)
    cp_ref = next(it)
    wout_ref = next(it)
    if is_ctx:
        for _ in range(4):
            next(it)
    else:
        ht0_ref = next(it)
        s0_ref = next(it)
        kvprev_ref = next(it)
        kvnext_ref = next(it)
        cs_ref = next(it)
        csp_ref = next(it)
        csn_ref = next(it)
        kc_ref = next(it)
        vc_ref = next(it)
    o_ref = next(it)
    if is_ctx:
        kout_ref = next(it)
        vout_ref = next(it)
        htf_ref = next(it)
        sf_ref = next(it)
    ht_ref = next(it)
    s_ref = next(it)

    tile_idx = pl.program_id(1)

    @pl.when(tile_idx == 0)
    def _():
        if is_ctx:
            ht_ref[...] = jnp.zeros(ht_ref.shape, F32)
            s_ref[...] = jnp.zeros(s_ref.shape, F32)
        else:
            _load_states(ht_ref, s_ref, ht0_ref, s0_ref)

    t = TILE
    xs = xs_ref[...]
    a_small_row = -jnp.exp(cp_ref[CP_ALOG:CP_ALOG + 1, :LANES])
    knw = cp_ref[CP_KNW:CP_KNW + 1, :ATT_KV_DIM]

    y_f = _ssd_direction(xs, bc_ref[...], rest_ref[:, P_DT:P_DT + LANES], a_small_row, ht_ref, False, 0)
    y = y_f + yb_ref[...] + xs * cp_ref[CP_DSKIP:CP_DSKIP + 1, :SSD_WIDTH]
    y = y * _silu(rest_ref[:, P_Z:P_Z + SSD_WIDTH])
    y_ssd = y * lax.rsqrt(jnp.mean(y * y, axis=-1, keepdims=True) + EPS) * cp_ref[CP_SSDNW:CP_SSDNW + 1, :SSD_WIDTH]

    qh, kh, vh, glog = _gla_inputs(rest_ref, P_GLF)
    o = _gla_direction(qh, kh, vh, glog, s_ref, False) + ob_ref[...]
    y_gla = (_head_rms(o, cp_ref[CP_GLANW:CP_GLANW + 1, :GLA_WIDTH])
             * _silu(rest_ref[:, P_GG:P_GG + GLA_WIDTH]))

    qn = _head_rms(rest_ref[:, P_AQ:P_AQ + ATT_WIDTH], cp_ref[CP_QNW:CP_QNW + 1, :ATT_WIDTH])
    kn = _head_rms(rest_ref[:, P_AK:P_AK + ATT_KV_DIM], knw)
    vv = rest_ref[:, P_AV:P_AV + ATT_KV_DIM]
    qscale = (HEAD_DIM ** -0.5) * LOG2E
    blk = ATT_BLOCK
    n_blk = t // blk
    sink2 = jnp.concatenate([jnp.full((1, blk), sink_ref[layer, h] * LOG2E, F32) for h in range(ATT_HEADS)], axis=1)
    if is_ctx:
        kout_ref[...] = kn
        vout_ref[...] = vv
        qs = qn * qscale
        kb = kn.astype(BF16)
        v_t = vv.T.astype(BF16)
        y_att = jnp.concatenate(
            [_attend_block(qs[nb * blk:(nb + 1) * blk], kb, v_t, sink2, None) for nb in range(n_blk)], axis=0)
    else:
        cos = cs_ref[:, :ATT_KV_DIM]
        sin = cs_ref[:, ATT_KV_DIM:]
        qr = _rope(qn, jnp.concatenate([cos, cos], axis=1), jnp.concatenate([sin, sin], axis=1)) * qscale
        k_halo_p = _rope(_head_rms(kvprev_ref[:, :ATT_KV_DIM], knw), csp_ref[:, :ATT_KV_DIM], csp_ref[:, ATT_KV_DIM:])
        k_halo_n = _rope(_head_rms(kvnext_ref[:, :ATT_KV_DIM], knw), csn_ref[:, :ATT_KV_DIM], csn_ref[:, ATT_KV_DIM:])
        k_loc = jnp.concatenate([k_halo_p, _rope(kn, cos, sin), k_halo_n], axis=0).astype(BF16)
        kc = kc_ref[...].astype(BF16)
        n_ctx = kc.shape[0]
        v_all_t = jnp.concatenate([vc_ref[...], kvprev_ref[:, ATT_KV_DIM:], vv, kvnext_ref[:, ATT_KV_DIM:]],
                                  axis=0).T.astype(BF16)
        kj = _iota2((blk, ATT_HEADS * blk), 0)
        qi = _iota2((blk, ATT_HEADS * blk), 1) & (blk - 1)
        blocks = []
        for nb in range(n_blk):
            kcat = jnp.concatenate([kc, k_loc[nb * blk:(nb + 3) * blk]], axis=0)
            vt = jnp.concatenate([v_all_t[:, :n_ctx], v_all_t[:, n_ctx + nb * blk:n_ctx + (nb + 3) * blk]], axis=1)
            off_p = jnp.where(tile_idx > 0, 0, 2 * blk) if nb == 0 else 0
            off_n = jnp.where(tile_idx < n_tiles - 1, 0, 2 * blk) if nb == n_blk - 1 else 0
            masks = (n_ctx, kj >= qi + off_p, kj <= qi - off_n)
            blocks.append(_attend_block(qr[nb * blk:(nb + 1) * blk], kcat, vt, sink2, masks))
        y_att = jnp.concatenate(blocks, axis=0)
    y_att = y_att * _silu(rest_ref[:, P_AG:P_AG + ATT_WIDTH])

    ycat = jnp.concatenate([y_ssd, y_gla, y_att], axis=1).astype(BF16)
    o_ref[...] = x_ref[...] + gate_ref[...] * _dot(ycat, wout_ref[...])
    if is_ctx:
        _store_states(ht_ref, s_ref, htf_ref, sf_ref)


def _mix_fwd(x, mod5, gate_row, xs, bc, rest, yb, ob, cpack, sink, w_out, l, extra, *, n_seq, n_tiles, is_ctx):
    n = x.shape[0]
    tile_of = lambda s, t: s * n_tiles + t
    row = lambda s, t: (tile_of(s, t), 0)
    in_specs = [
        pl.BlockSpec(memory_space=pltpu.SMEM),
        pl.BlockSpec((TILE, D_MODEL), row),
        _mod_spec(l, 2, lambda s, t: gate_row(s)),
        pl.BlockSpec((TILE, D_REST), row),
        pl.BlockSpec((TILE, SSD_WIDTH), row),
        pl.BlockSpec((TILE, 2 * SSD_BC), row),
        pl.BlockSpec((TILE, SSD_WIDTH), row),
        pl.BlockSpec((TILE, GLA_WIDTH), row),
        pl.BlockSpec((None, CP_ROWS, D_MODEL), lambda s, t: (l, 0, 0)),
        pl.BlockSpec((None, D_MIX, D_MODEL), lambda s, t: (l, 0, 0)),
    ]
    args = [sink, x, mod5, rest, xs, bc, yb, ob, cpack, w_out]
    out_specs = [pl.BlockSpec((TILE, D_MODEL), row)]
    out_shape = [jax.ShapeDtypeStruct((n, D_MODEL), F32)]
    aliases = {}
    if is_ctx:
        kv_spec = pl.BlockSpec((None, None, TILE, ATT_KV_DIM), lambda s, t: (s, l, t, 0))
        in_specs += [pl.BlockSpec(memory_space=pl.ANY)] * 4
        out_specs += [kv_spec, kv_spec] + _state_specs(l, 0)
        out_shape += [jax.ShapeDtypeStruct(a.shape, a.dtype) for a in extra]
        aliases = {len(args) + i: 1 + i for i in range(4)}
        args += list(extra)
    else:
        state_ssd, state_gla, rope_cs, cache_k, cache_v = extra
        per_blk = TILE // ATT_BLOCK
        kv_col = P_AK // (2 * ATT_KV_DIM)
        last_blk = n // ATT_BLOCK - 1
        kvprev = lambda s, t: (jnp.maximum(tile_of(s, t) * per_blk - 1, 0), kv_col)
        kvnext = lambda s, t: (jnp.minimum((tile_of(s, t) + 1) * per_blk, last_blk), kv_col)
        last_pos_blk = n_tiles * per_blk - 1
        n_ctx = cache_k.shape[2]
        ctx_spec = pl.BlockSpec((None, None, n_ctx, ATT_KV_DIM), lambda s, t: (s, l, 0, 0))
        in_specs += _state_specs(l, 0) + [
            pl.BlockSpec((ATT_BLOCK, 2 * ATT_KV_DIM), kvprev),
            pl.BlockSpec((ATT_BLOCK, 2 * ATT_KV_DIM), kvnext),
            pl.BlockSpec((TILE, 2 * ATT_KV_DIM), lambda s, t: (t, 0)),
            pl.BlockSpec((ATT_BLOCK, 2 * ATT_KV_DIM), lambda s, t: (jnp.maximum(t * per_blk - 1, 0), 0)),
            pl.BlockSpec((ATT_BLOCK, 2 * ATT_KV_DIM), lambda s, t: (jnp.minimum((t + 1) * per_blk, last_pos_blk), 0)),
            ctx_spec,
            ctx_spec,
        ]
        args += [state_ssd, state_gla, rest, rest, rope_cs, rope_cs, rope_cs, cache_k, cache_v]
    return pl.pallas_call(
        functools.partial(_mix_fwd_kernel, is_ctx=is_ctx, n_tiles=n_tiles, layer=l),
        grid=(n_seq, n_tiles),
        in_specs=in_specs,
        out_specs=out_specs,
        out_shape=out_shape,
        input_output_aliases=aliases,
        scratch_shapes=[pltpu.VMEM(HT_SHAPE, F32), pltpu.VMEM(S_SHAPE, F32)],
        compiler_params=pltpu.CompilerParams(dimension_semantics=("arbitrary", "arbitrary"),
                                             vmem_limit_bytes=VMEM_LIMIT_BYTES),
        name="mix_fwd_ctx" if is_ctx else "mix_fwd_lat",
    )(*args)


def _pad_cols(a, width):
    return jnp.pad(a, [(0, 0)] * (a.ndim - 1) + [(0, width - a.shape[-1])])


def _prep_w_in(w_in):
    sizes = (SSD_WIDTH, SSD_CONV_DIM, 2 * SSD_HEADS, GLA_KEY_DIM, GLA_KEY_DIM, GLA_WIDTH, GLA_WIDTH, 2 * GLA_LOWRANK,
             ATT_WIDTH, ATT_KV_DIM, ATT_KV_DIM, ATT_WIDTH)
    parts, start = [], 0
    for size in sizes:
        parts.append(w_in[..., start:start + size])
        start += size
    z, xbc, dt, gq, gk, gv, gg, glr, aq, ak, av, ag = parts
    cols = [xbc, _pad_cols(dt, LANES), gq, gk, _pad_cols(glr, LANES), gv, ak, av, z, gg, aq, ag]
    return jnp.concatenate(cols, axis=-1).astype(BF16)


def _pack_small_params(norm_w, conv_w, conv_b, ssd_a_log, ssd_dt_bias, ssd_d, ssd_norm_w, gla_gk_b, gla_norm_w,
                       attn_q_norm, attn_k_norm):
    depth = norm_w.shape[0]
    row = lambda a: _pad_cols(a.reshape(depth, 1, -1), D_MODEL)
    blank = lambda n: jnp.zeros((depth, n, D_MODEL), F32)
    rows = [
        _pad_cols(conv_w, D_MODEL), blank(CP_CONVB - SSD_CONV),
        row(conv_b), row(ssd_dt_bias), row(ssd_a_log), row(gla_gk_b[:, 0]), row(gla_gk_b[:, 1]),
        row(jnp.repeat(ssd_d, SSD_HEAD_DIM, axis=-1)), row(ssd_norm_w), row(jnp.tile(gla_norm_w, (1, GLA_HEADS))),
        row(jnp.tile(attn_q_norm, (1, ATT_HEADS))), row(jnp.tile(attn_k_norm, (1, ATT_KV_HEADS))), row(norm_w),
        blank(CP_ROWS - CP_NW - 1),
    ]
    return jnp.concatenate(rows, axis=1)


def _pad_gate_weights(gla_gk_up):
    depth = gla_gk_up.shape[0]
    w = jnp.zeros((depth, 2, LANES, GLA_KEY_DIM), F32)
    for d in range(2):
        w = w.at[:, d, d * GLA_LOWRANK:(d + 1) * GLA_LOWRANK].set(gla_gk_up[:, d])
    return w


def _rope_tables(seq_len):
    quarter = HEAD_DIM // 4
    rows = seq_len // GRID_W
    row_pos = jnp.repeat(jnp.arange(rows, dtype=F32), GRID_W)
    col_pos = jnp.tile(jnp.arange(GRID_W, dtype=F32), rows)
    inv = ROPE_THETA ** (-jnp.arange(quarter, dtype=F32) / quarter)
    ang_r = row_pos[:, None] * inv[None, :]
    ang_c = col_pos[:, None] * inv[None, :]
    cos = jnp.concatenate([jnp.cos(ang_r), jnp.cos(ang_r), jnp.cos(ang_c), jnp.cos(ang_c)], axis=1)
    sin = jnp.concatenate([-jnp.sin(ang_r), jnp.sin(ang_r), -jnp.sin(ang_c), jnp.sin(ang_c)], axis=1)
    return jnp.tile(cos, (1, ATT_KV_HEADS)), jnp.tile(sin, (1, ATT_KV_HEADS))


def kernel(x_prompt, x_sample, c, cache_k, cache_v, state_ssd, state_gla, c_ctx, w_ada, b_ada, norm_w, w_in, conv_w,
           conv_b, ssd_a_log, ssd_dt_bias, ssd_d, ssd_norm_w, gla_gk_up, gla_gk_b, gla_norm_w, attn_q_norm, attn_k_norm,
           attn_sink, w_out):
    batch, seq, _ = x_prompt.shape
    dec_batch, dec_seq, _ = x_sample.shape
    depth = w_in.shape[0]
    past = cache_k.shape[2]
    assert seq % TILE == 0 and dec_seq % TILE == 0
    assert (batch * seq) % IN_TILE == 0 and dec_seq % IN_TILE == 0

    rows = -(-(dec_batch + 1) // SUBLANES) * SUBLANES
    cond = jnp.zeros((rows, D_MODEL), F32).at[:dec_batch].set(c).at[dec_batch].set(c_ctx)
    mod = _modulation(cond, w_ada, b_ada)
    mod5 = mod.reshape(depth, rows, 3, 1, D_MODEL)

    w_in_p = _prep_w_in(w_in)
    w_out_b = w_out.astype(BF16)
    cpack = _pack_small_params(norm_w, conv_w, conv_b, ssd_a_log, ssd_dt_bias, ssd_d, ssd_norm_w, gla_gk_b,
                               gla_norm_w, attn_q_norm, attn_k_norm)
    gk_w = _pad_gate_weights(gla_gk_up)
    rope_cs = jnp.concatenate(_rope_tables(dec_seq), axis=1)
    cache_k2 = cache_k.reshape(dec_batch, depth, past, ATT_KV_DIM)
    cache_v2 = cache_v.reshape(dec_batch, depth, past, ATT_KV_DIM)

    x_ctx = x_prompt.reshape(batch * seq, D_MODEL)
    x_lat = x_sample.reshape(dec_batch * dec_seq, D_MODEL)
    ctx_tiles = seq // TILE
    lat_tiles = dec_seq // TILE
    lat_in_per_seq = dec_seq // IN_TILE

    new_k = jnp.zeros((batch, depth, seq, ATT_KV_DIM), F32)
    new_v = jnp.zeros((batch, depth, seq, ATT_KV_DIM), F32)
    new_ssd = jnp.zeros((batch, depth, 2) + SSD_STATE_SHAPE, F32)
    new_gla = jnp.zeros((batch, depth, 2) + GLA_STATE_SHAPE, F32)

    for l in range(depth):
        xs, bc, rest = _in_proj(x_ctx, mod5, cpack, w_in_p, gk_w, l, lambda i: dec_batch, seq)
        yb, ob, new_ssd, new_gla = _mix_bwd(xs, bc, rest, cpack, l, (new_ssd, new_gla), n_seq=batch,
                                            n_tiles=ctx_tiles, is_ctx=True)
        x_ctx, new_k, new_v, new_ssd, new_gla = _mix_fwd(
            x_ctx, mod5, lambda s: dec_batch, xs, bc, rest, yb, ob, cpack, attn_sink, w_out_b, l,
            (new_k, new_v, new_ssd, new_gla), n_seq=batch, n_tiles=ctx_tiles, is_ctx=True)

        xs, bc, rest = _in_proj(x_lat, mod5, cpack, w_in_p, gk_w, l, lambda i: i // lat_in_per_seq, dec_seq)
        yb, ob = _mix_bwd(xs, bc, rest, cpack, l, (state_ssd, state_gla), n_seq=dec_batch, n_tiles=lat_tiles,
                          is_ctx=False)
        (x_lat,) = _mix_fwd(x_lat, mod5, lambda s: s, xs, bc, rest, yb, ob, cpack, attn_sink, w_out_b, l,
                            (state_ssd, state_gla, rope_cs, cache_k2, cache_v2), n_seq=dec_batch, n_tiles=lat_tiles,
                            is_ctx=False)

    return (x_ctx.reshape(batch, seq, D_MODEL), x_lat.reshape(dec_batch, dec_seq, D_MODEL),
            new_k.reshape(batch, depth, seq, ATT_KV_HEADS, HEAD_DIM),
            new_v.reshape(batch, depth, seq, ATT_KV_HEADS, HEAD_DIM), new_ssd, new_gla)
```

```python
import functools
import math

import jax
import jax.numpy as jnp
from jax import lax
from jax.experimental import pallas as pl
from jax.experimental.pallas import tpu as pltpu

D_MODEL = 1024
GRID_W = 64
SSD_HEADS = 8
SSD_HEAD_DIM = 64
SSD_WIDTH = 512
SSD_GROUPS = 2
SSD_STATE = 128
SSD_BC = SSD_GROUPS * SSD_STATE
SSD_CONV = 5
SSD_CONV_DIM = SSD_WIDTH + 2 * SSD_BC
SSD_CHUNK = 128
GLA_HEADS = 4
GLA_DK = 32
GLA_DV = 64
GLA_KEY_DIM = GLA_HEADS * GLA_DK
GLA_WIDTH = GLA_HEADS * GLA_DV
GLA_LOWRANK = 16
GLA_GATE_NORM = 16.0
GLA_CHUNK = 64
ATT_HEADS = 4
ATT_KV_HEADS = 2
ATT_GROUP = ATT_HEADS // ATT_KV_HEADS
HEAD_DIM = 64
ATT_WIDTH = ATT_HEADS * HEAD_DIM
ATT_KV_DIM = ATT_KV_HEADS * HEAD_DIM
WINDOW = 128
ATT_BLOCK = 128
ROPE_THETA = 10000.0
D_MIX = SSD_WIDTH + GLA_WIDTH + ATT_WIDTH
EPS = 1e-6
NEG_INF = -1e30
LOG2E = math.log2(math.e)

LANES = 128
SUBLANES = 8
VMEM_LIMIT_BYTES = 56 * 1024 * 1024

P_DT = 0
P_GQ = P_DT + LANES
P_GK = P_GQ + GLA_KEY_DIM
P_GLR = P_GK + GLA_KEY_DIM
P_GV = P_GLR + LANES
P_AK = P_GV + GLA_WIDTH
P_AV = P_AK + ATT_KV_DIM
P_Z = P_AV + ATT_KV_DIM
P_GG = P_Z + SSD_WIDTH
P_AQ = P_GG + GLA_WIDTH
P_AG = P_AQ + ATT_WIDTH
D_LIN = P_AG + ATT_WIDTH
D_PROJ = SSD_CONV_DIM + D_LIN
P_GLF = D_LIN
D_REST = D_LIN + GLA_KEY_DIM
BWD_COLS = P_AK

CP_CONVW = 0
CP_CONVB = 8
CP_DTB = 9
CP_ALOG = 10
CP_GKB = 11
CP_DSKIP = 13
CP_SSDNW = 14
CP_GLANW = 15
CP_QNW = 16
CP_KNW = 17
CP_NW = 18
CP_ROWS = 24

TILE = 256
IN_TILE = 512
HALO = SUBLANES
CONV_COLS = 256
CONV_ROWS = 64
SEQ_PER_STEP = 2

HI = lax.Precision.HIGHEST
F32 = jnp.float32
BF16 = jnp.bfloat16

NT_DIMS = (((1,), (1,)), ((), ()))
TN_DIMS = (((0,), (0,)), ((), ()))


def _dot(a, b):
    return jnp.dot(a, b, preferred_element_type=F32)


def _dot_nt(a, b):
    return lax.dot_general(a, b, NT_DIMS, preferred_element_type=F32)


def _dot_tn(a, b):
    return lax.dot_general(a, b, TN_DIMS, preferred_element_type=F32)


def _split(x, parts):
    out = []
    for _ in range(parts - 1):
        piece = x.astype(BF16)
        out.append(piece)
        x = x - piece.astype(F32)
    out.append(x.astype(BF16))
    return out


def _dot_split_lhs(x, w_bf16, parts):
    acc = None
    for piece in _split(x, parts):
        term = _dot(piece, w_bf16)
        acc = term if acc is None else acc + term
    return acc


def _dot_split_rhs(w_bf16, x, parts):
    acc = None
    for piece in _split(x, parts):
        term = _dot(w_bf16, piece)
        acc = term if acc is None else acc + term
    return acc


def _sigmoid(x):
    return 1.0 / (1.0 + jnp.exp(-x))


def _silu(x):
    return x * _sigmoid(x)


def _softplus(x):
    return jnp.maximum(x, 0.0) + jnp.log(1.0 + jnp.exp(-jnp.abs(x)))


def _log_sigmoid(x):
    return jnp.minimum(x, 0.0) - jnp.log(1.0 + jnp.exp(-jnp.abs(x)))


def _iota2(shape, dim):
    return lax.broadcasted_iota(jnp.int32, shape, dim)


def _mod_kernel(cond_ref, w_ref, b_ref, o_ref):
    o_ref[...] = jnp.dot(_silu(cond_ref[...]), w_ref[...], preferred_element_type=F32, precision=HI) + b_ref[...]


def _modulation(cond, w_ada, b_ada):
    depth = w_ada.shape[0]
    rows = cond.shape[0]
    return pl.pallas_call(
        _mod_kernel,
        grid=(depth, 3),
        in_specs=[
            pl.BlockSpec((rows, D_MODEL), lambda l, j: (0, 0)),
            pl.BlockSpec((None, D_MODEL, D_MODEL), lambda l, j: (l, 0, j)),
            pl.BlockSpec((None, 1, D_MODEL), lambda l, j: (l, 0, j)),
        ],
        out_specs=pl.BlockSpec((None, rows, D_MODEL), lambda l, j: (l, 0, j)),
        out_shape=jax.ShapeDtypeStruct((depth, rows, 3 * D_MODEL), F32),
        compiler_params=pltpu.CompilerParams(dimension_semantics=("arbitrary", "arbitrary")),
        name="modulation",
    )(cond, w_ada, b_ada.reshape(depth, 1, 3 * D_MODEL))


def _in_proj_kernel(x_ref, xprev_ref, xnext_ref, shift_ref, scale_ref, cp_ref, w_ref, gkw_ref, xs_ref, bc_ref,
                    rest_ref, *, seq_len):
    nw = cp_ref[CP_NW:CP_NW + 1, :]
    scale1 = 1.0 + scale_ref[...]
    shift = shift_ref[...]

    def modnorm(x):
        xn = x * lax.rsqrt(jnp.mean(x * x, axis=-1, keepdims=True) + EPS) * nw
        return (xn * scale1 + shift).astype(BF16)

    h = modnorm(x_ref[...])
    t = h.shape[0]
    h_up = jnp.concatenate([modnorm(xprev_ref[...]), h, modnorm(xnext_ref[...])], axis=0)
    seg = min(seq_len, t)
    pos0 = (pl.program_id(0) * t) & (seq_len - 1)
    zero_halo = jnp.zeros((HALO, CONV_COLS), F32)
    for c0 in range(0, SSD_CONV_DIM, CONV_COLS):
        up = _dot(h_up, w_ref[:, c0:c0 + CONV_COLS])
        conv_w = cp_ref[CP_CONVW:CP_CONVW + SUBLANES, c0:c0 + CONV_COLS]
        conv_b = cp_ref[CP_CONVB:CP_CONVB + 1, c0:c0 + CONV_COLS]
        for a in range(0, t, CONV_ROWS):
            if a % seq_len == 0 and seq_len <= t:
                prev = zero_halo
            elif a % seg == 0:
                prev = jnp.where(pos0 + a > 0, up[a:a + HALO, :], 0.0)
            else:
                prev = up[a:a + HALO, :]
            b = a + CONV_ROWS
            if b % seq_len == 0 and seq_len <= t:
                nxt = zero_halo
            elif b % seg == 0:
                nxt = jnp.where(pos0 + b < seq_len, up[HALO + b:2 * HALO + b, :], 0.0)
            else:
                nxt = up[HALO + b:2 * HALO + b, :]
            xbc = _conv_silu(up[HALO + a:HALO + b, :], prev, nxt, conv_w, conv_b)
            if c0 < SSD_WIDTH:
                xs_ref[a:b, c0:c0 + CONV_COLS] = xbc
            else:
                bc_ref[a:b, c0 - SSD_WIDTH:c0 - SSD_WIDTH + CONV_COLS] = xbc.astype(BF16)

    res = _dot(h, w_ref[:, SSD_CONV_DIM:])
    rest_ref[:, P_DT:P_DT + LANES] = _softplus(res[:, P_DT:P_DT + LANES] + cp_ref[CP_DTB:CP_DTB + 1, :LANES])
    rest_ref[:, P_GQ:P_GLR] = res[:, P_GQ:P_GLR]
    rest_ref[:, P_GV:D_LIN] = res[:, P_GV:]
    lr_hi, lr_lo = _split(res[:, P_GLR:P_GLR + LANES], 2)
    for d, col in ((0, P_GLF), (1, P_GLR)):
        w_hi, w_lo = _split(gkw_ref[:, d * GLA_KEY_DIM:(d + 1) * GLA_KEY_DIM], 2)
        gate = (_dot(lr_hi, w_hi) + _dot(lr_lo, w_hi) + _dot(lr_hi, w_lo)
                + cp_ref[CP_GKB + d:CP_GKB + d + 1, :GLA_KEY_DIM])
        rest_ref[:, col:col + GLA_KEY_DIM] = _log_sigmoid(gate) * (1.0 / GLA_GATE_NORM)


def _mod_spec(l, part, row_of):
    return pl.BlockSpec((None, None, None, 1, D_MODEL), lambda *ids: (l, row_of(*ids), part, 0, 0))


def _in_proj(x, mod5, cpack, w_in_p, gk_w, l, mod_row, seq_len):
    n = x.shape[0]
    per = IN_TILE // HALO
    last_halo = n // HALO - 1
    assert seq_len & (seq_len - 1) == 0
    return pl.pallas_call(
        functools.partial(_in_proj_kernel, seq_len=seq_len),
        grid=(n // IN_TILE,),
        in_specs=[
            pl.BlockSpec((IN_TILE, D_MODEL), lambda i: (i, 0)),
            pl.BlockSpec((HALO, D_MODEL), lambda i: (jnp.maximum(i * per - 1, 0), 0)),
            pl.BlockSpec((HALO, D_MODEL), lambda i: (jnp.minimum((i + 1) * per, last_halo), 0)),
            _mod_spec(l, 0, mod_row),
            _mod_spec(l, 1, mod_row),
            pl.BlockSpec((None, CP_ROWS, D_MODEL), lambda i: (l, 0, 0)),
            pl.BlockSpec((None, D_MODEL, D_PROJ), lambda i: (l, 0, 0)),
            pl.BlockSpec((None, LANES, 2 * GLA_KEY_DIM), lambda i: (l, 0, 0)),
        ],
        out_specs=[pl.BlockSpec((IN_TILE, SSD_WIDTH), lambda i: (i, 0)),
                   pl.BlockSpec((IN_TILE, 2 * SSD_BC), lambda i: (i, 0)),
                   pl.BlockSpec((IN_TILE, D_REST), lambda i: (i, 0))],
        out_shape=[jax.ShapeDtypeStruct((n, SSD_WIDTH), F32), jax.ShapeDtypeStruct((n, 2 * SSD_BC), BF16),
                   jax.ShapeDtypeStruct((n, D_REST), F32)],
        compiler_params=pltpu.CompilerParams(dimension_semantics=("arbitrary",),
                                             vmem_limit_bytes=VMEM_LIMIT_BYTES),
        name="in_proj",
    )(x, x, x, mod5, mod5, cpack, w_in_p, gk_w)


def _conv_silu(main, prev, nxt, conv_w, conv_b):
    t = main.shape[0]
    up = jnp.concatenate([prev, main, nxt], axis=0)
    rows = t + 2 * HALO
    pad = (SSD_CONV - 1) // 2
    acc = conv_b
    for k in range(SSD_CONV):
        off = k - pad
        shifted = up if off == 0 else pltpu.roll(up, (rows - off) % rows, 0)
        acc = acc + conv_w[k:k + 1, :] * shifted[HALO:HALO + t, :]
    return _silu(acc)


def _ssd_direction(xs, bc, dt_small, a_small_row, ht_ref, rev, direction):
    t = xs.shape[0]
    q = SSD_CHUNK
    hpg = SSD_HEADS // SSD_GROUPS
    gw = hpg * SSD_HEAD_DIM
    er = _iota2((LANES, SSD_WIDTH), 0)
    ec = _iota2((LANES, SSD_WIDTH), 1)
    expand = (er == (ec >> 6) + SSD_HEADS * direction).astype(BF16)
    a_small = dt_small * a_small_row
    ri = _iota2((q, q), 0)
    ci = _iota2((q, q), 1)
    tri_incl = (ri >= ci).astype(BF16)
    keep = (ri <= ci) if rev else (ri >= ci)
    lane_lo = _iota2((q, LANES), 1) < SSD_HEAD_DIM
    ys = [None] * (t // q)
    order = range(t // q - 1, -1, -1) if rev else range(t // q)
    for c in order:
        sl = slice(c * q, (c + 1) * q)
        a_s = a_small[sl]
        dts = dt_small[sl]
        cs_s = _dot_split_rhs(tri_incl, a_s, 2)
        tot_s = cs_s[q - 1:q, :]
        pos_s = (cs_s - a_s) if rev else cs_s
        if rev:
            fac_s = jnp.exp(tot_s - pos_s)
            wgt_s = jnp.exp(pos_s)
        else:
            fac_s = jnp.exp(pos_s)
            wgt_s = jnp.exp(tot_s - pos_s)
        dtw_exp = _dot((dts * wgt_s).astype(BF16), expand)
        dec_state = _dot_split_lhs(jnp.broadcast_to(jnp.exp(tot_s), (SUBLANES, LANES)), expand, 3)[0:1, :]
        xc = xs[sl]
        xb = xc.astype(BF16)
        xw = (xc * dtw_exp).astype(BF16)
        pos_t = pos_s.T
        dt_t = dts.T
        y_groups = []
        for g in range(SSD_GROUPS):
            bg = bc[sl, g * SSD_STATE:(g + 1) * SSD_STATE]
            cg = bc[sl, SSD_BC + g * SSD_STATE:SSD_BC + (g + 1) * SSD_STATE]
            cg32 = cg.astype(F32)
            gmat = _dot_nt(cg, bg)
            ht = ht_ref[g]
            htb = ht.astype(BF16)
            glanes = slice(g * gw, (g + 1) * gw)
            pairs = []
            for j in range(hpg // 2):
                plo = g * gw + j * LANES
                rhs = jnp.concatenate([xb[:, plo:plo + LANES], htb[:, j * LANES:(j + 1) * LANES]], axis=0)
                lhs = []
                for hh in range(2):
                    lane = SSD_HEADS * direction + g * hpg + 2 * j + hh
                    col = pos_s[:, lane:lane + 1]
                    row = pos_t[lane:lane + 1, :]
                    seg = (row - col) if rev else (col - row)
                    decay = jnp.where(keep, jnp.exp(seg), 0.0) * dt_t[lane:lane + 1, :]
                    lhs.append(jnp.concatenate([(gmat * decay).astype(BF16),
                                                (cg32 * fac_s[:, lane:lane + 1]).astype(BF16)], axis=1))
                res = _dot(jnp.concatenate(lhs, axis=0), rhs)
                pairs.append(jnp.where(lane_lo, res[:q], res[q:]))
            y_groups.append(jnp.concatenate(pairs, axis=1))
            ht_ref[g] = ht * dec_state[:, glanes] + _dot_tn(bg, xw[:, glanes])
        ys[c] = jnp.concatenate(y_groups, axis=1)
    return jnp.concatenate(ys, axis=0)


def _gla_direction(qh, kh, vh, glog, s_ref, rev):
    t = qh.shape[0]
    q = GLA_CHUNK
    pair = 2 * q
    ri = _iota2((t, t), 0)
    ci = _iota2((t, t), 1)
    order_ok = (ri <= ci) if rev else (ri >= ci)
    prefix = (((ri >> 6) == (ci >> 6)) & order_ok).astype(BF16)
    gc = _dot_split_rhs(prefix, glog, 2)
    q_in = qh * jnp.exp(gc)
    k_in = kh * jnp.exp(-gc)
    pr = _iota2((pair, GLA_HEADS * pair), 0)
    pc = _iota2((pair, GLA_HEADS * pair), 1) & (pair - 1)
    causal = ((pr >> 6) == (pc >> 6)) & ((pc >= pr) if rev else (pc <= pr))
    khead = _iota2((GLA_KEY_DIM, pair), 0) >> 5
    vhead_rows = _iota2((GLA_HEADS * pair, GLA_WIDTH), 0) >> 7
    vhead_cols = _iota2((GLA_HEADS * pair, GLA_WIDTH), 1) >> 6
    blockdiag = (_iota2((GLA_KEY_DIM, GLA_WIDTH), 0) >> 5) == (_iota2((GLA_KEY_DIM, GLA_WIDTH), 1) >> 6)
    outs = [None] * (t // q)
    pair_order = range(t // pair - 1, -1, -1) if rev else range(t // pair)
    for pi in pair_order:
        psl = slice(pi * pair, (pi + 1) * pair)
        k_t = k_in[psl].T
        gc_t = gc[psl].T
        kbd = jnp.concatenate([jnp.where(khead == h, k_t, 0.0) for h in range(GLA_HEADS)], axis=1).astype(BF16)
        att = _dot(q_in[psl].astype(BF16), kbd)
        att = jnp.where(causal, att, 0.0).astype(BF16)
        vp = vh[psl].astype(BF16)
        vbd = jnp.where(vhead_rows == vhead_cols, jnp.concatenate([vp] * GLA_HEADS, axis=0), 0.0)
        o_intra = _dot(att, vbd)
        chunk_order = (1, 0) if rev else (0, 1)
        for cc in chunk_order:
            c = 2 * pi + cc
            sl = slice(c * q, (c + 1) * q)
            edge = cc * q if rev else cc * q + q - 1
            egl_col = jnp.exp(gc_t[:, edge:edge + 1])
            st = s_ref[...]
            o_inter = _dot(q_in[sl].astype(BF16), st.astype(BF16))
            outs[c] = o_intra[cc * q:(cc + 1) * q] + o_inter
            kw_t = (k_t[:, cc * q:(cc + 1) * q] * egl_col).astype(BF16)
            upd = _dot(kw_t, vh[sl].astype(BF16))
            s_ref[...] = st * egl_col + jnp.where(blockdiag, upd, 0.0)
    return jnp.concatenate(outs, axis=0)


def _head_rms(x, w_row):
    n = x.shape[1]
    ones = ((_iota2((n, n), 0) >> 6) == (_iota2((n, n), 1) >> 6)).astype(BF16)
    ms = _dot((x * x).astype(BF16), ones) * (1.0 / HEAD_DIM)
    return x * lax.rsqrt(ms + EPS) * w_row


def _rope(x, cos, sin_signed):
    n = x.shape[1]
    quarter = HEAD_DIM // 4
    first = (_iota2((1, n), 1) & (2 * quarter - 1)) < quarter
    swapped = jnp.where(first, pltpu.roll(x, n - quarter, 1), pltpu.roll(x, quarter, 1))
    return x * cos + swapped * sin_signed


def _attend_block(q_blk, kcat, vt, sink2, edge_masks):
    assert ATT_GROUP == 2 and ATT_KV_HEADS == 2
    blk = q_blk.shape[0]
    lane = _iota2((blk, LANES), 1)
    pieces = []
    for kvh in range(ATT_KV_HEADS):
        qt = q_blk[:, kvh * LANES:(kvh + 1) * LANES]
        rolled = pltpu.roll(qt, HEAD_DIM, 1)
        own = (lane < HEAD_DIM) if kvh == 0 else (lane >= HEAD_DIM)
        for g in range(ATT_GROUP):
            pieces.append(jnp.where(own, qt if g == kvh else rolled, 0.0))
    q_all = jnp.concatenate(pieces, axis=0).astype(BF16)
    st = _dot_nt(kcat, q_all)
    if edge_masks is not None:
        row, keep_before, keep_after = edge_masks
        st = jnp.concatenate([st[:row],
                              jnp.where(keep_before, st[row:row + blk], NEG_INF),
                              st[row + blk:row + 2 * blk],
                              jnp.where(keep_after, st[row + 2 * blk:], NEG_INF)], axis=0)
    m = jnp.maximum(jnp.max(st, axis=0, keepdims=True), sink2)
    p = jnp.exp2(st - m)
    den = jnp.sum(p, axis=0, keepdims=True) + jnp.exp2(sink2 - m)
    pb = p.astype(BF16)
    z = []
    for kvh in range(ATT_KV_HEADS):
        qcols = slice(kvh * ATT_GROUP * blk, (kvh + 1) * ATT_GROUP * blk)
        ot = _dot(vt[kvh * HEAD_DIM:(kvh + 1) * HEAD_DIM], pb[:, qcols]) / den[:, qcols]
        for g in range(ATT_GROUP):
            z.append(ot[:, g * blk:(g + 1) * blk])
    return jnp.concatenate(z, axis=0).T


def _gla_inputs(rest_ref, glog_col):
    qh = rest_ref[:, P_GQ:P_GQ + GLA_KEY_DIM] * (GLA_DK ** -0.5)
    kh = rest_ref[:, P_GK:P_GK + GLA_KEY_DIM]
    vh = rest_ref[:, P_GV:P_GV + GLA_WIDTH]
    return qh, kh, vh, rest_ref[:, glog_col:glog_col + GLA_KEY_DIM]


def _load_states(ht_ref, s_ref, ht0_ref, s0_ref):
    hpg = SSD_HEADS // SSD_GROUPS
    for g in range(SSD_GROUPS):
        blk = ht0_ref[g * hpg:(g + 1) * hpg].reshape(hpg * SSD_HEAD_DIM, SSD_STATE)
        ht_ref[g] = blk.T
    rows = []
    for h in range(GLA_HEADS):
        pieces = []
        if h > 0:
            pieces.append(jnp.zeros((GLA_DK, h * GLA_DV), F32))
        pieces.append(s0_ref[h])
        if h < GLA_HEADS - 1:
            pieces.append(jnp.zeros((GLA_DK, (GLA_HEADS - 1 - h) * GLA_DV), F32))
        rows.append(jnp.concatenate(pieces, axis=1))
    s_ref[...] = jnp.concatenate(rows, axis=0)


def _store_states(ht_ref, s_ref, htf_ref, sf_ref):
    hpg = SSD_HEADS // SSD_GROUPS
    for g in range(SSD_GROUPS):
        htf_ref[g * hpg:(g + 1) * hpg] = ht_ref[g].T.reshape(hpg, SSD_HEAD_DIM, SSD_STATE)
    st = s_ref[...]
    for h in range(GLA_HEADS):
        sf_ref[h] = st[h * GLA_DK:(h + 1) * GLA_DK, h * GLA_DV:(h + 1) * GLA_DV]


HT_SHAPE = (SSD_GROUPS, SSD_STATE, SSD_WIDTH // SSD_GROUPS)
S_SHAPE = (GLA_KEY_DIM, GLA_WIDTH)
SSD_STATE_SHAPE = (SSD_HEADS, SSD_HEAD_DIM, SSD_STATE)
GLA_STATE_SHAPE = (GLA_HEADS, GLA_DK, GLA_DV)


def _mix_bwd_kernel(*refs, is_ctx, n_tiles):
    it = iter(refs)
    xs_ref = next(it)
    bc_ref = next(it)
    rest_ref = next(it)
    cp_ref = next(it)
    if is_ctx:
        next(it)
        next(it)
    else:
        ht0_ref = next(it)
        s0_ref = next(it)
    yb_ref = next(it)
    ob_ref = next(it)
    if is_ctx:
        htf_ref = next(it)
        sf_ref = next(it)
    ht_ref = next(it)
    s_ref = next(it)

    @pl.when(pl.program_id(1) == 0)
    def _():
        for u in range(SEQ_PER_STEP):
            if is_ctx:
                ht_ref[u] = jnp.zeros(HT_SHAPE, F32)
                s_ref[u] = jnp.zeros(S_SHAPE, F32)
            else:
                _load_states(ht_ref.at[u], s_ref.at[u], ht0_ref.at[u], s0_ref.at[u])

    a_small_row = -jnp.exp(cp_ref[CP_ALOG:CP_ALOG + 1, :LANES])
    for u in range(SEQ_PER_STEP):
        rest_u = rest_ref.at[u]
        yb_ref[u] = _ssd_direction(xs_ref[u], bc_ref[u], rest_u[:, P_DT:P_DT + LANES], a_small_row, ht_ref.at[u],
                                   True, 1)
        qh, kh, vh, glog = _gla_inputs(rest_u, P_GLR)
        ob_ref[u] = _gla_direction(qh, kh, vh, glog, s_ref.at[u], True)
        if is_ctx:
            _store_states(ht_ref.at[u], s_ref.at[u], htf_ref.at[u], sf_ref.at[u])


def _state_specs(l, direction):
    return [pl.BlockSpec((SEQ_PER_STEP, None, None) + SSD_STATE_SHAPE, lambda s, t: (s, l, direction, 0, 0, 0)),
            pl.BlockSpec((SEQ_PER_STEP, None, None) + GLA_STATE_SHAPE, lambda s, t: (s, l, direction, 0, 0, 0))]


def _per_seq(a, n_seq):
    return a.reshape(n_seq, a.shape[0] // n_seq, a.shape[1])


def _mix_bwd(xs, bc, rest, cpack, l, states, *, n_seq, n_tiles, is_ctx):
    u = SEQ_PER_STEP
    seq_len = n_tiles * TILE
    row = lambda s, t: (s, n_tiles - 1 - t, 0)
    in_specs = [
        pl.BlockSpec((u, TILE, SSD_WIDTH), row),
        pl.BlockSpec((u, TILE, 2 * SSD_BC), row),
        pl.BlockSpec((u, TILE, BWD_COLS), row),
        pl.BlockSpec((None, CP_ROWS, D_MODEL), lambda s, t: (l, 0, 0)),
    ]
    args = [_per_seq(xs, n_seq), _per_seq(bc, n_seq), _per_seq(rest, n_seq), cpack]
    out_specs = [pl.BlockSpec((u, TILE, SSD_WIDTH), row), pl.BlockSpec((u, TILE, GLA_WIDTH), row)]
    out_shape = [jax.ShapeDtypeStruct((n_seq, seq_len, SSD_WIDTH), F32),
                 jax.ShapeDtypeStruct((n_seq, seq_len, GLA_WIDTH), F32)]
    aliases = {}
    if is_ctx:
        in_specs += [pl.BlockSpec(memory_space=pl.ANY), pl.BlockSpec(memory_space=pl.ANY)]
        out_specs += _state_specs(l, 1)
        out_shape += [jax.ShapeDtypeStruct(a.shape, a.dtype) for a in states]
        aliases = {len(args): len(out_shape) - 2, len(args) + 1: len(out_shape) - 1}
    else:
        in_specs += _state_specs(l, 1)
    args += list(states)
    return pl.pallas_call(
        functools.partial(_mix_bwd_kernel, is_ctx=is_ctx, n_tiles=n_tiles),
        grid=(n_seq // u, n_tiles),
        in_specs=in_specs,
        out_specs=out_specs,
        out_shape=out_shape,
        input_output_aliases=aliases,
        scratch_shapes=[pltpu.VMEM((u,) + HT_SHAPE, F32), pltpu.VMEM((u,) + S_SHAPE, F32)],
        compiler_params=pltpu.CompilerParams(dimension_semantics=("arbitrary", "arbitrary"),
                                             vmem_limit_bytes=VMEM_LIMIT_BYTES),
        name="mix_bwd_ctx" if is_ctx else "mix_bwd_lat",
    )(*args)


def _mix_fwd_kernel(*refs, is_ctx, n_tiles, layer):
    it = iter(refs)
    sink_ref = next(it)
    x_ref = next(it)
    gate_ref = next(it)
    rest_ref = next(it)
    xs_ref = next(it)
    bc_ref = next(it)
    yb_ref = next(it)
    ob_ref = next(it)
    cp_ref = next(it)
    wout_ref = next(it)
    if is_ctx:
        for _ in range(4):
            next(it)
    else:
        ht0_ref = next(it)
        s0_ref = next(it)
        kvprev_ref = next(it)
        kvnext_ref = next(it)
        cs_ref = next(it)
        csp_ref = next(it)
        csn_ref = next(it)
        kc_ref = next(it)
        vc_ref = next(it)
    o_ref = next(it)
    if is_ctx:
        kout_ref = next(it)
        vout_ref = next(it)
        htf_ref = next(it)
        sf_ref = next(it)
    ht_ref = next(it)
    s_ref = next(it)

    tile_idx = pl.program_id(1)

    @pl.when(tile_idx == 0)
    def _():
        for u in range(SEQ_PER_STEP):
            if is_ctx:
                ht_ref[u] = jnp.zeros(HT_SHAPE, F32)
                s_ref[u] = jnp.zeros(S_SHAPE, F32)
            else:
                _load_states(ht_ref.at[u], s_ref.at[u], ht0_ref.at[u], s0_ref.at[u])

    mixed = [
        _fwd_one_sequence(
            u, tile_idx, sink_ref, rest_ref.at[u], xs_ref, bc_ref, yb_ref, ob_ref, cp_ref, ht_ref.at[u], s_ref.at[u],
            (kout_ref, vout_ref, htf_ref.at[u], sf_ref.at[u]) if is_ctx else
            (kvprev_ref.at[u], kvnext_ref.at[u], cs_ref, csp_ref, csn_ref, kc_ref, vc_ref),
            is_ctx=is_ctx, n_tiles=n_tiles, layer=layer)
        for u in range(SEQ_PER_STEP)]
    y = _dot(jnp.concatenate(mixed, axis=0), wout_ref[...])
    for u in range(SEQ_PER_STEP):
        gate = gate_ref[...] if is_ctx else gate_ref[u]
        o_ref[u] = x_ref[u] + gate * y[u * TILE:(u + 1) * TILE]


def _fwd_one_sequence(u, tile_idx, sink_ref, rest_ref, xs_ref, bc_ref, yb_ref, ob_ref, cp_ref, ht_ref, s_ref, extra,
                      *, is_ctx, n_tiles, layer):
    if is_ctx:
        kout_ref, vout_ref, htf_ref, sf_ref = extra
    else:
        kvprev_ref, kvnext_ref, cs_ref, csp_ref, csn_ref, kc_ref, vc_ref = extra
    t = TILE
    xs = xs_ref[u]
    a_small_row = -jnp.exp(cp_ref[CP_ALOG:CP_ALOG + 1, :LANES])
    knw = cp_ref[CP_KNW:CP_KNW + 1, :ATT_KV_DIM]

    y_f = _ssd_direction(xs, bc_ref[u], rest_ref[:, P_DT:P_DT + LANES], a_small_row, ht_ref, False, 0)
    y = y_f + yb_ref[u] + xs * cp_ref[CP_DSKIP:CP_DSKIP + 1, :SSD_WIDTH]
    y = y * _silu(rest_ref[:, P_Z:P_Z + SSD_WIDTH])
    y_ssd = y * lax.rsqrt(jnp.mean(y * y, axis=-1, keepdims=True) + EPS) * cp_ref[CP_SSDNW:CP_SSDNW + 1, :SSD_WIDTH]

    qh, kh, vh, glog = _gla_inputs(rest_ref, P_GLF)
    o = _gla_direction(qh, kh, vh, glog, s_ref, False) + ob_ref[u]
    y_gla = (_head_rms(o, cp_ref[CP_GLANW:CP_GLANW + 1, :GLA_WIDTH])
             * _silu(rest_ref[:, P_GG:P_GG + GLA_WIDTH]))

    qn = _head_rms(rest_ref[:, P_AQ:P_AQ + ATT_WIDTH], cp_ref[CP_QNW:CP_QNW + 1, :ATT_WIDTH])
    kn = _head_rms(rest_ref[:, P_AK:P_AK + ATT_KV_DIM], knw)
    vv = rest_ref[:, P_AV:P_AV + ATT_KV_DIM]
    qscale = (HEAD_DIM ** -0.5) * LOG2E
    blk = ATT_BLOCK
    n_blk = t // blk
    sink2 = jnp.concatenate([jnp.full((1, blk), sink_ref[layer, h] * LOG2E, F32) for h in range(ATT_HEADS)], axis=1)
    if is_ctx:
        kout_ref[u] = kn
        vout_ref[u] = vv
        qs = qn * qscale
        kb = kn.astype(BF16)
        v_t = vv.T.astype(BF16)
        y_att = jnp.concatenate(
            [_attend_block(qs[nb * blk:(nb + 1) * blk], kb, v_t, sink2, None) for nb in range(n_blk)], axis=0)
    else:
        cos = cs_ref[:, :ATT_KV_DIM]
        sin = cs_ref[:, ATT_KV_DIM:]
        qr = _rope(qn, jnp.concatenate([cos, cos], axis=1), jnp.concatenate([sin, sin], axis=1)) * qscale
        k_halo_p = _rope(_head_rms(kvprev_ref[:, :ATT_KV_DIM], knw), csp_ref[:, :ATT_KV_DIM], csp_ref[:, ATT_KV_DIM:])
        k_halo_n = _rope(_head_rms(kvnext_ref[:, :ATT_KV_DIM], knw), csn_ref[:, :ATT_KV_DIM], csn_ref[:, ATT_KV_DIM:])
        k_loc = jnp.concatenate([k_halo_p, _rope(kn, cos, sin), k_halo_n], axis=0).astype(BF16)
        kc = kc_ref[u].astype(BF16)
        n_ctx = kc.shape[0]
        v_all_t = jnp.concatenate([vc_ref[u], kvprev_ref[:, ATT_KV_DIM:], vv, kvnext_ref[:, ATT_KV_DIM:]],
                                  axis=0).T.astype(BF16)
        kj = _iota2((blk, ATT_HEADS * blk), 0)
        qi = _iota2((blk, ATT_HEADS * blk), 1) & (blk - 1)
        blocks = []
        for nb in range(n_blk):
            kcat = jnp.concatenate([kc, k_loc[nb * blk:(nb + 3) * blk]], axis=0)
            vt = jnp.concatenate([v_all_t[:, :n_ctx], v_all_t[:, n_ctx + nb * blk:n_ctx + (nb + 3) * blk]], axis=1)
            off_p = jnp.where(tile_idx > 0, 0, 2 * blk) if nb == 0 else 0
            off_n = jnp.where(tile_idx < n_tiles - 1, 0, 2 * blk) if nb == n_blk - 1 else 0
            masks = (n_ctx, kj >= qi + off_p, kj <= qi - off_n)
            blocks.append(_attend_block(qr[nb * blk:(nb + 1) * blk], kcat, vt, sink2, masks))
        y_att = jnp.concatenate(blocks, axis=0)
    y_att = y_att * _silu(rest_ref[:, P_AG:P_AG + ATT_WIDTH])

    if is_ctx:
        _store_states(ht_ref, s_ref, htf_ref, sf_ref)
    return jnp.concatenate([y_ssd, y_gla, y_att], axis=1).astype(BF16)


def _mix_fwd(x, mod5, gate_row, xs, bc, rest, yb, ob, cpack, sink, w_out, l, extra, *, n_seq, n_tiles, is_ctx):
    u = SEQ_PER_STEP
    seq_len = n_tiles * TILE
    row = lambda s, t: (s, t, 0)
    rest3 = _per_seq(rest, n_seq)
    if is_ctx:
        gate_spec = _mod_spec(l, 2, lambda s, t: gate_row(s))
    else:
        gate_spec = pl.BlockSpec((None, u, None, 1, D_MODEL), lambda s, t: (l, s, 2, 0, 0))
    in_specs = [
        pl.BlockSpec(memory_space=pltpu.SMEM),
        pl.BlockSpec((u, TILE, D_MODEL), row),
        gate_spec,
        pl.BlockSpec((u, TILE, D_REST), row),
        pl.BlockSpec((u, TILE, SSD_WIDTH), row),
        pl.BlockSpec((u, TILE, 2 * SSD_BC), row),
        pl.BlockSpec((u, TILE, SSD_WIDTH), row),
        pl.BlockSpec((u, TILE, GLA_WIDTH), row),
        pl.BlockSpec((None, CP_ROWS, D_MODEL), lambda s, t: (l, 0, 0)),
        pl.BlockSpec((None, D_MIX, D_MODEL), lambda s, t: (l, 0, 0)),
    ]
    args = [sink, _per_seq(x, n_seq), mod5, rest3, _per_seq(xs, n_seq), _per_seq(bc, n_seq), yb, ob, cpack, w_out]
    out_specs = [pl.BlockSpec((u, TILE, D_MODEL), row)]
    out_shape = [jax.ShapeDtypeStruct((n_seq, seq_len, D_MODEL), F32)]
    aliases = {}
    if is_ctx:
        kv_spec = pl.BlockSpec((u, None, TILE, ATT_KV_DIM), lambda s, t: (s, l, t, 0))
        in_specs += [pl.BlockSpec(memory_space=pl.ANY)] * 4
        out_specs += [kv_spec, kv_spec] + _state_specs(l, 0)
        out_shape += [jax.ShapeDtypeStruct(a.shape, a.dtype) for a in extra]
        aliases = {len(args) + i: 1 + i for i in range(4)}
        args += list(extra)
    else:
        state_ssd, state_gla, rope_cs, cache_k, cache_v = extra
        per_blk = TILE // ATT_BLOCK
        kv_col = P_AK // (2 * ATT_KV_DIM)
        last_blk = n_tiles * per_blk - 1
        prev_blk = lambda t: jnp.maximum(t * per_blk - 1, 0)
        next_blk = lambda t: jnp.minimum((t + 1) * per_blk, last_blk)
        n_ctx = cache_k.shape[2]
        ctx_spec = pl.BlockSpec((u, None, n_ctx, ATT_KV_DIM), lambda s, t: (s, l, 0, 0))
        in_specs += _state_specs(l, 0) + [
            pl.BlockSpec((u, ATT_BLOCK, 2 * ATT_KV_DIM), lambda s, t: (s, prev_blk(t), kv_col)),
            pl.BlockSpec((u, ATT_BLOCK, 2 * ATT_KV_DIM), lambda s, t: (s, next_blk(t), kv_col)),
            pl.BlockSpec((TILE, 2 * ATT_KV_DIM), lambda s, t: (t, 0)),
            pl.BlockSpec((ATT_BLOCK, 2 * ATT_KV_DIM), lambda s, t: (prev_blk(t), 0)),
            pl.BlockSpec((ATT_BLOCK, 2 * ATT_KV_DIM), lambda s, t: (next_blk(t), 0)),
            ctx_spec,
            ctx_spec,
        ]
        args += [state_ssd, state_gla, rest3, rest3, rope_cs, rope_cs, rope_cs, cache_k, cache_v]
    outs = pl.pallas_call(
        functools.partial(_mix_fwd_kernel, is_ctx=is_ctx, n_tiles=n_tiles, layer=l),
        grid=(n_seq // u, n_tiles),
        in_specs=in_specs,
        out_specs=out_specs,
        out_shape=out_shape,
        input_output_aliases=aliases,
        scratch_shapes=[pltpu.VMEM((u,) + HT_SHAPE, F32), pltpu.VMEM((u,) + S_SHAPE, F32)],
        compiler_params=pltpu.CompilerParams(dimension_semantics=("arbitrary", "arbitrary"),
                                             vmem_limit_bytes=VMEM_LIMIT_BYTES),
        name="mix_fwd_ctx" if is_ctx else "mix_fwd_lat",
    )(*args)
    return [outs[0].reshape(n_seq * seq_len, D_MODEL)] + list(outs[1:])


def _pad_cols(a, width):
    return jnp.pad(a, [(0, 0)] * (a.ndim - 1) + [(0, width - a.shape[-1])])


def _prep_w_in(w_in):
    sizes = (SSD_WIDTH, SSD_CONV_DIM, 2 * SSD_HEADS, GLA_KEY_DIM, GLA_KEY_DIM, GLA_WIDTH, GLA_WIDTH, 2 * GLA_LOWRANK,
             ATT_WIDTH, ATT_KV_DIM, ATT_KV_DIM, ATT_WIDTH)
    parts, start = [], 0
    for size in sizes:
        parts.append(w_in[..., start:start + size])
        start += size
    z, xbc, dt, gq, gk, gv, gg, glr, aq, ak, av, ag = parts
    cols = [xbc, _pad_cols(dt, LANES), gq, gk, _pad_cols(glr, LANES), gv, ak, av, z, gg, aq, ag]
    return jnp.concatenate(cols, axis=-1).astype(BF16)


def _pack_small_params(norm_w, conv_w, conv_b, ssd_a_log, ssd_dt_bias, ssd_d, ssd_norm_w, gla_gk_b, gla_norm_w,
                       attn_q_norm, attn_k_norm):
    depth = norm_w.shape[0]
    row = lambda a: _pad_cols(a.reshape(depth, 1, -1), D_MODEL)
    blank = lambda n: jnp.zeros((depth, n, D_MODEL), F32)
    rows = [
        _pad_cols(conv_w, D_MODEL), blank(CP_CONVB - SSD_CONV),
        row(conv_b), row(ssd_dt_bias), row(ssd_a_log), row(gla_gk_b[:, 0]), row(gla_gk_b[:, 1]),
        row(jnp.repeat(ssd_d, SSD_HEAD_DIM, axis=-1)), row(ssd_norm_w), row(jnp.tile(gla_norm_w, (1, GLA_HEADS))),
        row(jnp.tile(attn_q_norm, (1, ATT_HEADS))), row(jnp.tile(attn_k_norm, (1, ATT_KV_HEADS))), row(norm_w),
        blank(CP_ROWS - CP_NW - 1),
    ]
    return jnp.concatenate(rows, axis=1)


def _pad_gate_weights(gla_gk_up):
    depth = gla_gk_up.shape[0]
    w = jnp.zeros((depth, LANES, 2 * GLA_KEY_DIM), F32)
    for d in range(2):
        w = w.at[:, d * GLA_LOWRANK:(d + 1) * GLA_LOWRANK, d * GLA_KEY_DIM:(d + 1) * GLA_KEY_DIM].set(gla_gk_up[:, d])
    return w


def _rope_tables(seq_len):
    quarter = HEAD_DIM // 4
    rows = seq_len // GRID_W
    row_pos = jnp.repeat(jnp.arange(rows, dtype=F32), GRID_W)
    col_pos = jnp.tile(jnp.arange(GRID_W, dtype=F32), rows)
    inv = ROPE_THETA ** (-jnp.arange(quarter, dtype=F32) / quarter)
    ang_r = row_pos[:, None] * inv[None, :]
    ang_c = col_pos[:, None] * inv[None, :]
    cos = jnp.concatenate([jnp.cos(ang_r), jnp.cos(ang_r), jnp.cos(ang_c), jnp.cos(ang_c)], axis=1)
    sin = jnp.concatenate([-jnp.sin(ang_r), jnp.sin(ang_r), -jnp.sin(ang_c), jnp.sin(ang_c)], axis=1)
    return jnp.tile(cos, (1, ATT_KV_HEADS)), jnp.tile(sin, (1, ATT_KV_HEADS))


def kernel(x_prompt, x_sample, c, cache_k, cache_v, state_ssd, state_gla, c_ctx, w_ada, b_ada, norm_w, w_in, conv_w,
           conv_b, ssd_a_log, ssd_dt_bias, ssd_d, ssd_norm_w, gla_gk_up, gla_gk_b, gla_norm_w, attn_q_norm, attn_k_norm,
           attn_sink, w_out):
    batch, seq, _ = x_prompt.shape
    dec_batch, dec_seq, _ = x_sample.shape
    depth = w_in.shape[0]
    past = cache_k.shape[2]
    assert seq % TILE == 0 and dec_seq % TILE == 0
    assert (batch * seq) % IN_TILE == 0 and dec_seq % IN_TILE == 0

    rows = -(-(dec_batch + 1) // SUBLANES) * SUBLANES
    cond = jnp.zeros((rows, D_MODEL), F32).at[:dec_batch].set(c).at[dec_batch].set(c_ctx)
    mod = _modulation(cond, w_ada, b_ada)
    mod5 = mod.reshape(depth, rows, 3, 1, D_MODEL)

    w_in_p = _prep_w_in(w_in)
    w_out_b = w_out.astype(BF16)
    cpack = _pack_small_params(norm_w, conv_w, conv_b, ssd_a_log, ssd_dt_bias, ssd_d, ssd_norm_w, gla_gk_b,
                               gla_norm_w, attn_q_norm, attn_k_norm)
    gk_w = _pad_gate_weights(gla_gk_up)
    rope_cs = jnp.concatenate(_rope_tables(dec_seq), axis=1)
    cache_k2 = cache_k.reshape(dec_batch, depth, past, ATT_KV_DIM)
    cache_v2 = cache_v.reshape(dec_batch, depth, past, ATT_KV_DIM)

    x_ctx = x_prompt.reshape(batch * seq, D_MODEL)
    x_lat = x_sample.reshape(dec_batch * dec_seq, D_MODEL)
    ctx_tiles = seq // TILE
    lat_tiles = dec_seq // TILE
    lat_in_per_seq = dec_seq // IN_TILE

    new_k = jnp.zeros((batch, depth, seq, ATT_KV_DIM), F32)
    new_v = jnp.zeros((batch, depth, seq, ATT_KV_DIM), F32)
    new_ssd = jnp.zeros((batch, depth, 2) + SSD_STATE_SHAPE, F32)
    new_gla = jnp.zeros((batch, depth, 2) + GLA_STATE_SHAPE, F32)

    for l in range(depth):
        xs, bc, rest = _in_proj(x_ctx, mod5, cpack, w_in_p, gk_w, l, lambda i: dec_batch, seq)
        yb, ob, new_ssd, new_gla = _mix_bwd(xs, bc, rest, cpack, l, (new_ssd, new_gla), n_seq=batch,
                                            n_tiles=ctx_tiles, is_ctx=True)
        x_ctx, new_k, new_v, new_ssd, new_gla = _mix_fwd(
            x_ctx, mod5, lambda s: dec_batch, xs, bc, rest, yb, ob, cpack, attn_sink, w_out_b, l,
            (new_k, new_v, new_ssd, new_gla), n_seq=batch, n_tiles=ctx_tiles, is_ctx=True)

        xs, bc, rest = _in_proj(x_lat, mod5, cpack, w_in_p, gk_w, l, lambda i: i // lat_in_per_seq, dec_seq)
        yb, ob = _mix_bwd(xs, bc, rest, cpack, l, (state_ssd, state_gla), n_seq=dec_batch, n_tiles=lat_tiles,
                          is_ctx=False)
        (x_lat,) = _mix_fwd(x_lat, mod5, lambda s: s, xs, bc, rest, yb, ob, cpack, attn_sink, w_out_b, l,
                            (state_ssd, state_gla, rope_cs, cache_k2, cache_v2), n_seq=dec_batch, n_tiles=lat_tiles,
                            is_ctx=False)

    return (x_ctx.reshape(batch, seq, D_MODEL), x_lat.reshape(dec_batch, dec_seq, D_MODEL),
            new_k.reshape(batch, depth, seq, ATT_KV_HEADS, HEAD_DIM),
            new_v.reshape(batch, depth, seq, ATT_KV_HEADS, HEAD_DIM), new_ssd, new_gla)
```

```python
import functools
import math

import jax
import jax.numpy as jnp
from jax import lax
from jax.experimental import pallas as pl
from jax.experimental.pallas import tpu as pltpu

D_MODEL = 1024
GRID_W = 64
SSD_HEADS = 8
SSD_HEAD_DIM = 64
SSD_WIDTH = 512
SSD_GROUPS = 2
SSD_STATE = 128
SSD_BC = SSD_GROUPS * SSD_STATE
SSD_CONV = 5
SSD_CONV_DIM = SSD_WIDTH + 2 * SSD_BC
SSD_CHUNK = 128
GLA_HEADS = 4
GLA_DK = 32
GLA_DV = 64
GLA_KEY_DIM = GLA_HEADS * GLA_DK
GLA_WIDTH = GLA_HEADS * GLA_DV
GLA_LOWRANK = 16
GLA_GATE_NORM = 16.0
GLA_CHUNK = 64
ATT_HEADS = 4
ATT_KV_HEADS = 2
ATT_GROUP = ATT_HEADS // ATT_KV_HEADS
HEAD_DIM = 64
ATT_WIDTH = ATT_HEADS * HEAD_DIM
ATT_KV_DIM = ATT_KV_HEADS * HEAD_DIM
WINDOW = 128
ATT_BLOCK = 128
ROPE_THETA = 10000.0
D_MIX = SSD_WIDTH + GLA_WIDTH + ATT_WIDTH
EPS = 1e-6
NEG_INF = -1e30
LOG2E = math.log2(math.e)

LANES = 128
SUBLANES = 8
VMEM_LIMIT_BYTES = 56 * 1024 * 1024

P_DT = 0
P_GQ = P_DT + LANES
P_GK = P_GQ + GLA_KEY_DIM
P_GLR = P_GK + GLA_KEY_DIM
P_GV = P_GLR + LANES
P_AK = P_GV + GLA_WIDTH
P_AV = P_AK + ATT_KV_DIM
P_Z = P_AV + ATT_KV_DIM
P_GG = P_Z + SSD_WIDTH
P_AQ = P_GG + GLA_WIDTH
P_AG = P_AQ + ATT_WIDTH
D_LIN = P_AG + ATT_WIDTH
D_PROJ = SSD_CONV_DIM + D_LIN
P_GLF = D_LIN
D_REST = D_LIN + GLA_KEY_DIM
BWD_COLS = P_AK

CP_CONVW = 0
CP_CONVB = 8
CP_DTB = 9
CP_ALOG = 10
CP_GKB = 11
CP_DSKIP = 13
CP_SSDNW = 14
CP_GLANW = 15
CP_QNW = 16
CP_KNW = 17
CP_NW = 18
CP_ROWS = 24

TILE = 256
IN_TILE = 512
HALO = SUBLANES
CONV_COLS = 256
CONV_ROWS = 64
SEQ_PER_STEP = 2

HI = lax.Precision.HIGHEST
F32 = jnp.float32
BF16 = jnp.bfloat16

NT_DIMS = (((1,), (1,)), ((), ()))
TN_DIMS = (((0,), (0,)), ((), ()))


def _dot(a, b):
    return jnp.dot(a, b, preferred_element_type=F32)


def _dot_nt(a, b):
    return lax.dot_general(a, b, NT_DIMS, preferred_element_type=F32)


def _dot_tn(a, b):
    return lax.dot_general(a, b, TN_DIMS, preferred_element_type=F32)


def _split(x, parts):
    out = []
    for _ in range(parts - 1):
        piece = x.astype(BF16)
        out.append(piece)
        x = x - piece.astype(F32)
    out.append(x.astype(BF16))
    return out


def _dot_split_lhs(x, w_bf16, parts):
    acc = None
    for piece in _split(x, parts):
        term = _dot(piece, w_bf16)
        acc = term if acc is None else acc + term
    return acc


def _dot_split_rhs(w_bf16, x, parts):
    acc = None
    for piece in _split(x, parts):
        term = _dot(w_bf16, piece)
        acc = term if acc is None else acc + term
    return acc


def _sigmoid(x):
    return 1.0 / (1.0 + jnp.exp(-x))


def _silu(x):
    return x * _sigmoid(x)


def _softplus(x):
    return jnp.maximum(x, 0.0) + jnp.log(1.0 + jnp.exp(-jnp.abs(x)))


def _log_sigmoid(x):
    return jnp.minimum(x, 0.0) - jnp.log(1.0 + jnp.exp(-jnp.abs(x)))


def _iota2(shape, dim):
    return lax.broadcasted_iota(jnp.int32, shape, dim)


def _run_interleaved(tasks):
    results = [None] * len(tasks)
    live = list(range(len(tasks)))
    while live:
        for i in list(live):
            try:
                next(tasks[i])
            except StopIteration as done:
                results[i] = done.value
                live.remove(i)
    return results


def _mod_kernel(cond_ref, w_ref, b_ref, o_ref):
    o_ref[...] = jnp.dot(_silu(cond_ref[...]), w_ref[...], preferred_element_type=F32, precision=HI) + b_ref[...]


def _modulation(cond, w_ada, b_ada):
    depth = w_ada.shape[0]
    rows = cond.shape[0]
    return pl.pallas_call(
        _mod_kernel,
        grid=(depth, 3),
        in_specs=[
            pl.BlockSpec((rows, D_MODEL), lambda l, j: (0, 0)),
            pl.BlockSpec((None, D_MODEL, D_MODEL), lambda l, j: (l, 0, j)),
            pl.BlockSpec((None, 1, D_MODEL), lambda l, j: (l, 0, j)),
        ],
        out_specs=pl.BlockSpec((None, rows, D_MODEL), lambda l, j: (l, 0, j)),
        out_shape=jax.ShapeDtypeStruct((depth, rows, 3 * D_MODEL), F32),
        compiler_params=pltpu.CompilerParams(dimension_semantics=("arbitrary", "arbitrary")),
        name="modulation",
    )(cond, w_ada, b_ada.reshape(depth, 1, 3 * D_MODEL))


def _in_proj_kernel(x_ref, xprev_ref, xnext_ref, shift_ref, scale_ref, cp_ref, w_ref, gkw_ref, xs_ref, bc_ref,
                    rest_ref, *, seq_len):
    nw = cp_ref[CP_NW:CP_NW + 1, :]
    scale1 = 1.0 + scale_ref[...]
    shift = shift_ref[...]

    def modnorm(x):
        xn = x * lax.rsqrt(jnp.mean(x * x, axis=-1, keepdims=True) + EPS) * nw
        return (xn * scale1 + shift).astype(BF16)

    h = modnorm(x_ref[...])
    t = h.shape[0]
    h_up = jnp.concatenate([modnorm(xprev_ref[...]), h, modnorm(xnext_ref[...])], axis=0)
    seg = min(seq_len, t)
    pos0 = (pl.program_id(0) * t) & (seq_len - 1)
    zero_halo = jnp.zeros((HALO, CONV_COLS), F32)

    for c0 in range(0, SSD_CONV_DIM, CONV_COLS):
        up = _dot(h_up, w_ref[:, c0:c0 + CONV_COLS])
        conv_w = cp_ref[CP_CONVW:CP_CONVW + SUBLANES, c0:c0 + CONV_COLS]
        conv_b = cp_ref[CP_CONVB:CP_CONVB + 1, c0:c0 + CONV_COLS]
        for a in range(0, t, CONV_ROWS):
            if a % seq_len == 0 and seq_len <= t:
                prev = zero_halo
            elif a % seg == 0:
                prev = jnp.where(pos0 + a > 0, up[a:a + HALO, :], 0.0)
            else:
                prev = up[a:a + HALO, :]
            b = a + CONV_ROWS
            if b % seq_len == 0 and seq_len <= t:
                nxt = zero_halo
            elif b % seg == 0:
                nxt = jnp.where(pos0 + b < seq_len, up[HALO + b:2 * HALO + b, :], 0.0)
            else:
                nxt = up[HALO + b:2 * HALO + b, :]
            xbc = _conv_silu(up[HALO + a:HALO + b, :], prev, nxt, conv_w, conv_b)
            if c0 < SSD_WIDTH:
                xs_ref[a:b, c0:c0 + CONV_COLS] = xbc
            else:
                bc_ref[a:b, c0 - SSD_WIDTH:c0 - SSD_WIDTH + CONV_COLS] = xbc.astype(BF16)

    res = _dot(h, w_ref[:, SSD_CONV_DIM:])
    rest_ref[:, P_DT:P_DT + LANES] = _softplus(res[:, P_DT:P_DT + LANES] + cp_ref[CP_DTB:CP_DTB + 1, :LANES])
    rest_ref[:, P_GQ:P_GLR] = res[:, P_GQ:P_GLR]
    rest_ref[:, P_GV:D_LIN] = res[:, P_GV:]
    lr_hi, lr_lo = _split(res[:, P_GLR:P_GLR + LANES], 2)
    for d, col in ((0, P_GLF), (1, P_GLR)):
        w_hi, w_lo = _split(gkw_ref[:, d * GLA_KEY_DIM:(d + 1) * GLA_KEY_DIM], 2)
        gate = (_dot(lr_hi, w_hi) + _dot(lr_lo, w_hi) + _dot(lr_hi, w_lo)
                + cp_ref[CP_GKB + d:CP_GKB + d + 1, :GLA_KEY_DIM])
        rest_ref[:, col:col + GLA_KEY_DIM] = _log_sigmoid(gate) * (1.0 / GLA_GATE_NORM)


def _mod_spec(l, part, row_of):
    return pl.BlockSpec((None, None, None, 1, D_MODEL), lambda *ids: (l, row_of(*ids), part, 0, 0))


def _in_proj(x, mod5, cpack, w_in_p, gk_w, l, mod_row, seq_len):
    n = x.shape[0]
    per = IN_TILE // HALO
    last_halo = n // HALO - 1
    assert seq_len & (seq_len - 1) == 0
    return pl.pallas_call(
        functools.partial(_in_proj_kernel, seq_len=seq_len),
        grid=(n // IN_TILE,),
        in_specs=[
            pl.BlockSpec((IN_TILE, D_MODEL), lambda i: (i, 0)),
            pl.BlockSpec((HALO, D_MODEL), lambda i: (jnp.maximum(i * per - 1, 0), 0)),
            pl.BlockSpec((HALO, D_MODEL), lambda i: (jnp.minimum((i + 1) * per, last_halo), 0)),
            _mod_spec(l, 0, mod_row),
            _mod_spec(l, 1, mod_row),
            pl.BlockSpec((None, CP_ROWS, D_MODEL), lambda i: (l, 0, 0)),
            pl.BlockSpec((None, D_MODEL, D_PROJ), lambda i: (l, 0, 0)),
            pl.BlockSpec((None, LANES, 2 * GLA_KEY_DIM), lambda i: (l, 0, 0)),
        ],
        out_specs=[pl.BlockSpec((IN_TILE, SSD_WIDTH), lambda i: (i, 0)),
                   pl.BlockSpec((IN_TILE, 2 * SSD_BC), lambda i: (i, 0)),
                   pl.BlockSpec((IN_TILE, D_REST), lambda i: (i, 0))],
        out_shape=[jax.ShapeDtypeStruct((n, SSD_WIDTH), F32), jax.ShapeDtypeStruct((n, 2 * SSD_BC), BF16),
                   jax.ShapeDtypeStruct((n, D_REST), F32)],
        compiler_params=pltpu.CompilerParams(dimension_semantics=("arbitrary",),
                                             vmem_limit_bytes=VMEM_LIMIT_BYTES),
        name="in_proj",
    )(x, x, x, mod5, mod5, cpack, w_in_p, gk_w)


def _conv_silu(main, prev, nxt, conv_w, conv_b):
    t = main.shape[0]
    up = jnp.concatenate([prev, main, nxt], axis=0)
    rows = t + 2 * HALO
    pad = (SSD_CONV - 1) // 2
    acc = conv_b
    for k in range(SSD_CONV):
        off = k - pad
        shifted = up if off == 0 else pltpu.roll(up, (rows - off) % rows, 0)
        acc = acc + conv_w[k:k + 1, :] * shifted[HALO:HALO + t, :]
    return _silu(acc)


def _ssd_direction(xs, bc, dt_small, a_small_row, ht_ref, rev, direction):
    t = xs.shape[0]
    q = SSD_CHUNK
    hpg = SSD_HEADS // SSD_GROUPS
    gw = hpg * SSD_HEAD_DIM
    er = _iota2((LANES, SSD_WIDTH), 0)
    ec = _iota2((LANES, SSD_WIDTH), 1)
    expand = (er == (ec >> 6) + SSD_HEADS * direction).astype(BF16)
    a_small = dt_small * a_small_row
    ri = _iota2((q, q), 0)
    ci = _iota2((q, q), 1)
    tri_incl = (ri >= ci).astype(BF16)
    keep = (ri <= ci) if rev else (ri >= ci)
    lane_lo = _iota2((q, LANES), 1) < SSD_HEAD_DIM
    ys = [None] * (t // q)
    order = range(t // q - 1, -1, -1) if rev else range(t // q)
    for c in order:
        sl = slice(c * q, (c + 1) * q)
        a_s = a_small[sl]
        dts = dt_small[sl]
        cs_s = _dot_split_rhs(tri_incl, a_s, 2)
        yield
        tot_s = cs_s[q - 1:q, :]
        pos_s = (cs_s - a_s) if rev else cs_s
        if rev:
            fac_s = jnp.exp(tot_s - pos_s)
            wgt_s = jnp.exp(pos_s)
        else:
            fac_s = jnp.exp(pos_s)
            wgt_s = jnp.exp(tot_s - pos_s)
        yield
        dtw_exp = _dot((dts * wgt_s).astype(BF16), expand)
        dec_state = _dot_split_lhs(jnp.broadcast_to(jnp.exp(tot_s), (SUBLANES, LANES)), expand, 3)[0:1, :]
        yield
        xc = xs[sl]
        xb = xc.astype(BF16)
        xw = (xc * dtw_exp).astype(BF16)
        pos_t = pos_s.T
        dt_t = dts.T
        yield
        y_groups = []
        for g in range(SSD_GROUPS):
            bg = bc[sl, g * SSD_STATE:(g + 1) * SSD_STATE]
            cg = bc[sl, SSD_BC + g * SSD_STATE:SSD_BC + (g + 1) * SSD_STATE]
            cg32 = cg.astype(F32)
            gmat = _dot_nt(cg, bg)
            yield
            ht = ht_ref[g]
            htb = ht.astype(BF16)
            glanes = slice(g * gw, (g + 1) * gw)
            pairs = []
            for j in range(hpg // 2):
                plo = g * gw + j * LANES
                rhs = jnp.concatenate([xb[:, plo:plo + LANES], htb[:, j * LANES:(j + 1) * LANES]], axis=0)
                lhs = []
                for hh in range(2):
                    lane = SSD_HEADS * direction + g * hpg + 2 * j + hh
                    col = pos_s[:, lane:lane + 1]
                    row = pos_t[lane:lane + 1, :]
                    seg = (row - col) if rev else (col - row)
                    decay = jnp.where(keep, jnp.exp(seg), 0.0) * dt_t[lane:lane + 1, :]
                    lhs.append(jnp.concatenate([(gmat * decay).astype(BF16),
                                                (cg32 * fac_s[:, lane:lane + 1]).astype(BF16)], axis=1))
                    yield
                res = _dot(jnp.concatenate(lhs, axis=0), rhs)
                pairs.append(jnp.where(lane_lo, res[:q], res[q:]))
                yield
            y_groups.append(jnp.concatenate(pairs, axis=1))
            ht_ref[g] = ht * dec_state[:, glanes] + _dot_tn(bg, xw[:, glanes])
            yield
        ys[c] = jnp.concatenate(y_groups, axis=1)
    return jnp.concatenate(ys, axis=0)


def _gla_direction(qh, kh, vh, glog, s_ref, rev):
    t = qh.shape[0]
    q = GLA_CHUNK
    pair = 2 * q
    ri = _iota2((t, t), 0)
    ci = _iota2((t, t), 1)
    order_ok = (ri <= ci) if rev else (ri >= ci)
    prefix = (((ri >> 6) == (ci >> 6)) & order_ok).astype(BF16)
    gc = _dot_split_rhs(prefix, glog, 2)
    yield
    q_in = qh * jnp.exp(gc)
    k_in = kh * jnp.exp(-gc)
    yield
    pr = _iota2((pair, GLA_HEADS * pair), 0)
    pc = _iota2((pair, GLA_HEADS * pair), 1) & (pair - 1)
    causal = ((pr >> 6) == (pc >> 6)) & ((pc >= pr) if rev else (pc <= pr))
    khead = _iota2((GLA_KEY_DIM, pair), 0) >> 5
    vhead_rows = _iota2((GLA_HEADS * pair, GLA_WIDTH), 0) >> 7
    vhead_cols = _iota2((GLA_HEADS * pair, GLA_WIDTH), 1) >> 6
    blockdiag = (_iota2((GLA_KEY_DIM, GLA_WIDTH), 0) >> 5) == (_iota2((GLA_KEY_DIM, GLA_WIDTH), 1) >> 6)
    outs = [None] * (t // q)
    pair_order = range(t // pair - 1, -1, -1) if rev else range(t // pair)
    for pi in pair_order:
        psl = slice(pi * pair, (pi + 1) * pair)
        k_t = k_in[psl].T
        gc_t = gc[psl].T
        yield
        kbd = jnp.concatenate([jnp.where(khead == h, k_t, 0.0) for h in range(GLA_HEADS)], axis=1).astype(BF16)
        att = _dot(q_in[psl].astype(BF16), kbd)
        yield
        att = jnp.where(causal, att, 0.0).astype(BF16)
        vp = vh[psl].astype(BF16)
        vbd = jnp.where(vhead_rows == vhead_cols, jnp.concatenate([vp] * GLA_HEADS, axis=0), 0.0)
        o_intra = _dot(att, vbd)
        yield
        chunk_order = (1, 0) if rev else (0, 1)
        for cc in chunk_order:
            c = 2 * pi + cc
            sl = slice(c * q, (c + 1) * q)
            edge = cc * q if rev else cc * q + q - 1
            egl_col = jnp.exp(gc_t[:, edge:edge + 1])
            st = s_ref[...]
            o_inter = _dot(q_in[sl].astype(BF16), st.astype(BF16))
            outs[c] = o_intra[cc * q:(cc + 1) * q] + o_inter
            kw_t = (k_t[:, cc * q:(cc + 1) * q] * egl_col).astype(BF16)
            upd = _dot(kw_t, vh[sl].astype(BF16))
            s_ref[...] = st * egl_col + jnp.where(blockdiag, upd, 0.0)
            yield
    return jnp.concatenate(outs, axis=0)


def _head_rms(x, w_row):
    n = x.shape[1]
    ones = ((_iota2((n, n), 0) >> 6) == (_iota2((n, n), 1) >> 6)).astype(BF16)
    ms = _dot((x * x).astype(BF16), ones) * (1.0 / HEAD_DIM)
    return x * lax.rsqrt(ms + EPS) * w_row


def _rope(x, cos, sin_signed):
    n = x.shape[1]
    quarter = HEAD_DIM // 4
    first = (_iota2((1, n), 1) & (2 * quarter - 1)) < quarter
    swapped = jnp.where(first, pltpu.roll(x, n - quarter, 1), pltpu.roll(x, quarter, 1))
    return x * cos + swapped * sin_signed


def _attend_block(q_blk, kcat, vt, sink2, edge_masks):
    assert ATT_GROUP == 2 and ATT_KV_HEADS == 2
    blk = q_blk.shape[0]
    lane = _iota2((blk, LANES), 1)
    pieces = []
    for kvh in range(ATT_KV_HEADS):
        qt = q_blk[:, kvh * LANES:(kvh + 1) * LANES]
        rolled = pltpu.roll(qt, HEAD_DIM, 1)
        own = (lane < HEAD_DIM) if kvh == 0 else (lane >= HEAD_DIM)
        for g in range(ATT_GROUP):
            pieces.append(jnp.where(own, qt if g == kvh else rolled, 0.0))
    q_all = jnp.concatenate(pieces, axis=0).astype(BF16)
    yield
    st = _dot_nt(kcat, q_all)
    yield
    if edge_masks is not None:
        row, keep_before, keep_after = edge_masks
        st = jnp.concatenate([st[:row],
                              jnp.where(keep_before, st[row:row + blk], NEG_INF),
                              st[row + blk:row + 2 * blk],
                              jnp.where(keep_after, st[row + 2 * blk:], NEG_INF)], axis=0)
    m = jnp.maximum(jnp.max(st, axis=0, keepdims=True), sink2)
    yield
    p = jnp.exp2(st - m)
    yield
    den = jnp.sum(p, axis=0, keepdims=True) + jnp.exp2(sink2 - m)
    pb = p.astype(BF16)
    yield
    z = []
    for kvh in range(ATT_KV_HEADS):
        qcols = slice(kvh * ATT_GROUP * blk, (kvh + 1) * ATT_GROUP * blk)
        ot = _dot(vt[kvh * HEAD_DIM:(kvh + 1) * HEAD_DIM], pb[:, qcols]) / den[:, qcols]
        for g in range(ATT_GROUP):
            z.append(ot[:, g * blk:(g + 1) * blk])
        yield
    return jnp.concatenate(z, axis=0).T


def _gla_inputs(rest_ref, glog_col):
    qh = rest_ref[:, P_GQ:P_GQ + GLA_KEY_DIM] * (GLA_DK ** -0.5)
    kh = rest_ref[:, P_GK:P_GK + GLA_KEY_DIM]
    vh = rest_ref[:, P_GV:P_GV + GLA_WIDTH]
    return qh, kh, vh, rest_ref[:, glog_col:glog_col + GLA_KEY_DIM]


def _load_states(ht_ref, s_ref, ht0_ref, s0_ref):
    hpg = SSD_HEADS // SSD_GROUPS
    for g in range(SSD_GROUPS):
        blk = ht0_ref[g * hpg:(g + 1) * hpg].reshape(hpg * SSD_HEAD_DIM, SSD_STATE)
        ht_ref[g] = blk.T
    rows = []
    for h in range(GLA_HEADS):
        pieces = []
        if h > 0:
            pieces.append(jnp.zeros((GLA_DK, h * GLA_DV), F32))
        pieces.append(s0_ref[h])
        if h < GLA_HEADS - 1:
            pieces.append(jnp.zeros((GLA_DK, (GLA_HEADS - 1 - h) * GLA_DV), F32))
        rows.append(jnp.concatenate(pieces, axis=1))
    s_ref[...] = jnp.concatenate(rows, axis=0)


def _store_states(ht_ref, s_ref, htf_ref, sf_ref):
    hpg = SSD_HEADS // SSD_GROUPS
    for g in range(SSD_GROUPS):
        htf_ref[g * hpg:(g + 1) * hpg] = ht_ref[g].T.reshape(hpg, SSD_HEAD_DIM, SSD_STATE)
    st = s_ref[...]
    for h in range(GLA_HEADS):
        sf_ref[h] = st[h * GLA_DK:(h + 1) * GLA_DK, h * GLA_DV:(h + 1) * GLA_DV]


HT_SHAPE = (SSD_GROUPS, SSD_STATE, SSD_WIDTH // SSD_GROUPS)
S_SHAPE = (GLA_KEY_DIM, GLA_WIDTH)
SSD_STATE_SHAPE = (SSD_HEADS, SSD_HEAD_DIM, SSD_STATE)
GLA_STATE_SHAPE = (GLA_HEADS, GLA_DK, GLA_DV)


def _mix_bwd_kernel(*refs, is_ctx, n_tiles):
    it = iter(refs)
    xs_ref = next(it)
    bc_ref = next(it)
    rest_ref = next(it)
    cp_ref = next(it)
    if is_ctx:
        next(it)
        next(it)
    else:
        ht0_ref = next(it)
        s0_ref = next(it)
    yb_ref = next(it)
    ob_ref = next(it)
    if is_ctx:
        htf_ref = next(it)
        sf_ref = next(it)
    ht_ref = next(it)
    s_ref = next(it)

    @pl.when(pl.program_id(1) == 0)
    def _():
        for u in range(SEQ_PER_STEP):
            if is_ctx:
                ht_ref[u] = jnp.zeros(HT_SHAPE, F32)
                s_ref[u] = jnp.zeros(S_SHAPE, F32)
            else:
                _load_states(ht_ref.at[u], s_ref.at[u], ht0_ref.at[u], s0_ref.at[u])

    a_small_row = -jnp.exp(cp_ref[CP_ALOG:CP_ALOG + 1, :LANES])
    tasks = []
    for u in range(SEQ_PER_STEP):
        rest_u = rest_ref.at[u]
        tasks.append(_ssd_direction(xs_ref[u], bc_ref[u], rest_u[:, P_DT:P_DT + LANES], a_small_row, ht_ref.at[u],
                                    True, 1))
        tasks.append(_gla_direction(*_gla_inputs(rest_u, P_GLR), s_ref.at[u], True))
    outs = _run_interleaved(tasks)
    for u in range(SEQ_PER_STEP):
        yb_ref[u] = outs[2 * u]
        ob_ref[u] = outs[2 * u + 1]
        if is_ctx:
            _store_states(ht_ref.at[u], s_ref.at[u], htf_ref.at[u], sf_ref.at[u])


def _state_specs(l, direction):
    return [pl.BlockSpec((SEQ_PER_STEP, None, None) + SSD_STATE_SHAPE, lambda s, t: (s, l, direction, 0, 0, 0)),
            pl.BlockSpec((SEQ_PER_STEP, None, None) + GLA_STATE_SHAPE, lambda s, t: (s, l, direction, 0, 0, 0))]


def _per_seq(a, n_seq):
    return a.reshape(n_seq, a.shape[0] // n_seq, a.shape[1])


def _mix_bwd(xs, bc, rest, cpack, l, states, *, n_seq, n_tiles, is_ctx):
    u = SEQ_PER_STEP
    seq_len = n_tiles * TILE
    row = lambda s, t: (s, n_tiles - 1 - t, 0)
    in_specs = [
        pl.BlockSpec((u, TILE, SSD_WIDTH), row),
        pl.BlockSpec((u, TILE, 2 * SSD_BC), row),
        pl.BlockSpec((u, TILE, BWD_COLS), row),
        pl.BlockSpec((None, CP_ROWS, D_MODEL), lambda s, t: (l, 0, 0)),
    ]
    args = [_per_seq(xs, n_seq), _per_seq(bc, n_seq), _per_seq(rest, n_seq), cpack]
    out_specs = [pl.BlockSpec((u, TILE, SSD_WIDTH), row), pl.BlockSpec((u, TILE, GLA_WIDTH), row)]
    out_shape = [jax.ShapeDtypeStruct((n_seq, seq_len, SSD_WIDTH), F32),
                 jax.ShapeDtypeStruct((n_seq, seq_len, GLA_WIDTH), F32)]
    aliases = {}
    if is_ctx:
        in_specs += [pl.BlockSpec(memory_space=pl.ANY), pl.BlockSpec(memory_space=pl.ANY)]
        out_specs += _state_specs(l, 1)
        out_shape += [jax.ShapeDtypeStruct(a.shape, a.dtype) for a in states]
        aliases = {len(args): len(out_shape) - 2, len(args) + 1: len(out_shape) - 1}
    else:
        in_specs += _state_specs(l, 1)
    args += list(states)
    return pl.pallas_call(
        functools.partial(_mix_bwd_kernel, is_ctx=is_ctx, n_tiles=n_tiles),
        grid=(n_seq // u, n_tiles),
        in_specs=in_specs,
        out_specs=out_specs,
        out_shape=out_shape,
        input_output_aliases=aliases,
        scratch_shapes=[pltpu.VMEM((u,) + HT_SHAPE, F32), pltpu.VMEM((u,) + S_SHAPE, F32)],
        compiler_params=pltpu.CompilerParams(dimension_semantics=("arbitrary", "arbitrary"),
                                             vmem_limit_bytes=VMEM_LIMIT_BYTES),
        name="mix_bwd_ctx" if is_ctx else "mix_bwd_lat",
    )(*args)


def _mix_fwd_kernel(*refs, is_ctx, n_tiles, layer):
    it = iter(refs)
    sink_ref = next(it)
    x_ref = next(it)
    gate_ref = next(it)
    rest_ref = next(it)
    xs_ref = next(it)
    bc_ref = next(it)
    yb_ref = next(it)
    ob_ref = next(it)
    cp_ref = next(it)
    wout_ref = next(it)
    if is_ctx:
        for _ in range(4):
            next(it)
    else:
        ht0_ref = next(it)
        s0_ref = next(it)
        kvprev_ref = next(it)
        kvnext_ref = next(it)
        cs_ref = next(it)
        csp_ref = next(it)
        csn_ref = next(it)
        kc_ref = next(it)
        vc_ref = next(it)
    o_ref = next(it)
    if is_ctx:
        kout_ref = next(it)
        vout_ref = next(it)
        htf_ref = next(it)
        sf_ref = next(it)
    ht_ref = next(it)
    s_ref = next(it)

    tile_idx = pl.program_id(1)

    @pl.when(tile_idx == 0)
    def _():
        for u in range(SEQ_PER_STEP):
            if is_ctx:
                ht_ref[u] = jnp.zeros(HT_SHAPE, F32)
                s_ref[u] = jnp.zeros(S_SHAPE, F32)
            else:
                _load_states(ht_ref.at[u], s_ref.at[u], ht0_ref.at[u], s0_ref.at[u])

    tasks = []
    for u in range(SEQ_PER_STEP):
        rest_u = rest_ref.at[u]
        tasks.append(_fwd_ssd(xs_ref[u], bc_ref[u], yb_ref[u], rest_u, cp_ref, ht_ref.at[u]))
        tasks.append(_fwd_gla(ob_ref[u], rest_u, cp_ref, s_ref.at[u]))
        tasks.append(_fwd_attention(
            u, tile_idx, sink_ref, rest_u, cp_ref,
            (kout_ref, vout_ref) if is_ctx else
            (kvprev_ref.at[u], kvnext_ref.at[u], cs_ref, csp_ref, csn_ref, kc_ref, vc_ref),
            is_ctx=is_ctx, n_tiles=n_tiles, layer=layer))
    parts = _run_interleaved(tasks)
    mixed = [jnp.concatenate(parts[3 * u:3 * u + 3], axis=1).astype(BF16) for u in range(SEQ_PER_STEP)]
    y = _dot(jnp.concatenate(mixed, axis=0), wout_ref[...])
    for u in range(SEQ_PER_STEP):
        gate = gate_ref[...] if is_ctx else gate_ref[u]
        o_ref[u] = x_ref[u] + gate * y[u * TILE:(u + 1) * TILE]
        if is_ctx:
            _store_states(ht_ref.at[u], s_ref.at[u], htf_ref.at[u], sf_ref.at[u])


def _fwd_ssd(xs, bc, yb, rest_ref, cp_ref, ht_ref):
    a_small_row = -jnp.exp(cp_ref[CP_ALOG:CP_ALOG + 1, :LANES])
    y_f = yield from _ssd_direction(xs, bc, rest_ref[:, P_DT:P_DT + LANES], a_small_row, ht_ref, False, 0)
    y = y_f + yb + xs * cp_ref[CP_DSKIP:CP_DSKIP + 1, :SSD_WIDTH]
    y = y * _silu(rest_ref[:, P_Z:P_Z + SSD_WIDTH])
    yield
    return y * lax.rsqrt(jnp.mean(y * y, axis=-1, keepdims=True) + EPS) * cp_ref[CP_SSDNW:CP_SSDNW + 1, :SSD_WIDTH]


def _fwd_gla(ob, rest_ref, cp_ref, s_ref):
    o_f = yield from _gla_direction(*_gla_inputs(rest_ref, P_GLF), s_ref, False)
    o = o_f + ob
    yield
    return _head_rms(o, cp_ref[CP_GLANW:CP_GLANW + 1, :GLA_WIDTH]) * _silu(rest_ref[:, P_GG:P_GG + GLA_WIDTH])


def _fwd_attention(u, tile_idx, sink_ref, rest_ref, cp_ref, extra, *, is_ctx, n_tiles, layer):
    if is_ctx:
        kout_ref, vout_ref = extra
    else:
        kvprev_ref, kvnext_ref, cs_ref, csp_ref, csn_ref, kc_ref, vc_ref = extra
    t = TILE
    knw = cp_ref[CP_KNW:CP_KNW + 1, :ATT_KV_DIM]
    qn = _head_rms(rest_ref[:, P_AQ:P_AQ + ATT_WIDTH], cp_ref[CP_QNW:CP_QNW + 1, :ATT_WIDTH])
    yield
    kn = _head_rms(rest_ref[:, P_AK:P_AK + ATT_KV_DIM], knw)
    vv = rest_ref[:, P_AV:P_AV + ATT_KV_DIM]
    yield
    qscale = (HEAD_DIM ** -0.5) * LOG2E
    blk = ATT_BLOCK
    n_blk = t // blk
    sink2 = jnp.concatenate([jnp.full((1, blk), sink_ref[layer, h] * LOG2E, F32) for h in range(ATT_HEADS)], axis=1)
    blocks = []
    if is_ctx:
        kout_ref[u] = kn
        vout_ref[u] = vv
        qs = qn * qscale
        kb = kn.astype(BF16)
        v_t = vv.T.astype(BF16)
        yield
        for nb in range(n_blk):
            blocks.append((yield from _attend_block(qs[nb * blk:(nb + 1) * blk], kb, v_t, sink2, None)))
    else:
        cos = cs_ref[:, :ATT_KV_DIM]
        sin = cs_ref[:, ATT_KV_DIM:]
        qr = _rope(qn, jnp.concatenate([cos, cos], axis=1), jnp.concatenate([sin, sin], axis=1)) * qscale
        yield
        k_halo_p = _rope(_head_rms(kvprev_ref[:, :ATT_KV_DIM], knw), csp_ref[:, :ATT_KV_DIM], csp_ref[:, ATT_KV_DIM:])
        k_halo_n = _rope(_head_rms(kvnext_ref[:, :ATT_KV_DIM], knw), csn_ref[:, :ATT_KV_DIM], csn_ref[:, ATT_KV_DIM:])
        yield
        k_loc = jnp.concatenate([k_halo_p, _rope(kn, cos, sin), k_halo_n], axis=0).astype(BF16)
        kc = kc_ref[u].astype(BF16)
        n_ctx = kc.shape[0]
        yield
        v_all_t = jnp.concatenate([vc_ref[u], kvprev_ref[:, ATT_KV_DIM:], vv, kvnext_ref[:, ATT_KV_DIM:]],
                                  axis=0).T.astype(BF16)
        yield
        kj = _iota2((blk, ATT_HEADS * blk), 0)
        qi = _iota2((blk, ATT_HEADS * blk), 1) & (blk - 1)
        for nb in range(n_blk):
            kcat = jnp.concatenate([kc, k_loc[nb * blk:(nb + 3) * blk]], axis=0)
            vt = jnp.concatenate([v_all_t[:, :n_ctx], v_all_t[:, n_ctx + nb * blk:n_ctx + (nb + 3) * blk]], axis=1)
            off_p = jnp.where(tile_idx > 0, 0, 2 * blk) if nb == 0 else 0
            off_n = jnp.where(tile_idx < n_tiles - 1, 0, 2 * blk) if nb == n_blk - 1 else 0
            masks = (n_ctx, kj >= qi + off_p, kj <= qi - off_n)
            blocks.append((yield from _attend_block(qr[nb * blk:(nb + 1) * blk], kcat, vt, sink2, masks)))
    return jnp.concatenate(blocks, axis=0) * _silu(rest_ref[:, P_AG:P_AG + ATT_WIDTH])


def _mix_fwd(x, mod5, gate_row, xs, bc, rest, yb, ob, cpack, sink, w_out, l, extra, *, n_seq, n_tiles, is_ctx):
    u = SEQ_PER_STEP
    seq_len = n_tiles * TILE
    row = lambda s, t: (s, t, 0)
    rest3 = _per_seq(rest, n_seq)
    if is_ctx:
        gate_spec = _mod_spec(l, 2, lambda s, t: gate_row(s))
    else:
        gate_spec = pl.BlockSpec((None, u, None, 1, D_MODEL), lambda s, t: (l, s, 2, 0, 0))
    in_specs = [
        pl.BlockSpec(memory_space=pltpu.SMEM),
        pl.BlockSpec((u, TILE, D_MODEL), row),
        gate_spec,
        pl.BlockSpec((u, TILE, D_REST), row),
        pl.BlockSpec((u, TILE, SSD_WIDTH), row),
        pl.BlockSpec((u, TILE, 2 * SSD_BC), row),
        pl.BlockSpec((u, TILE, SSD_WIDTH), row),
        pl.BlockSpec((u, TILE, GLA_WIDTH), row),
        pl.BlockSpec((None, CP_ROWS, D_MODEL), lambda s, t: (l, 0, 0)),
        pl.BlockSpec((None, D_MIX, D_MODEL), lambda s, t: (l, 0, 0)),
    ]
    args = [sink, _per_seq(x, n_seq), mod5, rest3, _per_seq(xs, n_seq), _per_seq(bc, n_seq), yb, ob, cpack, w_out]
    out_specs = [pl.BlockSpec((u, TILE, D_MODEL), row)]
    out_shape = [jax.ShapeDtypeStruct((n_seq, seq_len, D_MODEL), F32)]
    aliases = {}
    if is_ctx:
        kv_spec = pl.BlockSpec((u, None, TILE, ATT_KV_DIM), lambda s, t: (s, l, t, 0))
        in_specs += [pl.BlockSpec(memory_space=pl.ANY)] * 4
        out_specs += [kv_spec, kv_spec] + _state_specs(l, 0)
        out_shape += [jax.ShapeDtypeStruct(a.shape, a.dtype) for a in extra]
        aliases = {len(args) + i: 1 + i for i in range(4)}
        args += list(extra)
    else:
        state_ssd, state_gla, rope_cs, cache_k, cache_v = extra
        per_blk = TILE // ATT_BLOCK
        kv_col = P_AK // (2 * ATT_KV_DIM)
        last_blk = n_tiles * per_blk - 1
        prev_blk = lambda t: jnp.maximum(t * per_blk - 1, 0)
        next_blk = lambda t: jnp.minimum((t + 1) * per_blk, last_blk)
        n_ctx = cache_k.shape[2]
        ctx_spec = pl.BlockSpec((u, None, n_ctx, ATT_KV_DIM), lambda s, t: (s, l, 0, 0))
        in_specs += _state_specs(l, 0) + [
            pl.BlockSpec((u, ATT_BLOCK, 2 * ATT_KV_DIM), lambda s, t: (s, prev_blk(t), kv_col)),
            pl.BlockSpec((u, ATT_BLOCK, 2 * ATT_KV_DIM), lambda s, t: (s, next_blk(t), kv_col)),
            pl.BlockSpec((TILE, 2 * ATT_KV_DIM), lambda s, t: (t, 0)),
            pl.BlockSpec((ATT_BLOCK, 2 * ATT_KV_DIM), lambda s, t: (prev_blk(t), 0)),
            pl.BlockSpec((ATT_BLOCK, 2 * ATT_KV_DIM), lambda s, t: (next_blk(t), 0)),
            ctx_spec,
            ctx_spec,
        ]
        args += [state_ssd, state_gla, rest3, rest3, rope_cs, rope_cs, rope_cs, cache_k, cache_v]
    outs = pl.pallas_call(
        functools.partial(_mix_fwd_kernel, is_ctx=is_ctx, n_tiles=n_tiles, layer=l),
        grid=(n_seq // u, n_tiles),
        in_specs=in_specs,
        out_specs=out_specs,
        out_shape=out_shape,
        input_output_aliases=aliases,
        scratch_shapes=[pltpu.VMEM((u,) + HT_SHAPE, F32), pltpu.VMEM((u,) + S_SHAPE, F32)],
        compiler_params=pltpu.CompilerParams(dimension_semantics=("arbitrary", "arbitrary"),
                                             vmem_limit_bytes=VMEM_LIMIT_BYTES),
        name="mix_fwd_ctx" if is_ctx else "mix_fwd_lat",
    )(*args)
    return [outs[0].reshape(n_seq * seq_len, D_MODEL)] + list(outs[1:])


def _pad_cols(a, width):
    return jnp.pad(a, [(0, 0)] * (a.ndim - 1) + [(0, width - a.shape[-1])])


def _prep_w_in(w_in):
    sizes = (SSD_WIDTH, SSD_CONV_DIM, 2 * SSD_HEADS, GLA_KEY_DIM, GLA_KEY_DIM, GLA_WIDTH, GLA_WIDTH, 2 * GLA_LOWRANK,
             ATT_WIDTH, ATT_KV_DIM, ATT_KV_DIM, ATT_WIDTH)
    parts, start = [], 0
    for size in sizes:
        parts.append(w_in[..., start:start + size])
        start += size
    z, xbc, dt, gq, gk, gv, gg, glr, aq, ak, av, ag = parts
    cols = [xbc, _pad_cols(dt, LANES), gq, gk, _pad_cols(glr, LANES), gv, ak, av, z, gg, aq, ag]
    return jnp.concatenate(cols, axis=-1).astype(BF16)


def _pack_small_params(norm_w, conv_w, conv_b, ssd_a_log, ssd_dt_bias, ssd_d, ssd_norm_w, gla_gk_b, gla_norm_w,
                       attn_q_norm, attn_k_norm):
    depth = norm_w.shape[0]
    row = lambda a: _pad_cols(a.reshape(depth, 1, -1), D_MODEL)
    blank = lambda n: jnp.zeros((depth, n, D_MODEL), F32)
    rows = [
        _pad_cols(conv_w, D_MODEL), blank(CP_CONVB - SSD_CONV),
        row(conv_b), row(ssd_dt_bias), row(ssd_a_log), row(gla_gk_b[:, 0]), row(gla_gk_b[:, 1]),
        row(jnp.repeat(ssd_d, SSD_HEAD_DIM, axis=-1)), row(ssd_norm_w), row(jnp.tile(gla_norm_w, (1, GLA_HEADS))),
        row(jnp.tile(attn_q_norm, (1, ATT_HEADS))), row(jnp.tile(attn_k_norm, (1, ATT_KV_HEADS))), row(norm_w),
        blank(CP_ROWS - CP_NW - 1),
    ]
    return jnp.concatenate(rows, axis=1)


def _pad_gate_weights(gla_gk_up):
    depth = gla_gk_up.shape[0]
    w = jnp.zeros((depth, LANES, 2 * GLA_KEY_DIM), F32)
    for d in range(2):
        w = w.at[:, d * GLA_LOWRANK:(d + 1) * GLA_LOWRANK, d * GLA_KEY_DIM:(d + 1) * GLA_KEY_DIM].set(gla_gk_up[:, d])
    return w


def _rope_tables(seq_len):
    quarter = HEAD_DIM // 4
    rows = seq_len // GRID_W
    row_pos = jnp.repeat(jnp.arange(rows, dtype=F32), GRID_W)
    col_pos = jnp.tile(jnp.arange(GRID_W, dtype=F32), rows)
    inv = ROPE_THETA ** (-jnp.arange(quarter, dtype=F32) / quarter)
    ang_r = row_pos[:, None] * inv[None, :]
    ang_c = col_pos[:, None] * inv[None, :]
    cos = jnp.concatenate([jnp.cos(ang_r), jnp.cos(ang_r), jnp.cos(ang_c), jnp.cos(ang_c)], axis=1)
    sin = jnp.concatenate([-jnp.sin(ang_r), jnp.sin(ang_r), -jnp.sin(ang_c), jnp.sin(ang_c)], axis=1)
    return jnp.tile(cos, (1, ATT_KV_HEADS)), jnp.tile(sin, (1, ATT_KV_HEADS))


def kernel(x_prompt, x_sample, c, cache_k, cache_v, state_ssd, state_gla, c_ctx, w_ada, b_ada, norm_w, w_in, conv_w,
           conv_b, ssd_a_log, ssd_dt_bias, ssd_d, ssd_norm_w, gla_gk_up, gla_gk_b, gla_norm_w, attn_q_norm, attn_k_norm,
           attn_sink, w_out):
    batch, seq, _ = x_prompt.shape
    dec_batch, dec_seq, _ = x_sample.shape
    depth = w_in.shape[0]
    past = cache_k.shape[2]
    assert seq % TILE == 0 and dec_seq % TILE == 0
    assert (batch * seq) % IN_TILE == 0 and dec_seq % IN_TILE == 0

    rows = -(-(dec_batch + 1) // SUBLANES) * SUBLANES
    cond = jnp.zeros((rows, D_MODEL), F32).at[:dec_batch].set(c).at[dec_batch].set(c_ctx)
    mod = _modulation(cond, w_ada, b_ada)
    mod5 = mod.reshape(depth, rows, 3, 1, D_MODEL)

    w_in_p = _prep_w_in(w_in)
    w_out_b = w_out.astype(BF16)
    cpack = _pack_small_params(norm_w, conv_w, conv_b, ssd_a_log, ssd_dt_bias, ssd_d, ssd_norm_w, gla_gk_b,
                               gla_norm_w, attn_q_norm, attn_k_norm)
    gk_w = _pad_gate_weights(gla_gk_up)
    rope_cs = jnp.concatenate(_rope_tables(dec_seq), axis=1)
    cache_k2 = cache_k.reshape(dec_batch, depth, past, ATT_KV_DIM)
    cache_v2 = cache_v.reshape(dec_batch, depth, past, ATT_KV_DIM)

    x_ctx = x_prompt.reshape(batch * seq, D_MODEL)
    x_lat = x_sample.reshape(dec_batch * dec_seq, D_MODEL)
    ctx_tiles = seq // TILE
    lat_tiles = dec_seq // TILE
    lat_in_per_seq = dec_seq // IN_TILE

    new_k = jnp.zeros((batch, depth, seq, ATT_KV_DIM), F32)
    new_v = jnp.zeros((batch, depth, seq, ATT_KV_DIM), F32)
    new_ssd = jnp.zeros((batch, depth, 2) + SSD_STATE_SHAPE, F32)
    new_gla = jnp.zeros((batch, depth, 2) + GLA_STATE_SHAPE, F32)

    for l in range(depth):
        xs, bc, rest = _in_proj(x_ctx, mod5, cpack, w_in_p, gk_w, l, lambda i: dec_batch, seq)
        yb, ob, new_ssd, new_gla = _mix_bwd(xs, bc, rest, cpack, l, (new_ssd, new_gla), n_seq=batch,
                                            n_tiles=ctx_tiles, is_ctx=True)
        x_ctx, new_k, new_v, new_ssd, new_gla = _mix_fwd(
            x_ctx, mod5, lambda s: dec_batch, xs, bc, rest, yb, ob, cpack, attn_sink, w_out_b, l,
            (new_k, new_v, new_ssd, new_gla), n_seq=batch, n_tiles=ctx_tiles, is_ctx=True)

        xs, bc, rest = _in_proj(x_lat, mod5, cpack, w_in_p, gk_w, l, lambda i: i // lat_in_per_seq, dec_seq)
        yb, ob = _mix_bwd(xs, bc, rest, cpack, l, (state_ssd, state_gla), n_seq=dec_batch, n_tiles=lat_tiles,
                          is_ctx=False)
        (x_lat,) = _mix_fwd(x_lat, mod5, lambda s: s, xs, bc, rest, yb, ob, cpack, attn_sink, w_out_b, l,
                            (state_ssd, state_gla, rope_cs, cache_k2, cache_v2), n_seq=dec_batch, n_tiles=lat_tiles,
                            is_ctx=False)

    return (x_ctx.reshape(batch, seq, D_MODEL), x_lat.reshape(dec_batch, dec_seq, D_MODEL),
            new_k.reshape(batch, depth, seq, ATT_KV_HEADS, HEAD_DIM),
            new_v.reshape(batch, depth, seq, ATT_KV_HEADS, HEAD_DIM), new_ssd, new_gla)
```

```python
import functools
import math

import jax
import jax.numpy as jnp
from jax import lax
from jax.experimental import pallas as pl
from jax.experimental.pallas import tpu as pltpu

D_MODEL = 1024
GRID_W = 64
SSD_HEADS = 8
SSD_HEAD_DIM = 64
SSD_WIDTH = 512
SSD_GROUPS = 2
SSD_STATE = 128
SSD_BC = SSD_GROUPS * SSD_STATE
SSD_CONV = 5
SSD_CONV_DIM = SSD_WIDTH + 2 * SSD_BC
SSD_CHUNK = 128
GLA_HEADS = 4
GLA_DK = 32
GLA_DV = 64
GLA_KEY_DIM = GLA_HEADS * GLA_DK
GLA_WIDTH = GLA_HEADS * GLA_DV
GLA_LOWRANK = 16
GLA_GATE_NORM = 16.0
GLA_CHUNK = 64
ATT_HEADS = 4
ATT_KV_HEADS = 2
ATT_GROUP = ATT_HEADS // ATT_KV_HEADS
HEAD_DIM = 64
ATT_WIDTH = ATT_HEADS * HEAD_DIM
ATT_KV_DIM = ATT_KV_HEADS * HEAD_DIM
WINDOW = 128
ATT_BLOCK = 128
ROPE_THETA = 10000.0
D_MIX = SSD_WIDTH + GLA_WIDTH + ATT_WIDTH
EPS = 1e-6
NEG_INF = -1e30
LOG2E = math.log2(math.e)

LANES = 128
SUBLANES = 8
VMEM_LIMIT_BYTES = 56 * 1024 * 1024

P_DT = 0
P_GQ = P_DT + LANES
P_GK = P_GQ + GLA_KEY_DIM
P_GLR = P_GK + GLA_KEY_DIM
P_GV = P_GLR + LANES
P_AK = P_GV + GLA_WIDTH
P_AV = P_AK + ATT_KV_DIM
P_Z = P_AV + ATT_KV_DIM
P_GG = P_Z + SSD_WIDTH
P_AQ = P_GG + GLA_WIDTH
P_AG = P_AQ + ATT_WIDTH
D_LIN = P_AG + ATT_WIDTH
D_PROJ = SSD_CONV_DIM + D_LIN
P_GLF = D_LIN
D_REST = D_LIN + GLA_KEY_DIM
BWD_COLS = P_AK

CP_CONVW = 0
CP_CONVB = 8
CP_DTB = 9
CP_ALOG = 10
CP_GKB = 11
CP_DSKIP = 13
CP_SSDNW = 14
CP_GLANW = 15
CP_QNW = 16
CP_KNW = 17
CP_NW = 18
CP_ROWS = 24

TILE = 256
IN_TILE = 512
HALO = SUBLANES
CONV_COLS = 256
CONV_ROWS = 64
SEQ_PER_STEP = 2

HI = lax.Precision.HIGHEST
F32 = jnp.float32
BF16 = jnp.bfloat16

NT_DIMS = (((1,), (1,)), ((), ()))
TN_DIMS = (((0,), (0,)), ((), ()))


def _dot(a, b):
    return jnp.dot(a, b, preferred_element_type=F32)


def _dot_nt(a, b):
    return lax.dot_general(a, b, NT_DIMS, preferred_element_type=F32)


def _dot_tn(a, b):
    return lax.dot_general(a, b, TN_DIMS, preferred_element_type=F32)


def _split(x, parts):
    out = []
    for _ in range(parts - 1):
        piece = x.astype(BF16)
        out.append(piece)
        x = x - piece.astype(F32)
    out.append(x.astype(BF16))
    return out


def _dot_split_lhs(x, w_bf16, parts):
    acc = None
    for piece in _split(x, parts):
        term = _dot(piece, w_bf16)
        acc = term if acc is None else acc + term
    return acc


def _dot_split_rhs(w_bf16, x, parts):
    acc = None
    for piece in _split(x, parts):
        term = _dot(w_bf16, piece)
        acc = term if acc is None else acc + term
    return acc


def _silu(x):
    half = 0.5 * x
    return half + half * jnp.tanh(half)


def _softplus(x):
    return jnp.maximum(x, 0.0) + jnp.log(1.0 + jnp.exp(-jnp.abs(x)))


def _log_sigmoid(x):
    return jnp.minimum(x, 0.0) - jnp.log(1.0 + jnp.exp(-jnp.abs(x)))


def _iota2(shape, dim):
    return lax.broadcasted_iota(jnp.int32, shape, dim)


def _log2(n):
    assert n > 0 and n & (n - 1) == 0
    return n.bit_length() - 1


def _run_interleaved(tasks):
    results = [None] * len(tasks)
    live = list(range(len(tasks)))
    while live:
        for i in list(live):
            try:
                next(tasks[i])
            except StopIteration as done:
                results[i] = done.value
                live.remove(i)
    return results


def _mod_kernel(cond_ref, w_ref, b_ref, o_ref):
    o_ref[...] = jnp.dot(_silu(cond_ref[...]), w_ref[...], preferred_element_type=F32, precision=HI) + b_ref[...]


def _modulation(cond, w_ada, b_ada):
    depth = w_ada.shape[0]
    rows = cond.shape[0]
    return pl.pallas_call(
        _mod_kernel,
        grid=(depth, 3),
        in_specs=[
            pl.BlockSpec((rows, D_MODEL), lambda l, j: (0, 0)),
            pl.BlockSpec((None, D_MODEL, D_MODEL), lambda l, j: (l, 0, j)),
            pl.BlockSpec((None, 1, D_MODEL), lambda l, j: (l, 0, j)),
        ],
        out_specs=pl.BlockSpec((None, rows, D_MODEL), lambda l, j: (l, 0, j)),
        out_shape=jax.ShapeDtypeStruct((depth, rows, 3 * D_MODEL), F32),
        compiler_params=pltpu.CompilerParams(dimension_semantics=("arbitrary", "arbitrary")),
        name="modulation",
    )(cond, w_ada, b_ada.reshape(depth, 1, 3 * D_MODEL))


def _in_proj_kernel(x_ref, xprev_ref, xnext_ref, shift_ref, scale_ref, cp_ref, w_ref, gkw_ref, xs_ref, bc_ref,
                    rest_ref, *, seq_len):
    gain = cp_ref[CP_NW:CP_NW + 1, :] * (1.0 + scale_ref[...])
    shift = shift_ref[...]

    def modnorm(x):
        xn = x * lax.rsqrt(jnp.mean(x * x, axis=-1, keepdims=True) + EPS)
        return (xn * gain + shift).astype(BF16)

    h = modnorm(x_ref[...])
    t = h.shape[0]
    h_up = jnp.concatenate([modnorm(xprev_ref[...]), h, modnorm(xnext_ref[...])], axis=0)
    seg = min(seq_len, t)
    pos0 = (pl.program_id(0) * t) & (seq_len - 1)
    zero_halo = jnp.zeros((HALO, CONV_COLS), F32)

    for c0 in range(0, SSD_CONV_DIM, CONV_COLS):
        up = _dot(h_up, w_ref[:, c0:c0 + CONV_COLS])
        conv_w = cp_ref[CP_CONVW:CP_CONVW + SUBLANES, c0:c0 + CONV_COLS]
        conv_b = cp_ref[CP_CONVB:CP_CONVB + 1, c0:c0 + CONV_COLS]
        for a in range(0, t, CONV_ROWS):
            if a % seq_len == 0 and seq_len <= t:
                prev = zero_halo
            elif a % seg == 0:
                prev = jnp.where(pos0 + a > 0, up[a:a + HALO, :], 0.0)
            else:
                prev = up[a:a + HALO, :]
            b = a + CONV_ROWS
            if b % seq_len == 0 and seq_len <= t:
                nxt = zero_halo
            elif b % seg == 0:
                nxt = jnp.where(pos0 + b < seq_len, up[HALO + b:2 * HALO + b, :], 0.0)
            else:
                nxt = up[HALO + b:2 * HALO + b, :]
            xbc = _conv_silu(up[HALO + a:HALO + b, :], prev, nxt, conv_w, conv_b)
            if c0 < SSD_WIDTH:
                xs_ref[a:b, c0:c0 + CONV_COLS] = xbc
            else:
                bc_ref[a:b, c0 - SSD_WIDTH:c0 - SSD_WIDTH + CONV_COLS] = xbc.astype(BF16)

    res = _dot(h, w_ref[:, SSD_CONV_DIM:])
    rest_ref[:, P_DT:P_DT + LANES] = _softplus(res[:, P_DT:P_DT + LANES] + cp_ref[CP_DTB:CP_DTB + 1, :LANES])
    rest_ref[:, P_GQ:P_GLR] = res[:, P_GQ:P_GLR]
    rest_ref[:, P_GV:D_LIN] = res[:, P_GV:]
    lr = res[:, P_GLR:P_GLR + LANES].astype(BF16)
    for d, col in ((0, P_GLF), (1, P_GLR)):
        gate = (_dot(lr, gkw_ref[:, d * GLA_KEY_DIM:(d + 1) * GLA_KEY_DIM].astype(BF16))
                + cp_ref[CP_GKB + d:CP_GKB + d + 1, :GLA_KEY_DIM])
        rest_ref[:, col:col + GLA_KEY_DIM] = _log_sigmoid(gate) * (1.0 / GLA_GATE_NORM)


def _mod_spec(l, part, row_of):
    return pl.BlockSpec((None, None, None, 1, D_MODEL), lambda *ids: (l, row_of(*ids), part, 0, 0))


def _in_proj(x, mod5, cpack, w_in_p, gk_w, l, mod_row, seq_len):
    n = x.shape[0]
    per = IN_TILE // HALO
    last_halo = n // HALO - 1
    assert seq_len & (seq_len - 1) == 0
    return pl.pallas_call(
        functools.partial(_in_proj_kernel, seq_len=seq_len),
        grid=(n // IN_TILE,),
        in_specs=[
            pl.BlockSpec((IN_TILE, D_MODEL), lambda i: (i, 0)),
            pl.BlockSpec((HALO, D_MODEL), lambda i: (jnp.maximum(i * per - 1, 0), 0)),
            pl.BlockSpec((HALO, D_MODEL), lambda i: (jnp.minimum((i + 1) * per, last_halo), 0)),
            _mod_spec(l, 0, mod_row),
            _mod_spec(l, 1, mod_row),
            pl.BlockSpec((None, CP_ROWS, D_MODEL), lambda i: (l, 0, 0)),
            pl.BlockSpec((None, D_MODEL, D_PROJ), lambda i: (l, 0, 0)),
            pl.BlockSpec((None, LANES, 2 * GLA_KEY_DIM), lambda i: (l, 0, 0)),
        ],
        out_specs=[pl.BlockSpec((IN_TILE, SSD_WIDTH), lambda i: (i, 0)),
                   pl.BlockSpec((IN_TILE, 2 * SSD_BC), lambda i: (i, 0)),
                   pl.BlockSpec((IN_TILE, D_REST), lambda i: (i, 0))],
        out_shape=[jax.ShapeDtypeStruct((n, SSD_WIDTH), F32), jax.ShapeDtypeStruct((n, 2 * SSD_BC), BF16),
                   jax.ShapeDtypeStruct((n, D_REST), F32)],
        compiler_params=pltpu.CompilerParams(dimension_semantics=("arbitrary",),
                                             vmem_limit_bytes=VMEM_LIMIT_BYTES),
        name="in_proj",
    )(x, x, x, mod5, mod5, cpack, w_in_p, gk_w)


def _conv_silu(main, prev, nxt, conv_w, conv_b):
    t = main.shape[0]
    up = jnp.concatenate([prev, main, nxt], axis=0)
    rows = t + 2 * HALO
    pad = (SSD_CONV - 1) // 2
    acc = conv_b
    for k in range(SSD_CONV):
        off = k - pad
        shifted = up if off == 0 else pltpu.roll(up, (rows - off) % rows, 0)
        acc = acc + conv_w[k:k + 1, :] * shifted[HALO:HALO + t, :]
    return _silu(acc)


def _ssd_direction(xs, bc, dt_small, a_small_row, ht_ref, rev, direction):
    t = xs.shape[0]
    q = SSD_CHUNK
    hpg = SSD_HEADS // SSD_GROUPS
    gw = hpg * SSD_HEAD_DIM
    er = _iota2((LANES, SSD_WIDTH), 0)
    ec = _iota2((LANES, SSD_WIDTH), 1)
    expand = (er == (ec >> _log2(SSD_HEAD_DIM)) + SSD_HEADS * direction).astype(BF16)
    a_small = dt_small * a_small_row
    ri = _iota2((q, q), 0)
    ci = _iota2((q, q), 1)
    tri_incl = (ri >= ci).astype(BF16)
    keep = (ri <= ci) if rev else (ri >= ci)
    lane_lo = _iota2((q, LANES), 1) < SSD_HEAD_DIM
    ys = [None] * (t // q)
    order = range(t // q - 1, -1, -1) if rev else range(t // q)
    for c in order:
        sl = slice(c * q, (c + 1) * q)
        a_s = a_small[sl]
        dts = dt_small[sl]
        cs_s = _dot_split_rhs(tri_incl, a_s, 2)
        yield
        tot_s = cs_s[q - 1:q, :]
        pos_s = (cs_s - a_s) if rev else cs_s
        if rev:
            fac_s = jnp.exp(tot_s - pos_s)
            wgt_s = jnp.exp(pos_s)
        else:
            fac_s = jnp.exp(pos_s)
            wgt_s = jnp.exp(tot_s - pos_s)
        yield
        dtw_exp = _dot((dts * wgt_s).astype(BF16), expand)
        dec_state = _dot_split_lhs(jnp.broadcast_to(jnp.exp(tot_s), (SUBLANES, LANES)), expand, 3)[0:1, :]
        yield
        xc = xs[sl]
        xb = xc.astype(BF16)
        xw = (xc * dtw_exp).astype(BF16)
        pos_t = pos_s.T
        dt_t = dts.T
        yield
        y_groups = []
        for g in range(SSD_GROUPS):
            bg = bc[sl, g * SSD_STATE:(g + 1) * SSD_STATE]
            cg = bc[sl, SSD_BC + g * SSD_STATE:SSD_BC + (g + 1) * SSD_STATE]
            cg32 = cg.astype(F32)
            gmat = _dot_nt(cg, bg)
            yield
            ht = ht_ref[g]
            htb = ht.astype(BF16)
            glanes = slice(g * gw, (g + 1) * gw)
            pairs = []
            for j in range(hpg // 2):
                plo = g * gw + j * LANES
                rhs = jnp.concatenate([xb[:, plo:plo + LANES], htb[:, j * LANES:(j + 1) * LANES]], axis=0)
                lhs = []
                for hh in range(2):
                    lane = SSD_HEADS * direction + g * hpg + 2 * j + hh
                    col = pos_s[:, lane:lane + 1]
                    row = pos_t[lane:lane + 1, :]
                    seg = (row - col) if rev else (col - row)
                    decay = jnp.where(keep, jnp.exp(seg), 0.0) * dt_t[lane:lane + 1, :]
                    lhs.append(jnp.concatenate([(gmat * decay).astype(BF16),
                                                (cg32 * fac_s[:, lane:lane + 1]).astype(BF16)], axis=1))
                    yield
                res = _dot(jnp.concatenate(lhs, axis=0), rhs)
                pairs.append(jnp.where(lane_lo, res[:q], res[q:]))
                yield
            y_groups.append(jnp.concatenate(pairs, axis=1))
            ht_ref[g] = ht * dec_state[:, glanes] + _dot_tn(bg, xw[:, glanes])
            yield
        ys[c] = jnp.concatenate(y_groups, axis=1)
    return jnp.concatenate(ys, axis=0)


def _gla_direction(qh, kh, vh, glog, s_ref, rev):
    t = qh.shape[0]
    q = GLA_CHUNK
    pair = 2 * q
    ri = _iota2((t, t), 0)
    ci = _iota2((t, t), 1)
    order_ok = (ri <= ci) if rev else (ri >= ci)
    chunk_bits = _log2(q)
    prefix = (((ri >> chunk_bits) == (ci >> chunk_bits)) & order_ok).astype(BF16)
    gc = _dot_split_rhs(prefix, glog, 2)
    yield
    q_in = qh * jnp.exp(gc)
    k_in = kh * jnp.exp(-gc)
    yield
    pr = _iota2((pair, GLA_HEADS * pair), 0)
    pc = _iota2((pair, GLA_HEADS * pair), 1) & (pair - 1)
    causal = ((pr >> chunk_bits) == (pc >> chunk_bits)) & ((pc >= pr) if rev else (pc <= pr))
    dk_bits, dv_bits = _log2(GLA_DK), _log2(GLA_DV)
    khead = _iota2((GLA_KEY_DIM, pair), 0) >> dk_bits
    vhead_rows = _iota2((GLA_HEADS * pair, GLA_WIDTH), 0) >> _log2(pair)
    vhead_cols = _iota2((GLA_HEADS * pair, GLA_WIDTH), 1) >> dv_bits
    blockdiag = ((_iota2((GLA_KEY_DIM, GLA_WIDTH), 0) >> dk_bits)
                 == (_iota2((GLA_KEY_DIM, GLA_WIDTH), 1) >> dv_bits))
    outs = [None] * (t // q)
    pair_order = range(t // pair - 1, -1, -1) if rev else range(t // pair)
    for pi in pair_order:
        psl = slice(pi * pair, (pi + 1) * pair)
        k_t = k_in[psl].T
        gc_t = gc[psl].T
        yield
        kbd = jnp.concatenate([jnp.where(khead == h, k_t, 0.0) for h in range(GLA_HEADS)], axis=1).astype(BF16)
        att = _dot(q_in[psl].astype(BF16), kbd)
        yield
        att = jnp.where(causal, att, 0.0).astype(BF16)
        vp = vh[psl].astype(BF16)
        vbd = jnp.where(vhead_rows == vhead_cols, jnp.concatenate([vp] * GLA_HEADS, axis=0), 0.0)
        o_intra = _dot(att, vbd)
        yield
        chunk_order = (1, 0) if rev else (0, 1)
        for cc in chunk_order:
            c = 2 * pi + cc
            sl = slice(c * q, (c + 1) * q)
            edge = cc * q if rev else cc * q + q - 1
            egl_col = jnp.exp(gc_t[:, edge:edge + 1])
            st = s_ref[...]
            o_inter = _dot(q_in[sl].astype(BF16), st.astype(BF16))
            outs[c] = o_intra[cc * q:(cc + 1) * q] + o_inter
            kw_t = (k_t[:, cc * q:(cc + 1) * q] * egl_col).astype(BF16)
            upd = _dot(kw_t, vh[sl].astype(BF16))
            s_ref[...] = st * egl_col + jnp.where(blockdiag, upd, 0.0)
            yield
    return jnp.concatenate(outs, axis=0)


def _head_rms(x, w_row):
    n = x.shape[1]
    head_bits = _log2(HEAD_DIM)
    ones = ((_iota2((n, n), 0) >> head_bits) == (_iota2((n, n), 1) >> head_bits)).astype(BF16)
    ms = _dot((x * x).astype(BF16), ones) * (1.0 / HEAD_DIM)
    return x * lax.rsqrt(ms + EPS) * w_row


def _rope(x, cos, sin_signed):
    n = x.shape[1]
    quarter = HEAD_DIM // 4
    first = (_iota2((1, n), 1) & (2 * quarter - 1)) < quarter
    swapped = jnp.where(first, pltpu.roll(x, n - quarter, 1), pltpu.roll(x, quarter, 1))
    return x * cos + swapped * sin_signed


def _attend_block(q_blk, kcat, vt, sink2, edge_masks):
    assert ATT_GROUP == 2 and ATT_KV_HEADS == 2
    blk = q_blk.shape[0]
    lane = _iota2((blk, LANES), 1)
    pieces = []
    for kvh in range(ATT_KV_HEADS):
        qt = q_blk[:, kvh * LANES:(kvh + 1) * LANES]
        rolled = pltpu.roll(qt, HEAD_DIM, 1)
        own = (lane < HEAD_DIM) if kvh == 0 else (lane >= HEAD_DIM)
        for g in range(ATT_GROUP):
            pieces.append(jnp.where(own, qt if g == kvh else rolled, 0.0))
    q_all = jnp.concatenate(pieces, axis=0).astype(BF16)
    yield
    st = _dot_nt(kcat, q_all)
    yield
    if edge_masks is not None:
        row, keep_before, keep_after = edge_masks
        st = jnp.concatenate([st[:row],
                              jnp.where(keep_before, st[row:row + blk], NEG_INF),
                              st[row + blk:row + 2 * blk],
                              jnp.where(keep_after, st[row + 2 * blk:], NEG_INF)], axis=0)
    m = jnp.maximum(jnp.max(st, axis=0, keepdims=True), sink2)
    yield
    p = jnp.exp2(st - m)
    yield
    den = jnp.sum(p, axis=0, keepdims=True) + jnp.exp2(sink2 - m)
    pb = p.astype(BF16)
    yield
    z = []
    for kvh in range(ATT_KV_HEADS):
        qcols = slice(kvh * ATT_GROUP * blk, (kvh + 1) * ATT_GROUP * blk)
        ot = _dot(vt[kvh * HEAD_DIM:(kvh + 1) * HEAD_DIM], pb[:, qcols]) / den[:, qcols]
        for g in range(ATT_GROUP):
            z.append(ot[:, g * blk:(g + 1) * blk])
        yield
    return jnp.concatenate(z, axis=0).T


def _gla_inputs(rest_ref, glog_col):
    qh = rest_ref[:, P_GQ:P_GQ + GLA_KEY_DIM] * (GLA_DK ** -0.5)
    kh = rest_ref[:, P_GK:P_GK + GLA_KEY_DIM]
    vh = rest_ref[:, P_GV:P_GV + GLA_WIDTH]
    return qh, kh, vh, rest_ref[:, glog_col:glog_col + GLA_KEY_DIM]


def _load_states(ht_ref, s_ref, ht0_ref, s0_ref):
    hpg = SSD_HEADS // SSD_GROUPS
    for g in range(SSD_GROUPS):
        blk = ht0_ref[g * hpg:(g + 1) * hpg].reshape(hpg * SSD_HEAD_DIM, SSD_STATE)
        ht_ref[g] = blk.T
    rows = []
    for h in range(GLA_HEADS):
        pieces = []
        if h > 0:
            pieces.append(jnp.zeros((GLA_DK, h * GLA_DV), F32))
        pieces.append(s0_ref[h])
        if h < GLA_HEADS - 1:
            pieces.append(jnp.zeros((GLA_DK, (GLA_HEADS - 1 - h) * GLA_DV), F32))
        rows.append(jnp.concatenate(pieces, axis=1))
    s_ref[...] = jnp.concatenate(rows, axis=0)


def _store_states(ht_ref, s_ref, htf_ref, sf_ref):
    hpg = SSD_HEADS // SSD_GROUPS
    for g in range(SSD_GROUPS):
        htf_ref[g * hpg:(g + 1) * hpg] = ht_ref[g].T.reshape(hpg, SSD_HEAD_DIM, SSD_STATE)
    st = s_ref[...]
    for h in range(GLA_HEADS):
        sf_ref[h] = st[h * GLA_DK:(h + 1) * GLA_DK, h * GLA_DV:(h + 1) * GLA_DV]


HT_SHAPE = (SSD_GROUPS, SSD_STATE, SSD_WIDTH // SSD_GROUPS)
S_SHAPE = (GLA_KEY_DIM, GLA_WIDTH)
SSD_STATE_SHAPE = (SSD_HEADS, SSD_HEAD_DIM, SSD_STATE)
GLA_STATE_SHAPE = (GLA_HEADS, GLA_DK, GLA_DV)


def _mix_bwd_kernel(*refs, is_ctx, n_tiles):
    it = iter(refs)
    xs_ref = next(it)
    bc_ref = next(it)
    rest_ref = next(it)
    cp_ref = next(it)
    if is_ctx:
        next(it)
        next(it)
    else:
        ht0_ref = next(it)
        s0_ref = next(it)
    yb_ref = next(it)
    ob_ref = next(it)
    if is_ctx:
        htf_ref = next(it)
        sf_ref = next(it)
    ht_ref = next(it)
    s_ref = next(it)

    @pl.when(pl.program_id(1) == 0)
    def _():
        for u in range(SEQ_PER_STEP):
            if is_ctx:
                ht_ref[u] = jnp.zeros(HT_SHAPE, F32)
                s_ref[u] = jnp.zeros(S_SHAPE, F32)
            else:
                _load_states(ht_ref.at[u], s_ref.at[u], ht0_ref.at[u], s0_ref.at[u])

    a_small_row = -jnp.exp(cp_ref[CP_ALOG:CP_ALOG + 1, :LANES])
    tasks = []
    for u in range(SEQ_PER_STEP):
        rest_u = rest_ref.at[u]
        tasks.append(_ssd_direction(xs_ref[u], bc_ref[u], rest_u[:, P_DT:P_DT + LANES], a_small_row, ht_ref.at[u],
                                    True, 1))
        tasks.append(_gla_direction(*_gla_inputs(rest_u, P_GLR), s_ref.at[u], True))
    outs = _run_interleaved(tasks)
    for u in range(SEQ_PER_STEP):
        yb_ref[u] = outs[2 * u]
        ob_ref[u] = outs[2 * u + 1]
        if is_ctx:
            _store_states(ht_ref.at[u], s_ref.at[u], htf_ref.at[u], sf_ref.at[u])


def _state_specs(l, direction):
    return [pl.BlockSpec((SEQ_PER_STEP, None, None) + SSD_STATE_SHAPE, lambda s, t: (s, l, direction, 0, 0, 0)),
            pl.BlockSpec((SEQ_PER_STEP, None, None) + GLA_STATE_SHAPE, lambda s, t: (s, l, direction, 0, 0, 0))]


def _per_seq(a, n_seq):
    return a.reshape(n_seq, a.shape[0] // n_seq, a.shape[1])


def _mix_bwd(xs, bc, rest, cpack, l, states, *, n_seq, n_tiles, is_ctx):
    u = SEQ_PER_STEP
    seq_len = n_tiles * TILE
    row = lambda s, t: (s, n_tiles - 1 - t, 0)
    in_specs = [
        pl.BlockSpec((u, TILE, SSD_WIDTH), row),
        pl.BlockSpec((u, TILE, 2 * SSD_BC), row),
        pl.BlockSpec((u, TILE, BWD_COLS), row),
        pl.BlockSpec((None, CP_ROWS, D_MODEL), lambda s, t: (l, 0, 0)),
    ]
    args = [_per_seq(xs, n_seq), _per_seq(bc, n_seq), _per_seq(rest, n_seq), cpack]
    out_specs = [pl.BlockSpec((u, TILE, SSD_WIDTH), row), pl.BlockSpec((u, TILE, GLA_WIDTH), row)]
    out_shape = [jax.ShapeDtypeStruct((n_seq, seq_len, SSD_WIDTH), F32),
                 jax.ShapeDtypeStruct((n_seq, seq_len, GLA_WIDTH), F32)]
    aliases = {}
    if is_ctx:
        in_specs += [pl.BlockSpec(memory_space=pl.ANY), pl.BlockSpec(memory_space=pl.ANY)]
        out_specs += _state_specs(l, 1)
        out_shape += [jax.ShapeDtypeStruct(a.shape, a.dtype) for a in states]
        aliases = {len(args): len(out_shape) - 2, len(args) + 1: len(out_shape) - 1}
    else:
        in_specs += _state_specs(l, 1)
    args += list(states)
    return pl.pallas_call(
        functools.partial(_mix_bwd_kernel, is_ctx=is_ctx, n_tiles=n_tiles),
        grid=(n_seq // u, n_tiles),
        in_specs=in_specs,
        out_specs=out_specs,
        out_shape=out_shape,
        input_output_aliases=aliases,
        scratch_shapes=[pltpu.VMEM((u,) + HT_SHAPE, F32), pltpu.VMEM((u,) + S_SHAPE, F32)],
        compiler_params=pltpu.CompilerParams(dimension_semantics=("arbitrary", "arbitrary"),
                                             vmem_limit_bytes=VMEM_LIMIT_BYTES),
        name="mix_bwd_ctx" if is_ctx else "mix_bwd_lat",
    )(*args)


def _mix_fwd_kernel(*refs, is_ctx, n_tiles, layer):
    it = iter(refs)
    sink_ref = next(it)
    x_ref = next(it)
    gate_ref = next(it)
    rest_ref = next(it)
    xs_ref = next(it)
    bc_ref = next(it)
    yb_ref = next(it)
    ob_ref = next(it)
    cp_ref = next(it)
    wout_ref = next(it)
    if is_ctx:
        for _ in range(4):
            next(it)
    else:
        ht0_ref = next(it)
        s0_ref = next(it)
        kvprev_ref = next(it)
        kvnext_ref = next(it)
        cs_ref = next(it)
        csp_ref = next(it)
        csn_ref = next(it)
        kc_ref = next(it)
        vc_ref = next(it)
    o_ref = next(it)
    if is_ctx:
        kout_ref = next(it)
        vout_ref = next(it)
        htf_ref = next(it)
        sf_ref = next(it)
    ht_ref = next(it)
    s_ref = next(it)

    tile_idx = pl.program_id(1)

    @pl.when(tile_idx == 0)
    def _():
        for u in range(SEQ_PER_STEP):
            if is_ctx:
                ht_ref[u] = jnp.zeros(HT_SHAPE, F32)
                s_ref[u] = jnp.zeros(S_SHAPE, F32)
            else:
                _load_states(ht_ref.at[u], s_ref.at[u], ht0_ref.at[u], s0_ref.at[u])

    tasks = []
    for u in range(SEQ_PER_STEP):
        rest_u = rest_ref.at[u]
        tasks.append(_fwd_ssd(xs_ref[u], bc_ref[u], yb_ref[u], rest_u, cp_ref, ht_ref.at[u]))
        tasks.append(_fwd_gla(ob_ref[u], rest_u, cp_ref, s_ref.at[u]))
        tasks.append(_fwd_attention(
            u, tile_idx, sink_ref, rest_u, cp_ref,
            (kout_ref, vout_ref) if is_ctx else
            (kvprev_ref.at[u], kvnext_ref.at[u], cs_ref, csp_ref, csn_ref, kc_ref, vc_ref),
            is_ctx=is_ctx, n_tiles=n_tiles, layer=layer))
    parts = _run_interleaved(tasks)
    mixed = [jnp.concatenate(parts[3 * u:3 * u + 3], axis=1).astype(BF16) for u in range(SEQ_PER_STEP)]
    y = _dot(jnp.concatenate(mixed, axis=0), wout_ref[...])
    for u in range(SEQ_PER_STEP):
        gate = gate_ref[...] if is_ctx else gate_ref[u]
        o_ref[u] = x_ref[u] + gate * y[u * TILE:(u + 1) * TILE]
        if is_ctx:
            _store_states(ht_ref.at[u], s_ref.at[u], htf_ref.at[u], sf_ref.at[u])


def _fwd_ssd(xs, bc, yb, rest_ref, cp_ref, ht_ref):
    a_small_row = -jnp.exp(cp_ref[CP_ALOG:CP_ALOG + 1, :LANES])
    y_f = yield from _ssd_direction(xs, bc, rest_ref[:, P_DT:P_DT + LANES], a_small_row, ht_ref, False, 0)
    y = y_f + yb + xs * cp_ref[CP_DSKIP:CP_DSKIP + 1, :SSD_WIDTH]
    y = y * _silu(rest_ref[:, P_Z:P_Z + SSD_WIDTH])
    yield
    return y * lax.rsqrt(jnp.mean(y * y, axis=-1, keepdims=True) + EPS) * cp_ref[CP_SSDNW:CP_SSDNW + 1, :SSD_WIDTH]


def _fwd_gla(ob, rest_ref, cp_ref, s_ref):
    o_f = yield from _gla_direction(*_gla_inputs(rest_ref, P_GLF), s_ref, False)
    o = o_f + ob
    yield
    return _head_rms(o, cp_ref[CP_GLANW:CP_GLANW + 1, :GLA_WIDTH]) * _silu(rest_ref[:, P_GG:P_GG + GLA_WIDTH])


def _fwd_attention(u, tile_idx, sink_ref, rest_ref, cp_ref, extra, *, is_ctx, n_tiles, layer):
    if is_ctx:
        kout_ref, vout_ref = extra
    else:
        kvprev_ref, kvnext_ref, cs_ref, csp_ref, csn_ref, kc_ref, vc_ref = extra
    t = TILE
    knw = cp_ref[CP_KNW:CP_KNW + 1, :ATT_KV_DIM]
    qn = _head_rms(rest_ref[:, P_AQ:P_AQ + ATT_WIDTH], cp_ref[CP_QNW:CP_QNW + 1, :ATT_WIDTH])
    yield
    kn = _head_rms(rest_ref[:, P_AK:P_AK + ATT_KV_DIM], knw)
    vv = rest_ref[:, P_AV:P_AV + ATT_KV_DIM]
    yield
    qscale = (HEAD_DIM ** -0.5) * LOG2E
    blk = ATT_BLOCK
    n_blk = t // blk
    sink2 = jnp.concatenate([jnp.full((1, blk), sink_ref[layer, h] * LOG2E, F32) for h in range(ATT_HEADS)], axis=1)
    blocks = []
    if is_ctx:
        kout_ref[u] = kn
        vout_ref[u] = vv
        qs = qn * qscale
        kb = kn.astype(BF16)
        v_t = vv.T.astype(BF16)
        yield
        for nb in range(n_blk):
            blocks.append((yield from _attend_block(qs[nb * blk:(nb + 1) * blk], kb, v_t, sink2, None)))
    else:
        cos = cs_ref[:, :ATT_KV_DIM]
        sin = cs_ref[:, ATT_KV_DIM:]
        qr = _rope(qn, jnp.concatenate([cos, cos], axis=1), jnp.concatenate([sin, sin], axis=1)) * qscale
        yield
        k_halo_p = _rope(_head_rms(kvprev_ref[:, :ATT_KV_DIM], knw), csp_ref[:, :ATT_KV_DIM], csp_ref[:, ATT_KV_DIM:])
        k_halo_n = _rope(_head_rms(kvnext_ref[:, :ATT_KV_DIM], knw), csn_ref[:, :ATT_KV_DIM], csn_ref[:, ATT_KV_DIM:])
        yield
        k_loc = jnp.concatenate([k_halo_p, _rope(kn, cos, sin), k_halo_n], axis=0).astype(BF16)
        kc = kc_ref[u].astype(BF16)
        n_ctx = kc.shape[0]
        yield
        v_all_t = jnp.concatenate([vc_ref[u], kvprev_ref[:, ATT_KV_DIM:], vv, kvnext_ref[:, ATT_KV_DIM:]],
                                  axis=0).T.astype(BF16)
        yield
        assert WINDOW == ATT_BLOCK
        kj = _iota2((blk, ATT_HEADS * blk), 0)
        qi = _iota2((blk, ATT_HEADS * blk), 1) & (blk - 1)
        for nb in range(n_blk):
            kcat = jnp.concatenate([kc, k_loc[nb * blk:(nb + 3) * blk]], axis=0)
            vt = jnp.concatenate([v_all_t[:, :n_ctx], v_all_t[:, n_ctx + nb * blk:n_ctx + (nb + 3) * blk]], axis=1)
            off_p = jnp.where(tile_idx > 0, 0, 2 * blk) if nb == 0 else 0
            off_n = jnp.where(tile_idx < n_tiles - 1, 0, 2 * blk) if nb == n_blk - 1 else 0
            masks = (n_ctx, kj >= qi + off_p, kj <= qi - off_n)
            blocks.append((yield from _attend_block(qr[nb * blk:(nb + 1) * blk], kcat, vt, sink2, masks)))
    return jnp.concatenate(blocks, axis=0) * _silu(rest_ref[:, P_AG:P_AG + ATT_WIDTH])


def _mix_fwd(x, mod5, gate_row, xs, bc, rest, yb, ob, cpack, sink, w_out, l, extra, *, n_seq, n_tiles, is_ctx):
    u = SEQ_PER_STEP
    seq_len = n_tiles * TILE
    row = lambda s, t: (s, t, 0)
    rest3 = _per_seq(rest, n_seq)
    if is_ctx:
        gate_spec = _mod_spec(l, 2, lambda s, t: gate_row(s))
    else:
        gate_spec = pl.BlockSpec((None, u, None, 1, D_MODEL), lambda s, t: (l, s, 2, 0, 0))
    in_specs = [
        pl.BlockSpec(memory_space=pltpu.SMEM),
        pl.BlockSpec((u, TILE, D_MODEL), row),
        gate_spec,
        pl.BlockSpec((u, TILE, D_REST), row),
        pl.BlockSpec((u, TILE, SSD_WIDTH), row),
        pl.BlockSpec((u, TILE, 2 * SSD_BC), row),
        pl.BlockSpec((u, TILE, SSD_WIDTH), row),
        pl.BlockSpec((u, TILE, GLA_WIDTH), row),
        pl.BlockSpec((None, CP_ROWS, D_MODEL), lambda s, t: (l, 0, 0)),
        pl.BlockSpec((None, D_MIX, D_MODEL), lambda s, t: (l, 0, 0)),
    ]
    args = [sink, _per_seq(x, n_seq), mod5, rest3, _per_seq(xs, n_seq), _per_seq(bc, n_seq), yb, ob, cpack, w_out]
    out_specs = [pl.BlockSpec((u, TILE, D_MODEL), row)]
    out_shape = [jax.ShapeDtypeStruct((n_seq, seq_len, D_MODEL), F32)]
    aliases = {}
    if is_ctx:
        kv_spec = pl.BlockSpec((u, None, TILE, ATT_KV_DIM), lambda s, t: (s, l, t, 0))
        in_specs += [pl.BlockSpec(memory_space=pl.ANY)] * 4
        out_specs += [kv_spec, kv_spec] + _state_specs(l, 0)
        out_shape += [jax.ShapeDtypeStruct(a.shape, a.dtype) for a in extra]
        aliases = {len(args) + i: 1 + i for i in range(4)}
        args += list(extra)
    else:
        state_ssd, state_gla, rope_cs, cache_k, cache_v = extra
        per_blk = TILE // ATT_BLOCK
        kv_col = P_AK // (2 * ATT_KV_DIM)
        last_blk = n_tiles * per_blk - 1
        prev_blk = lambda t: jnp.maximum(t * per_blk - 1, 0)
        next_blk = lambda t: jnp.minimum((t + 1) * per_blk, last_blk)
        n_ctx = cache_k.shape[2]
        ctx_spec = pl.BlockSpec((u, None, n_ctx, ATT_KV_DIM), lambda s, t: (s, l, 0, 0))
        in_specs += _state_specs(l, 0) + [
            pl.BlockSpec((u, ATT_BLOCK, 2 * ATT_KV_DIM), lambda s, t: (s, prev_blk(t), kv_col)),
            pl.BlockSpec((u, ATT_BLOCK, 2 * ATT_KV_DIM), lambda s, t: (s, next_blk(t), kv_col)),
            pl.BlockSpec((TILE, 2 * ATT_KV_DIM), lambda s, t: (t, 0)),
            pl.BlockSpec((ATT_BLOCK, 2 * ATT_KV_DIM), lambda s, t: (prev_blk(t), 0)),
            pl.BlockSpec((ATT_BLOCK, 2 * ATT_KV_DIM), lambda s, t: (next_blk(t), 0)),
            ctx_spec,
            ctx_spec,
        ]
        args += [state_ssd, state_gla, rest3, rest3, rope_cs, rope_cs, rope_cs, cache_k, cache_v]
    outs = pl.pallas_call(
        functools.partial(_mix_fwd_kernel, is_ctx=is_ctx, n_tiles=n_tiles, layer=l),
        grid=(n_seq // u, n_tiles),
        in_specs=in_specs,
        out_specs=out_specs,
        out_shape=out_shape,
        input_output_aliases=aliases,
        scratch_shapes=[pltpu.VMEM((u,) + HT_SHAPE, F32), pltpu.VMEM((u,) + S_SHAPE, F32)],
        compiler_params=pltpu.CompilerParams(dimension_semantics=("arbitrary", "arbitrary"),
                                             vmem_limit_bytes=VMEM_LIMIT_BYTES),
        name="mix_fwd_ctx" if is_ctx else "mix_fwd_lat",
    )(*args)
    return [outs[0].reshape(n_seq * seq_len, D_MODEL)] + list(outs[1:])


def _pad_cols(a, width):
    return jnp.pad(a, [(0, 0)] * (a.ndim - 1) + [(0, width - a.shape[-1])])


def _prep_w_in(w_in):
    sizes = (SSD_WIDTH, SSD_CONV_DIM, 2 * SSD_HEADS, GLA_KEY_DIM, GLA_KEY_DIM, GLA_WIDTH, GLA_WIDTH, 2 * GLA_LOWRANK,
             ATT_WIDTH, ATT_KV_DIM, ATT_KV_DIM, ATT_WIDTH)
    parts, start = [], 0
    for size in sizes:
        parts.append(w_in[..., start:start + size])
        start += size
    z, xbc, dt, gq, gk, gv, gg, glr, aq, ak, av, ag = parts
    cols = [xbc, _pad_cols(dt, LANES), gq, gk, _pad_cols(glr, LANES), gv, ak, av, z, gg, aq, ag]
    return jnp.concatenate(cols, axis=-1).astype(BF16)


def _pack_small_params(norm_w, conv_w, conv_b, ssd_a_log, ssd_dt_bias, ssd_d, ssd_norm_w, gla_gk_b, gla_norm_w,
                       attn_q_norm, attn_k_norm):
    depth = norm_w.shape[0]
    row = lambda a: _pad_cols(a.reshape(depth, 1, -1), D_MODEL)
    blank = lambda n: jnp.zeros((depth, n, D_MODEL), F32)
    rows = [
        _pad_cols(conv_w, D_MODEL), blank(CP_CONVB - SSD_CONV),
        row(conv_b), row(ssd_dt_bias), row(ssd_a_log), row(gla_gk_b[:, 0]), row(gla_gk_b[:, 1]),
        row(jnp.repeat(ssd_d, SSD_HEAD_DIM, axis=-1)), row(ssd_norm_w), row(jnp.tile(gla_norm_w, (1, GLA_HEADS))),
        row(jnp.tile(attn_q_norm, (1, ATT_HEADS))), row(jnp.tile(attn_k_norm, (1, ATT_KV_HEADS))), row(norm_w),
        blank(CP_ROWS - CP_NW - 1),
    ]
    return jnp.concatenate(rows, axis=1)


def _pad_gate_weights(gla_gk_up):
    depth = gla_gk_up.shape[0]
    w = jnp.zeros((depth, LANES, 2 * GLA_KEY_DIM), F32)
    for d in range(2):
        w = w.at[:, d * GLA_LOWRANK:(d + 1) * GLA_LOWRANK, d * GLA_KEY_DIM:(d + 1) * GLA_KEY_DIM].set(gla_gk_up[:, d])
    return w


def _rope_tables(seq_len):
    quarter = HEAD_DIM // 4
    rows = seq_len // GRID_W
    row_pos = jnp.repeat(jnp.arange(rows, dtype=F32), GRID_W)
    col_pos = jnp.tile(jnp.arange(GRID_W, dtype=F32), rows)
    inv = ROPE_THETA ** (-jnp.arange(quarter, dtype=F32) / quarter)
    ang_r = row_pos[:, None] * inv[None, :]
    ang_c = col_pos[:, None] * inv[None, :]
    cos = jnp.concatenate([jnp.cos(ang_r), jnp.cos(ang_r), jnp.cos(ang_c), jnp.cos(ang_c)], axis=1)
    sin = jnp.concatenate([-jnp.sin(ang_r), jnp.sin(ang_r), -jnp.sin(ang_c), jnp.sin(ang_c)], axis=1)
    return jnp.tile(cos, (1, ATT_KV_HEADS)), jnp.tile(sin, (1, ATT_KV_HEADS))


def kernel(x_prompt, x_sample, c, cache_k, cache_v, state_ssd, state_gla, c_ctx, w_ada, b_ada, norm_w, w_in, conv_w,
           conv_b, ssd_a_log, ssd_dt_bias, ssd_d, ssd_norm_w, gla_gk_up, gla_gk_b, gla_norm_w, attn_q_norm, attn_k_norm,
           attn_sink, w_out):
    batch, seq, _ = x_prompt.shape
    dec_batch, dec_seq, _ = x_sample.shape
    depth = w_in.shape[0]
    past = cache_k.shape[2]
    assert seq % TILE == 0 and dec_seq % TILE == 0
    assert (batch * seq) % IN_TILE == 0 and dec_seq % IN_TILE == 0

    rows = -(-(dec_batch + 1) // SUBLANES) * SUBLANES
    cond = jnp.zeros((rows, D_MODEL), F32).at[:dec_batch].set(c).at[dec_batch].set(c_ctx)
    mod = _modulation(cond, w_ada, b_ada)
    mod5 = mod.reshape(depth, rows, 3, 1, D_MODEL)

    w_in_p = _prep_w_in(w_in)
    w_out_b = w_out.astype(BF16)
    cpack = _pack_small_params(norm_w, conv_w, conv_b, ssd_a_log, ssd_dt_bias, ssd_d, ssd_norm_w, gla_gk_b,
                               gla_norm_w, attn_q_norm, attn_k_norm)
    gk_w = _pad_gate_weights(gla_gk_up)
    rope_cs = jnp.concatenate(_rope_tables(dec_seq), axis=1)
    cache_k2 = cache_k.reshape(dec_batch, depth, past, ATT_KV_DIM)
    cache_v2 = cache_v.reshape(dec_batch, depth, past, ATT_KV_DIM)

    x_ctx = x_prompt.reshape(batch * seq, D_MODEL)
    x_lat = x_sample.reshape(dec_batch * dec_seq, D_MODEL)
    ctx_tiles = seq // TILE
    lat_tiles = dec_seq // TILE
    lat_in_per_seq = dec_seq // IN_TILE

    new_k = jnp.zeros((batch, depth, seq, ATT_KV_DIM), F32)
    new_v = jnp.zeros((batch, depth, seq, ATT_KV_DIM), F32)
    new_ssd = jnp.zeros((batch, depth, 2) + SSD_STATE_SHAPE, F32)
    new_gla = jnp.zeros((batch, depth, 2) + GLA_STATE_SHAPE, F32)

    for l in range(depth):
        xs, bc, rest = _in_proj(x_ctx, mod5, cpack, w_in_p, gk_w, l, lambda i: dec_batch, seq)
        yb, ob, new_ssd, new_gla = _mix_bwd(xs, bc, rest, cpack, l, (new_ssd, new_gla), n_seq=batch,
                                            n_tiles=ctx_tiles, is_ctx=True)
        x_ctx, new_k, new_v, new_ssd, new_gla = _mix_fwd(
            x_ctx, mod5, lambda s: dec_batch, xs, bc, rest, yb, ob, cpack, attn_sink, w_out_b, l,
            (new_k, new_v, new_ssd, new_gla), n_seq=batch, n_tiles=ctx_tiles, is_ctx=True)

        xs, bc, rest = _in_proj(x_lat, mod5, cpack, w_in_p, gk_w, l, lambda i: i // lat_in_per_seq, dec_seq)
        yb, ob = _mix_bwd(xs, bc, rest, cpack, l, (state_ssd, state_gla), n_seq=dec_batch, n_tiles=lat_tiles,
                          is_ctx=False)
        (x_lat,) = _mix_fwd(x_lat, mod5, lambda s: s, xs, bc, rest, yb, ob, cpack, attn_sink, w_out_b, l,
                            (state_ssd, state_gla, rope_cs, cache_k2, cache_v2), n_seq=dec_batch, n_tiles=lat_tiles,
                            is_ctx=False)

    return (x_ctx.reshape(batch, seq, D_MODEL), x_lat.reshape(dec_batch, dec_seq, D_MODEL),
            new_k.reshape(batch, depth, seq, ATT_KV_HEADS, HEAD_DIM),
            new_v.reshape(batch, depth, seq, ATT_KV_HEADS, HEAD_DIM), new_ssd, new_gla)
```

```python
import functools
import math

import jax
import jax.numpy as jnp
from jax import lax
from jax.experimental import pallas as pl
from jax.experimental.pallas import tpu as pltpu

D_MODEL = 1024
GRID_W = 64
SSD_HEADS = 8
SSD_HEAD_DIM = 64
SSD_WIDTH = 512
SSD_GROUPS = 2
SSD_STATE = 128
SSD_BC = SSD_GROUPS * SSD_STATE
SSD_CONV = 5
SSD_CONV_DIM = SSD_WIDTH + 2 * SSD_BC
SSD_CHUNK = 128
GLA_HEADS = 4
GLA_DK = 32
GLA_DV = 64
GLA_KEY_DIM = GLA_HEADS * GLA_DK
GLA_WIDTH = GLA_HEADS * GLA_DV
GLA_LOWRANK = 16
GLA_GATE_NORM = 16.0
GLA_CHUNK = 64
ATT_HEADS = 4
ATT_KV_HEADS = 2
ATT_GROUP = ATT_HEADS // ATT_KV_HEADS
HEAD_DIM = 64
ATT_WIDTH = ATT_HEADS * HEAD_DIM
ATT_KV_DIM = ATT_KV_HEADS * HEAD_DIM
WINDOW = 128
ATT_BLOCK = 128
ROPE_THETA = 10000.0
D_MIX = SSD_WIDTH + GLA_WIDTH + ATT_WIDTH
EPS = 1e-6
NEG_INF = -1e30
LOG2E = math.log2(math.e)

LANES = 128
SUBLANES = 8
VMEM_LIMIT_BYTES = 56 * 1024 * 1024

P_DT = 0
P_GQ = P_DT + LANES
P_GK = P_GQ + GLA_KEY_DIM
P_GLR = P_GK + GLA_KEY_DIM
P_GV = P_GLR + LANES
P_AK = P_GV + GLA_WIDTH
P_AV = P_AK + ATT_KV_DIM
P_Z = P_AV + ATT_KV_DIM
P_GG = P_Z + SSD_WIDTH
P_AQ = P_GG + GLA_WIDTH
P_AG = P_AQ + ATT_WIDTH
D_LIN = P_AG + ATT_WIDTH
D_PROJ = SSD_CONV_DIM + D_LIN
P_GLF = D_LIN
D_REST = D_LIN + GLA_KEY_DIM
BWD_COLS = P_AK

CP_CONVW = 0
CP_CONVB = 8
CP_DTB = 9
CP_ALOG = 10
CP_GKB = 11
CP_DSKIP = 13
CP_SSDNW = 14
CP_GLANW = 15
CP_QNW = 16
CP_KNW = 17
CP_NW = 18
CP_ROWS = 24

TILE = 256
IN_TILE = 512
HALO = SUBLANES
CONV_COLS = 256
CONV_ROWS = 64
SEQ_PER_STEP = 2
BWD_SEQ_PER_STEP = 4

HI = lax.Precision.HIGHEST
F32 = jnp.float32
BF16 = jnp.bfloat16

NT_DIMS = (((1,), (1,)), ((), ()))
TN_DIMS = (((0,), (0,)), ((), ()))


def _dot(a, b):
    return jnp.dot(a, b, preferred_element_type=F32)


def _dot_nt(a, b):
    return lax.dot_general(a, b, NT_DIMS, preferred_element_type=F32)


def _dot_tn(a, b):
    return lax.dot_general(a, b, TN_DIMS, preferred_element_type=F32)


def _split(x, parts):
    out = []
    for _ in range(parts - 1):
        piece = x.astype(BF16)
        out.append(piece)
        x = x - piece.astype(F32)
    out.append(x.astype(BF16))
    return out


def _dot_split_lhs(x, w_bf16, parts):
    acc = None
    for piece in _split(x, parts):
        term = _dot(piece, w_bf16)
        acc = term if acc is None else acc + term
    return acc


def _dot_split_rhs(w_bf16, x, parts):
    acc = None
    for piece in _split(x, parts):
        term = _dot(w_bf16, piece)
        acc = term if acc is None else acc + term
    return acc


def _silu(x):
    half = 0.5 * x
    return half + half * jnp.tanh(half)


def _softplus(x):
    return jnp.maximum(x, 0.0) + jnp.log(1.0 + jnp.exp(-jnp.abs(x)))


def _log_sigmoid(x):
    return jnp.minimum(x, 0.0) - jnp.log(1.0 + jnp.exp(-jnp.abs(x)))


def _iota2(shape, dim):
    return lax.broadcasted_iota(jnp.int32, shape, dim)


def _log2(n):
    assert n > 0 and n & (n - 1) == 0
    return n.bit_length() - 1


def _run_interleaved(tasks):
    results = [None] * len(tasks)
    live = list(range(len(tasks)))
    while live:
        for i in list(live):
            try:
                next(tasks[i])
            except StopIteration as done:
                results[i] = done.value
                live.remove(i)
    return results


def _mod_kernel(cond_ref, w_ref, b_ref, o_ref):
    o_ref[...] = jnp.dot(_silu(cond_ref[...]), w_ref[...], preferred_element_type=F32, precision=HI) + b_ref[...]


def _modulation(cond, w_ada, b_ada):
    depth = w_ada.shape[0]
    rows = cond.shape[0]
    return pl.pallas_call(
        _mod_kernel,
        grid=(depth, 3),
        in_specs=[
            pl.BlockSpec((rows, D_MODEL), lambda l, j: (0, 0)),
            pl.BlockSpec((None, D_MODEL, D_MODEL), lambda l, j: (l, 0, j)),
            pl.BlockSpec((None, 1, D_MODEL), lambda l, j: (l, 0, j)),
        ],
        out_specs=pl.BlockSpec((None, rows, D_MODEL), lambda l, j: (l, 0, j)),
        out_shape=jax.ShapeDtypeStruct((depth, rows, 3 * D_MODEL), F32),
        compiler_params=pltpu.CompilerParams(dimension_semantics=("arbitrary", "arbitrary")),
        name="modulation",
    )(cond, w_ada, b_ada.reshape(depth, 1, 3 * D_MODEL))


def _in_proj_kernel(x_ref, xprev_ref, xnext_ref, shift_ref, scale_ref, cp_ref, w_ref, gkw_ref, xs_ref, bc_ref,
                    rest_ref, *, seq_len):
    gain = cp_ref[CP_NW:CP_NW + 1, :] * (1.0 + scale_ref[...])
    shift = shift_ref[...]

    def modnorm(x):
        xn = x * lax.rsqrt(jnp.mean(x * x, axis=-1, keepdims=True) + EPS)
        return (xn * gain + shift).astype(BF16)

    h = modnorm(x_ref[...])
    t = h.shape[0]
    h_up = jnp.concatenate([modnorm(xprev_ref[...]), h, modnorm(xnext_ref[...])], axis=0)
    seg = min(seq_len, t)
    pos0 = (pl.program_id(0) * t) & (seq_len - 1)
    zero_halo = jnp.zeros((HALO, CONV_COLS), F32)

    for c0 in range(0, SSD_CONV_DIM, CONV_COLS):
        up = _dot(h_up, w_ref[:, c0:c0 + CONV_COLS])
        conv_w = cp_ref[CP_CONVW:CP_CONVW + SUBLANES, c0:c0 + CONV_COLS]
        conv_b = cp_ref[CP_CONVB:CP_CONVB + 1, c0:c0 + CONV_COLS]
        for a in range(0, t, CONV_ROWS):
            if a % seq_len == 0 and seq_len <= t:
                prev = zero_halo
            elif a % seg == 0:
                prev = jnp.where(pos0 + a > 0, up[a:a + HALO, :], 0.0)
            else:
                prev = up[a:a + HALO, :]
            b = a + CONV_ROWS
            if b % seq_len == 0 and seq_len <= t:
                nxt = zero_halo
            elif b % seg == 0:
                nxt = jnp.where(pos0 + b < seq_len, up[HALO + b:2 * HALO + b, :], 0.0)
            else:
                nxt = up[HALO + b:2 * HALO + b, :]
            xbc = _conv_silu(up[HALO + a:HALO + b, :], prev, nxt, conv_w, conv_b)
            if c0 < SSD_WIDTH:
                xs_ref[a:b, c0:c0 + CONV_COLS] = xbc
            else:
                bc_ref[a:b, c0 - SSD_WIDTH:c0 - SSD_WIDTH + CONV_COLS] = xbc.astype(BF16)

    res = _dot(h, w_ref[:, SSD_CONV_DIM:])
    rest_ref[:, P_DT:P_DT + LANES] = _softplus(res[:, P_DT:P_DT + LANES] + cp_ref[CP_DTB:CP_DTB + 1, :LANES])
    rest_ref[:, P_GQ:P_GLR] = res[:, P_GQ:P_GLR]
    rest_ref[:, P_GV:D_LIN] = res[:, P_GV:]
    lr = res[:, P_GLR:P_GLR + LANES].astype(BF16)
    for d, col in ((0, P_GLF), (1, P_GLR)):
        gate = (_dot(lr, gkw_ref[:, d * GLA_KEY_DIM:(d + 1) * GLA_KEY_DIM].astype(BF16))
                + cp_ref[CP_GKB + d:CP_GKB + d + 1, :GLA_KEY_DIM])
        rest_ref[:, col:col + GLA_KEY_DIM] = _log_sigmoid(gate) * (1.0 / GLA_GATE_NORM)


def _mod_spec(l, part, row_of):
    return pl.BlockSpec((None, None, None, 1, D_MODEL), lambda *ids: (l, row_of(*ids), part, 0, 0))


def _in_proj(x, mod5, cpack, w_in_p, gk_w, l, mod_row, seq_len):
    n = x.shape[0]
    per = IN_TILE // HALO
    last_halo = n // HALO - 1
    assert seq_len & (seq_len - 1) == 0
    return pl.pallas_call(
        functools.partial(_in_proj_kernel, seq_len=seq_len),
        grid=(n // IN_TILE,),
        in_specs=[
            pl.BlockSpec((IN_TILE, D_MODEL), lambda i: (i, 0)),
            pl.BlockSpec((HALO, D_MODEL), lambda i: (jnp.maximum(i * per - 1, 0), 0)),
            pl.BlockSpec((HALO, D_MODEL), lambda i: (jnp.minimum((i + 1) * per, last_halo), 0)),
            _mod_spec(l, 0, mod_row),
            _mod_spec(l, 1, mod_row),
            pl.BlockSpec((None, CP_ROWS, D_MODEL), lambda i: (l, 0, 0)),
            pl.BlockSpec((None, D_MODEL, D_PROJ), lambda i: (l, 0, 0)),
            pl.BlockSpec((None, LANES, 2 * GLA_KEY_DIM), lambda i: (l, 0, 0)),
        ],
        out_specs=[pl.BlockSpec((IN_TILE, SSD_WIDTH), lambda i: (i, 0)),
                   pl.BlockSpec((IN_TILE, 2 * SSD_BC), lambda i: (i, 0)),
                   pl.BlockSpec((IN_TILE, D_REST), lambda i: (i, 0))],
        out_shape=[jax.ShapeDtypeStruct((n, SSD_WIDTH), F32), jax.ShapeDtypeStruct((n, 2 * SSD_BC), BF16),
                   jax.ShapeDtypeStruct((n, D_REST), F32)],
        compiler_params=pltpu.CompilerParams(dimension_semantics=("arbitrary",),
                                             vmem_limit_bytes=VMEM_LIMIT_BYTES),
        name="in_proj",
    )(x, x, x, mod5, mod5, cpack, w_in_p, gk_w)


def _conv_silu(main, prev, nxt, conv_w, conv_b):
    t = main.shape[0]
    up = jnp.concatenate([prev, main, nxt], axis=0)
    rows = t + 2 * HALO
    pad = (SSD_CONV - 1) // 2
    acc = conv_b
    for k in range(SSD_CONV):
        off = k - pad
        shifted = up if off == 0 else pltpu.roll(up, (rows - off) % rows, 0)
        acc = acc + conv_w[k:k + 1, :] * shifted[HALO:HALO + t, :]
    return _silu(acc)


def _ssd_direction(xs, bc, dt_small, a_small_row, ht_ref, rev, direction):
    t = xs.shape[0]
    q = SSD_CHUNK
    hpg = SSD_HEADS // SSD_GROUPS
    gw = hpg * SSD_HEAD_DIM
    er = _iota2((LANES, SSD_WIDTH), 0)
    ec = _iota2((LANES, SSD_WIDTH), 1)
    expand = (er == (ec >> _log2(SSD_HEAD_DIM)) + SSD_HEADS * direction).astype(BF16)
    a_small = dt_small * a_small_row
    ri = _iota2((q, q), 0)
    ci = _iota2((q, q), 1)
    tri_incl = (ri >= ci).astype(BF16)
    keep = (ri <= ci) if rev else (ri >= ci)
    lane_lo = _iota2((q, LANES), 1) < SSD_HEAD_DIM
    ys = [None] * (t // q)
    order = range(t // q - 1, -1, -1) if rev else range(t // q)
    for c in order:
        sl = slice(c * q, (c + 1) * q)
        a_s = a_small[sl]
        dts = dt_small[sl]
        cs_s = _dot_split_rhs(tri_incl, a_s, 2)
        yield
        tot_s = cs_s[q - 1:q, :]
        pos_s = (cs_s - a_s) if rev else cs_s
        if rev:
            fac_s = jnp.exp(tot_s - pos_s)
            wgt_s = jnp.exp(pos_s)
        else:
            fac_s = jnp.exp(pos_s)
            wgt_s = jnp.exp(tot_s - pos_s)
        yield
        dtw_exp = _dot((dts * wgt_s).astype(BF16), expand)
        dec_state = _dot_split_lhs(jnp.broadcast_to(jnp.exp(tot_s), (SUBLANES, LANES)), expand, 3)[0:1, :]
        yield
        xc = xs[sl]
        xb = xc.astype(BF16)
        xw = (xc * dtw_exp).astype(BF16)
        pos_t = pos_s.T
        dt_t = dts.T
        yield
        y_groups = []
        for g in range(SSD_GROUPS):
            bg = bc[sl, g * SSD_STATE:(g + 1) * SSD_STATE]
            cg = bc[sl, SSD_BC + g * SSD_STATE:SSD_BC + (g + 1) * SSD_STATE]
            cg32 = cg.astype(F32)
            gmat = _dot_nt(cg, bg)
            yield
            ht = ht_ref[g]
            htb = ht.astype(BF16)
            glanes = slice(g * gw, (g + 1) * gw)
            pairs = []
            for j in range(hpg // 2):
                plo = g * gw + j * LANES
                rhs = jnp.concatenate([xb[:, plo:plo + LANES], htb[:, j * LANES:(j + 1) * LANES]], axis=0)
                lhs = []
                for hh in range(2):
                    lane = SSD_HEADS * direction + g * hpg + 2 * j + hh
                    col = pos_s[:, lane:lane + 1]
                    row = pos_t[lane:lane + 1, :]
                    seg = (row - col) if rev else (col - row)
                    decay = jnp.where(keep, jnp.exp(seg), 0.0) * dt_t[lane:lane + 1, :]
                    lhs.append(jnp.concatenate([(gmat * decay).astype(BF16),
                                                (cg32 * fac_s[:, lane:lane + 1]).astype(BF16)], axis=1))
                    yield
                res = _dot(jnp.concatenate(lhs, axis=0), rhs)
                pairs.append(jnp.where(lane_lo, res[:q], res[q:]))
                yield
            y_groups.append(jnp.concatenate(pairs, axis=1))
            ht_ref[g] = ht * dec_state[:, glanes] + _dot_tn(bg, xw[:, glanes])
            yield
        ys[c] = jnp.concatenate(y_groups, axis=1)
    return jnp.concatenate(ys, axis=0)


def _gla_direction(qh, kh, vh, glog, s_ref, rev):
    t = qh.shape[0]
    q = GLA_CHUNK
    pair = 2 * q
    ri = _iota2((t, t), 0)
    ci = _iota2((t, t), 1)
    order_ok = (ri <= ci) if rev else (ri >= ci)
    chunk_bits = _log2(q)
    prefix = (((ri >> chunk_bits) == (ci >> chunk_bits)) & order_ok).astype(BF16)
    gc = _dot_split_rhs(prefix, glog, 2)
    yield
    q_in = qh * jnp.exp(gc)
    k_in = kh * jnp.exp(-gc)
    yield
    pr = _iota2((pair, GLA_HEADS * pair), 0)
    pc = _iota2((pair, GLA_HEADS * pair), 1) & (pair - 1)
    causal = ((pr >> chunk_bits) == (pc >> chunk_bits)) & ((pc >= pr) if rev else (pc <= pr))
    dk_bits, dv_bits = _log2(GLA_DK), _log2(GLA_DV)
    khead = _iota2((GLA_KEY_DIM, pair), 0) >> dk_bits
    vhead_rows = _iota2((GLA_HEADS * pair, GLA_WIDTH), 0) >> _log2(pair)
    vhead_cols = _iota2((GLA_HEADS * pair, GLA_WIDTH), 1) >> dv_bits
    blockdiag = ((_iota2((GLA_KEY_DIM, GLA_WIDTH), 0) >> dk_bits)
                 == (_iota2((GLA_KEY_DIM, GLA_WIDTH), 1) >> dv_bits))
    outs = [None] * (t // q)
    pair_order = range(t // pair - 1, -1, -1) if rev else range(t // pair)
    for pi in pair_order:
        psl = slice(pi * pair, (pi + 1) * pair)
        k_t = k_in[psl].T
        gc_t = gc[psl].T
        yield
        kbd = jnp.concatenate([jnp.where(khead == h, k_t, 0.0) for h in range(GLA_HEADS)], axis=1).astype(BF16)
        att = _dot(q_in[psl].astype(BF16), kbd)
        yield
        att = jnp.where(causal, att, 0.0).astype(BF16)
        vp = vh[psl].astype(BF16)
        vbd = jnp.where(vhead_rows == vhead_cols, jnp.concatenate([vp] * GLA_HEADS, axis=0), 0.0)
        o_intra = _dot(att, vbd)
        yield
        chunk_order = (1, 0) if rev else (0, 1)
        for cc in chunk_order:
            c = 2 * pi + cc
            sl = slice(c * q, (c + 1) * q)
            edge = cc * q if rev else cc * q + q - 1
            egl_col = jnp.exp(gc_t[:, edge:edge + 1])
            st = s_ref[...]
            o_inter = _dot(q_in[sl].astype(BF16), st.astype(BF16))
            outs[c] = o_intra[cc * q:(cc + 1) * q] + o_inter
            kw_t = (k_t[:, cc * q:(cc + 1) * q] * egl_col).astype(BF16)
            upd = _dot(kw_t, vh[sl].astype(BF16))
            s_ref[...] = st * egl_col + jnp.where(blockdiag, upd, 0.0)
            yield
    return jnp.concatenate(outs, axis=0)


def _head_rms(x, w_row):
    n = x.shape[1]
    head_bits = _log2(HEAD_DIM)
    ones = ((_iota2((n, n), 0) >> head_bits) == (_iota2((n, n), 1) >> head_bits)).astype(BF16)
    ms = _dot((x * x).astype(BF16), ones) * (1.0 / HEAD_DIM)
    return x * lax.rsqrt(ms + EPS) * w_row


def _rope(x, cos, sin_signed):
    n = x.shape[1]
    quarter = HEAD_DIM // 4
    first = (_iota2((1, n), 1) & (2 * quarter - 1)) < quarter
    swapped = jnp.where(first, pltpu.roll(x, n - quarter, 1), pltpu.roll(x, quarter, 1))
    return x * cos + swapped * sin_signed


def _attend_block(q_blk, kcat, vt, sink2, edge_masks):
    assert ATT_GROUP == 2 and ATT_KV_HEADS == 2
    blk = q_blk.shape[0]
    lane = _iota2((blk, LANES), 1)
    pieces = []
    for kvh in range(ATT_KV_HEADS):
        qt = q_blk[:, kvh * LANES:(kvh + 1) * LANES]
        rolled = pltpu.roll(qt, HEAD_DIM, 1)
        own = (lane < HEAD_DIM) if kvh == 0 else (lane >= HEAD_DIM)
        for g in range(ATT_GROUP):
            pieces.append(jnp.where(own, qt if g == kvh else rolled, 0.0))
    q_all = jnp.concatenate(pieces, axis=0).astype(BF16)
    yield
    st = _dot_nt(kcat, q_all)
    yield
    if edge_masks is not None:
        row, keep_before, keep_after = edge_masks
        st = jnp.concatenate([st[:row],
                              jnp.where(keep_before, st[row:row + blk], NEG_INF),
                              st[row + blk:row + 2 * blk],
                              jnp.where(keep_after, st[row + 2 * blk:], NEG_INF)], axis=0)
    m = jnp.maximum(jnp.max(st, axis=0, keepdims=True), sink2)
    yield
    pb = jnp.exp2(st - m).astype(BF16)
    yield
    sink_term = jnp.exp2(sink2 - m)
    ones_rows = jnp.ones((2 * SUBLANES, vt.shape[1]), BF16)
    z = []
    for kvh in range(ATT_KV_HEADS):
        qcols = slice(kvh * ATT_GROUP * blk, (kvh + 1) * ATT_GROUP * blk)
        v_aug = jnp.concatenate([vt[kvh * HEAD_DIM:(kvh + 1) * HEAD_DIM], ones_rows], axis=0)
        ot = _dot(v_aug, pb[:, qcols])
        den = ot[HEAD_DIM:HEAD_DIM + 1] + sink_term[:, qcols]
        ot = ot[:HEAD_DIM] / den
        for g in range(ATT_GROUP):
            z.append(ot[:, g * blk:(g + 1) * blk])
        yield
    return jnp.concatenate(z, axis=0).T


def _gla_inputs(rest_ref, glog_col):
    qh = rest_ref[:, P_GQ:P_GQ + GLA_KEY_DIM] * (GLA_DK ** -0.5)
    kh = rest_ref[:, P_GK:P_GK + GLA_KEY_DIM]
    vh = rest_ref[:, P_GV:P_GV + GLA_WIDTH]
    return qh, kh, vh, rest_ref[:, glog_col:glog_col + GLA_KEY_DIM]


def _load_states(ht_ref, s_ref, ht0_ref, s0_ref):
    hpg = SSD_HEADS // SSD_GROUPS
    for g in range(SSD_GROUPS):
        blk = ht0_ref[g * hpg:(g + 1) * hpg].reshape(hpg * SSD_HEAD_DIM, SSD_STATE)
        ht_ref[g] = blk.T
    rows = []
    for h in range(GLA_HEADS):
        pieces = []
        if h > 0:
            pieces.append(jnp.zeros((GLA_DK, h * GLA_DV), F32))
        pieces.append(s0_ref[h])
        if h < GLA_HEADS - 1:
            pieces.append(jnp.zeros((GLA_DK, (GLA_HEADS - 1 - h) * GLA_DV), F32))
        rows.append(jnp.concatenate(pieces, axis=1))
    s_ref[...] = jnp.concatenate(rows, axis=0)


def _store_states(ht_ref, s_ref, htf_ref, sf_ref):
    hpg = SSD_HEADS // SSD_GROUPS
    for g in range(SSD_GROUPS):
        htf_ref[g * hpg:(g + 1) * hpg] = ht_ref[g].T.reshape(hpg, SSD_HEAD_DIM, SSD_STATE)
    st = s_ref[...]
    for h in range(GLA_HEADS):
        sf_ref[h] = st[h * GLA_DK:(h + 1) * GLA_DK, h * GLA_DV:(h + 1) * GLA_DV]


HT_SHAPE = (SSD_GROUPS, SSD_STATE, SSD_WIDTH // SSD_GROUPS)
S_SHAPE = (GLA_KEY_DIM, GLA_WIDTH)
SSD_STATE_SHAPE = (SSD_HEADS, SSD_HEAD_DIM, SSD_STATE)
GLA_STATE_SHAPE = (GLA_HEADS, GLA_DK, GLA_DV)


def _mix_bwd_kernel(*refs, is_ctx, n_tiles):
    it = iter(refs)
    xs_ref = next(it)
    bc_ref = next(it)
    rest_ref = next(it)
    cp_ref = next(it)
    if is_ctx:
        next(it)
        next(it)
    else:
        ht0_ref = next(it)
        s0_ref = next(it)
    yb_ref = next(it)
    ob_ref = next(it)
    if is_ctx:
        htf_ref = next(it)
        sf_ref = next(it)
    ht_ref = next(it)
    s_ref = next(it)
    n_u = xs_ref.shape[0]

    @pl.when(pl.program_id(1) == 0)
    def _():
        for u in range(n_u):
            if is_ctx:
                ht_ref[u] = jnp.zeros(HT_SHAPE, F32)
                s_ref[u] = jnp.zeros(S_SHAPE, F32)
            else:
                _load_states(ht_ref.at[u], s_ref.at[u], ht0_ref.at[u], s0_ref.at[u])

    a_small_row = -jnp.exp(cp_ref[CP_ALOG:CP_ALOG + 1, :LANES])
    tasks = []
    for u in range(n_u):
        rest_u = rest_ref.at[u]
        tasks.append(_ssd_direction(xs_ref[u], bc_ref[u], rest_u[:, P_DT:P_DT + LANES], a_small_row, ht_ref.at[u],
                                    True, 1))
        tasks.append(_gla_direction(*_gla_inputs(rest_u, P_GLR), s_ref.at[u], True))
    outs = _run_interleaved(tasks)
    for u in range(n_u):
        yb_ref[u] = outs[2 * u]
        ob_ref[u] = outs[2 * u + 1]
        if is_ctx:
            _store_states(ht_ref.at[u], s_ref.at[u], htf_ref.at[u], sf_ref.at[u])


def _state_specs(l, direction, u):
    return [pl.BlockSpec((u, None, None) + SSD_STATE_SHAPE, lambda s, t: (s, l, direction, 0, 0, 0)),
            pl.BlockSpec((u, None, None) + GLA_STATE_SHAPE, lambda s, t: (s, l, direction, 0, 0, 0))]


def _per_seq(a, n_seq):
    return a.reshape(n_seq, a.shape[0] // n_seq, a.shape[1])


def _mix_bwd(xs, bc, rest, cpack, l, states, *, n_seq, n_tiles, is_ctx):
    u = BWD_SEQ_PER_STEP
    assert n_seq % u == 0
    seq_len = n_tiles * TILE
    row = lambda s, t: (s, n_tiles - 1 - t, 0)
    in_specs = [
        pl.BlockSpec((u, TILE, SSD_WIDTH), row),
        pl.BlockSpec((u, TILE, 2 * SSD_BC), row),
        pl.BlockSpec((u, TILE, BWD_COLS), row),
        pl.BlockSpec((None, CP_ROWS, D_MODEL), lambda s, t: (l, 0, 0)),
    ]
    args = [_per_seq(xs, n_seq), _per_seq(bc, n_seq), _per_seq(rest, n_seq), cpack]
    out_specs = [pl.BlockSpec((u, TILE, SSD_WIDTH), row), pl.BlockSpec((u, TILE, GLA_WIDTH), row)]
    out_shape = [jax.ShapeDtypeStruct((n_seq, seq_len, SSD_WIDTH), F32),
                 jax.ShapeDtypeStruct((n_seq, seq_len, GLA_WIDTH), F32)]
    aliases = {}
    if is_ctx:
        in_specs += [pl.BlockSpec(memory_space=pl.ANY), pl.BlockSpec(memory_space=pl.ANY)]
        out_specs += _state_specs(l, 1, u)
        out_shape += [jax.ShapeDtypeStruct(a.shape, a.dtype) for a in states]
        aliases = {len(args): len(out_shape) - 2, len(args) + 1: len(out_shape) - 1}
    else:
        in_specs += _state_specs(l, 1, u)
    args += list(states)
    return pl.pallas_call(
        functools.partial(_mix_bwd_kernel, is_ctx=is_ctx, n_tiles=n_tiles),
        grid=(n_seq // u, n_tiles),
        in_specs=in_specs,
        out_specs=out_specs,
        out_shape=out_shape,
        input_output_aliases=aliases,
        scratch_shapes=[pltpu.VMEM((u,) + HT_SHAPE, F32), pltpu.VMEM((u,) + S_SHAPE, F32)],
        compiler_params=pltpu.CompilerParams(dimension_semantics=("arbitrary", "arbitrary"),
                                             vmem_limit_bytes=VMEM_LIMIT_BYTES),
        name="mix_bwd_ctx" if is_ctx else "mix_bwd_lat",
    )(*args)


def _mix_fwd_kernel(*refs, is_ctx, n_tiles, layer):
    it = iter(refs)
    sink_ref = next(it)
    x_ref = next(it)
    gate_ref = next(it)
    rest_ref = next(it)
    xs_ref = next(it)
    bc_ref = next(it)
    yb_ref = next(it)
    ob_ref = next(it)
    cp_ref = next(it)
    wout_ref = next(it)
    if is_ctx:
        for _ in range(4):
            next(it)
    else:
        ht0_ref = next(it)
        s0_ref = next(it)
        kvprev_ref = next(it)
        kvnext_ref = next(it)
        cs_ref = next(it)
        csp_ref = next(it)
        csn_ref = next(it)
        kc_ref = next(it)
        vc_ref = next(it)
    o_ref = next(it)
    if is_ctx:
        kout_ref = next(it)
        vout_ref = next(it)
        htf_ref = next(it)
        sf_ref = next(it)
    ht_ref = next(it)
    s_ref = next(it)

    tile_idx = pl.program_id(1)

    @pl.when(tile_idx == 0)
    def _():
        for u in range(SEQ_PER_STEP):
            if is_ctx:
                ht_ref[u] = jnp.zeros(HT_SHAPE, F32)
                s_ref[u] = jnp.zeros(S_SHAPE, F32)
            else:
                _load_states(ht_ref.at[u], s_ref.at[u], ht0_ref.at[u], s0_ref.at[u])

    tasks = []
    for u in range(SEQ_PER_STEP):
        rest_u = rest_ref.at[u]
        tasks.append(_fwd_ssd(xs_ref[u], bc_ref[u], yb_ref[u], rest_u, cp_ref, ht_ref.at[u]))
        tasks.append(_fwd_gla(ob_ref[u], rest_u, cp_ref, s_ref.at[u]))
        tasks.append(_fwd_attention(
            u, tile_idx, sink_ref, rest_u, cp_ref,
            (kout_ref, vout_ref) if is_ctx else
            (kvprev_ref.at[u], kvnext_ref.at[u], cs_ref, csp_ref, csn_ref, kc_ref, vc_ref),
            is_ctx=is_ctx, n_tiles=n_tiles, layer=layer))
    parts = _run_interleaved(tasks)
    mixed = [jnp.concatenate(parts[3 * u:3 * u + 3], axis=1).astype(BF16) for u in range(SEQ_PER_STEP)]
    y = _dot(jnp.concatenate(mixed, axis=0), wout_ref[...])
    for u in range(SEQ_PER_STEP):
        gate = gate_ref[...] if is_ctx else gate_ref[u]
        o_ref[u] = x_ref[u] + gate * y[u * TILE:(u + 1) * TILE]
        if is_ctx:
            _store_states(ht_ref.at[u], s_ref.at[u], htf_ref.at[u], sf_ref.at[u])


def _fwd_ssd(xs, bc, yb, rest_ref, cp_ref, ht_ref):
    a_small_row = -jnp.exp(cp_ref[CP_ALOG:CP_ALOG + 1, :LANES])
    y_f = yield from _ssd_direction(xs, bc, rest_ref[:, P_DT:P_DT + LANES], a_small_row, ht_ref, False, 0)
    y = y_f + yb + xs * cp_ref[CP_DSKIP:CP_DSKIP + 1, :SSD_WIDTH]
    y = y * _silu(rest_ref[:, P_Z:P_Z + SSD_WIDTH])
    yield
    return y * lax.rsqrt(jnp.mean(y * y, axis=-1, keepdims=True) + EPS) * cp_ref[CP_SSDNW:CP_SSDNW + 1, :SSD_WIDTH]


def _fwd_gla(ob, rest_ref, cp_ref, s_ref):
    o_f = yield from _gla_direction(*_gla_inputs(rest_ref, P_GLF), s_ref, False)
    o = o_f + ob
    yield
    return _head_rms(o, cp_ref[CP_GLANW:CP_GLANW + 1, :GLA_WIDTH]) * _silu(rest_ref[:, P_GG:P_GG + GLA_WIDTH])


def _fwd_attention(u, tile_idx, sink_ref, rest_ref, cp_ref, extra, *, is_ctx, n_tiles, layer):
    if is_ctx:
        kout_ref, vout_ref = extra
    else:
        kvprev_ref, kvnext_ref, cs_ref, csp_ref, csn_ref, kc_ref, vc_ref = extra
    t = TILE
    knw = cp_ref[CP_KNW:CP_KNW + 1, :ATT_KV_DIM]
    qn = _head_rms(rest_ref[:, P_AQ:P_AQ + ATT_WIDTH], cp_ref[CP_QNW:CP_QNW + 1, :ATT_WIDTH])
    yield
    kn = _head_rms(rest_ref[:, P_AK:P_AK + ATT_KV_DIM], knw)
    vv = rest_ref[:, P_AV:P_AV + ATT_KV_DIM]
    yield
    qscale = (HEAD_DIM ** -0.5) * LOG2E
    blk = ATT_BLOCK
    n_blk = t // blk
    sink2 = jnp.concatenate([jnp.full((1, blk), sink_ref[layer, h] * LOG2E, F32) for h in range(ATT_HEADS)], axis=1)
    blocks = []
    if is_ctx:
        kout_ref[u] = kn
        vout_ref[u] = vv
        qs = qn * qscale
        kb = kn.astype(BF16)
        v_t = vv.T.astype(BF16)
        yield
        for nb in range(n_blk):
            blocks.append((yield from _attend_block(qs[nb * blk:(nb + 1) * blk], kb, v_t, sink2, None)))
    else:
        cos = cs_ref[:, :ATT_KV_DIM]
        sin = cs_ref[:, ATT_KV_DIM:]
        qr = _rope(qn, jnp.concatenate([cos, cos], axis=1), jnp.concatenate([sin, sin], axis=1)) * qscale
        yield
        k_halo_p = _rope(_head_rms(kvprev_ref[:, :ATT_KV_DIM], knw), csp_ref[:, :ATT_KV_DIM], csp_ref[:, ATT_KV_DIM:])
        k_halo_n = _rope(_head_rms(kvnext_ref[:, :ATT_KV_DIM], knw), csn_ref[:, :ATT_KV_DIM], csn_ref[:, ATT_KV_DIM:])
        yield
        k_loc = jnp.concatenate([k_halo_p, _rope(kn, cos, sin), k_halo_n], axis=0).astype(BF16)
        kc = kc_ref[u].astype(BF16)
        n_ctx = kc.shape[0]
        yield
        v_all_t = jnp.concatenate([vc_ref[u], kvprev_ref[:, ATT_KV_DIM:], vv, kvnext_ref[:, ATT_KV_DIM:]],
                                  axis=0).T.astype(BF16)
        yield
        assert WINDOW == ATT_BLOCK
        kj = _iota2((blk, ATT_HEADS * blk), 0)
        qi = _iota2((blk, ATT_HEADS * blk), 1) & (blk - 1)
        for nb in range(n_blk):
            kcat = jnp.concatenate([kc, k_loc[nb * blk:(nb + 3) * blk]], axis=0)
            vt = jnp.concatenate([v_all_t[:, :n_ctx], v_all_t[:, n_ctx + nb * blk:n_ctx + (nb + 3) * blk]], axis=1)
            off_p = jnp.where(tile_idx > 0, 0, 2 * blk) if nb == 0 else 0
            off_n = jnp.where(tile_idx < n_tiles - 1, 0, 2 * blk) if nb == n_blk - 1 else 0
            masks = (n_ctx, kj >= qi + off_p, kj <= qi - off_n)
            blocks.append((yield from _attend_block(qr[nb * blk:(nb + 1) * blk], kcat, vt, sink2, masks)))
    return jnp.concatenate(blocks, axis=0) * _silu(rest_ref[:, P_AG:P_AG + ATT_WIDTH])


def _mix_fwd(x, mod5, gate_row, xs, bc, rest, yb, ob, cpack, sink, w_out, l, extra, *, n_seq, n_tiles, is_ctx):
    u = SEQ_PER_STEP
    seq_len = n_tiles * TILE
    row = lambda s, t: (s, t, 0)
    rest3 = _per_seq(rest, n_seq)
    if is_ctx:
        gate_spec = _mod_spec(l, 2, lambda s, t: gate_row(s))
    else:
        gate_spec = pl.BlockSpec((None, u, None, 1, D_MODEL), lambda s, t: (l, s, 2, 0, 0))
    in_specs = [
        pl.BlockSpec(memory_space=pltpu.SMEM),
        pl.BlockSpec((u, TILE, D_MODEL), row),
        gate_spec,
        pl.BlockSpec((u, TILE, D_REST), row),
        pl.BlockSpec((u, TILE, SSD_WIDTH), row),
        pl.BlockSpec((u, TILE, 2 * SSD_BC), row),
        pl.BlockSpec((u, TILE, SSD_WIDTH), row),
        pl.BlockSpec((u, TILE, GLA_WIDTH), row),
        pl.BlockSpec((None, CP_ROWS, D_MODEL), lambda s, t: (l, 0, 0)),
        pl.BlockSpec((None, D_MIX, D_MODEL), lambda s, t: (l, 0, 0)),
    ]
    args = [sink, _per_seq(x, n_seq), mod5, rest3, _per_seq(xs, n_seq), _per_seq(bc, n_seq), yb, ob, cpack, w_out]
    out_specs = [pl.BlockSpec((u, TILE, D_MODEL), row)]
    out_shape = [jax.ShapeDtypeStruct((n_seq, seq_len, D_MODEL), F32)]
    aliases = {}
    if is_ctx:
        kv_spec = pl.BlockSpec((u, None, TILE, ATT_KV_DIM), lambda s, t: (s, l, t, 0))
        in_specs += [pl.BlockSpec(memory_space=pl.ANY)] * 4
        out_specs += [kv_spec, kv_spec] + _state_specs(l, 0, u)
        out_shape += [jax.ShapeDtypeStruct(a.shape, a.dtype) for a in extra]
        aliases = {len(args) + i: 1 + i for i in range(4)}
        args += list(extra)
    else:
        state_ssd, state_gla, rope_cs, cache_k, cache_v = extra
        per_blk = TILE // ATT_BLOCK
        kv_col = P_AK // (2 * ATT_KV_DIM)
        last_blk = n_tiles * per_blk - 1
        prev_blk = lambda t: jnp.maximum(t * per_blk - 1, 0)
        next_blk = lambda t: jnp.minimum((t + 1) * per_blk, last_blk)
        n_ctx = cache_k.shape[2]
        ctx_spec = pl.BlockSpec((u, None, n_ctx, ATT_KV_DIM), lambda s, t: (s, l, 0, 0))
        in_specs += _state_specs(l, 0, u) + [
            pl.BlockSpec((u, ATT_BLOCK, 2 * ATT_KV_DIM), lambda s, t: (s, prev_blk(t), kv_col)),
            pl.BlockSpec((u, ATT_BLOCK, 2 * ATT_KV_DIM), lambda s, t: (s, next_blk(t), kv_col)),
            pl.BlockSpec((TILE, 2 * ATT_KV_DIM), lambda s, t: (t, 0)),
            pl.BlockSpec((ATT_BLOCK, 2 * ATT_KV_DIM), lambda s, t: (prev_blk(t), 0)),
            pl.BlockSpec((ATT_BLOCK, 2 * ATT_KV_DIM), lambda s, t: (next_blk(t), 0)),
            ctx_spec,
            ctx_spec,
        ]
        args += [state_ssd, state_gla, rest3, rest3, rope_cs, rope_cs, rope_cs, cache_k, cache_v]
    outs = pl.pallas_call(
        functools.partial(_mix_fwd_kernel, is_ctx=is_ctx, n_tiles=n_tiles, layer=l),
        grid=(n_seq // u, n_tiles),
        in_specs=in_specs,
        out_specs=out_specs,
        out_shape=out_shape,
        input_output_aliases=aliases,
        scratch_shapes=[pltpu.VMEM((u,) + HT_SHAPE, F32), pltpu.VMEM((u,) + S_SHAPE, F32)],
        compiler_params=pltpu.CompilerParams(dimension_semantics=("arbitrary", "arbitrary"),
                                             vmem_limit_bytes=VMEM_LIMIT_BYTES),
        name="mix_fwd_ctx" if is_ctx else "mix_fwd_lat",
    )(*args)
    return [outs[0].reshape(n_seq * seq_len, D_MODEL)] + list(outs[1:])


def _pad_cols(a, width):
    return jnp.pad(a, [(0, 0)] * (a.ndim - 1) + [(0, width - a.shape[-1])])


def _prep_w_in(w_in):
    sizes = (SSD_WIDTH, SSD_CONV_DIM, 2 * SSD_HEADS, GLA_KEY_DIM, GLA_KEY_DIM, GLA_WIDTH, GLA_WIDTH, 2 * GLA_LOWRANK,
             ATT_WIDTH, ATT_KV_DIM, ATT_KV_DIM, ATT_WIDTH)
    parts, start = [], 0
    for size in sizes:
        parts.append(w_in[..., start:start + size])
        start += size
    z, xbc, dt, gq, gk, gv, gg, glr, aq, ak, av, ag = parts
    cols = [xbc, _pad_cols(dt, LANES), gq, gk, _pad_cols(glr, LANES), gv, ak, av, z, gg, aq, ag]
    return jnp.concatenate(cols, axis=-1).astype(BF16)


def _pack_small_params(norm_w, conv_w, conv_b, ssd_a_log, ssd_dt_bias, ssd_d, ssd_norm_w, gla_gk_b, gla_norm_w,
                       attn_q_norm, attn_k_norm):
    depth = norm_w.shape[0]
    row = lambda a: _pad_cols(a.reshape(depth, 1, -1), D_MODEL)
    blank = lambda n: jnp.zeros((depth, n, D_MODEL), F32)
    rows = [
        _pad_cols(conv_w, D_MODEL), blank(CP_CONVB - SSD_CONV),
        row(conv_b), row(ssd_dt_bias), row(ssd_a_log), row(gla_gk_b[:, 0]), row(gla_gk_b[:, 1]),
        row(jnp.repeat(ssd_d, SSD_HEAD_DIM, axis=-1)), row(ssd_norm_w), row(jnp.tile(gla_norm_w, (1, GLA_HEADS))),
        row(jnp.tile(attn_q_norm, (1, ATT_HEADS))), row(jnp.tile(attn_k_norm, (1, ATT_KV_HEADS))), row(norm_w),
        blank(CP_ROWS - CP_NW - 1),
    ]
    return jnp.concatenate(rows, axis=1)


def _pad_gate_weights(gla_gk_up):
    depth = gla_gk_up.shape[0]
    w = jnp.zeros((depth, LANES, 2 * GLA_KEY_DIM), F32)
    for d in range(2):
        w = w.at[:, d * GLA_LOWRANK:(d + 1) * GLA_LOWRANK, d * GLA_KEY_DIM:(d + 1) * GLA_KEY_DIM].set(gla_gk_up[:, d])
    return w


def _rope_tables(seq_len):
    quarter = HEAD_DIM // 4
    rows = seq_len // GRID_W
    row_pos = jnp.repeat(jnp.arange(rows, dtype=F32), GRID_W)
    col_pos = jnp.tile(jnp.arange(GRID_W, dtype=F32), rows)
    inv = ROPE_THETA ** (-jnp.arange(quarter, dtype=F32) / quarter)
    ang_r = row_pos[:, None] * inv[None, :]
    ang_c = col_pos[:, None] * inv[None, :]
    cos = jnp.concatenate([jnp.cos(ang_r), jnp.cos(ang_r), jnp.cos(ang_c), jnp.cos(ang_c)], axis=1)
    sin = jnp.concatenate([-jnp.sin(ang_r), jnp.sin(ang_r), -jnp.sin(ang_c), jnp.sin(ang_c)], axis=1)
    return jnp.tile(cos, (1, ATT_KV_HEADS)), jnp.tile(sin, (1, ATT_KV_HEADS))


def kernel(x_prompt, x_sample, c, cache_k, cache_v, state_ssd, state_gla, c_ctx, w_ada, b_ada, norm_w, w_in, conv_w,
           conv_b, ssd_a_log, ssd_dt_bias, ssd_d, ssd_norm_w, gla_gk_up, gla_gk_b, gla_norm_w, attn_q_norm, attn_k_norm,
           attn_sink, w_out):
    batch, seq, _ = x_prompt.shape
    dec_batch, dec_seq, _ = x_sample.shape
    depth = w_in.shape[0]
    past = cache_k.shape[2]
    assert seq % TILE == 0 and dec_seq % TILE == 0
    assert (batch * seq) % IN_TILE == 0 and dec_seq % IN_TILE == 0

    rows = -(-(dec_batch + 1) // SUBLANES) * SUBLANES
    cond = jnp.zeros((rows, D_MODEL), F32).at[:dec_batch].set(c).at[dec_batch].set(c_ctx)
    mod = _modulation(cond, w_ada, b_ada)
    mod5 = mod.reshape(depth, rows, 3, 1, D_MODEL)

    w_in_p = _prep_w_in(w_in)
    w_out_b = w_out.astype(BF16)
    cpack = _pack_small_params(norm_w, conv_w, conv_b, ssd_a_log, ssd_dt_bias, ssd_d, ssd_norm_w, gla_gk_b,
                               gla_norm_w, attn_q_norm, attn_k_norm)
    gk_w = _pad_gate_weights(gla_gk_up)
    rope_cs = jnp.concatenate(_rope_tables(dec_seq), axis=1)
    cache_k2 = cache_k.reshape(dec_batch, depth, past, ATT_KV_DIM)
    cache_v2 = cache_v.reshape(dec_batch, depth, past, ATT_KV_DIM)

    x_ctx = x_prompt.reshape(batch * seq, D_MODEL)
    x_lat = x_sample.reshape(dec_batch * dec_seq, D_MODEL)
    ctx_tiles = seq // TILE
    lat_tiles = dec_seq // TILE
    lat_in_per_seq = dec_seq // IN_TILE

    new_k = jnp.zeros((batch, depth, seq, ATT_KV_DIM), F32)
    new_v = jnp.zeros((batch, depth, seq, ATT_KV_DIM), F32)
    new_ssd = jnp.zeros((batch, depth, 2) + SSD_STATE_SHAPE, F32)
    new_gla = jnp.zeros((batch, depth, 2) + GLA_STATE_SHAPE, F32)

    for l in range(depth):
        xs, bc, rest = _in_proj(x_ctx, mod5, cpack, w_in_p, gk_w, l, lambda i: dec_batch, seq)
        yb, ob, new_ssd, new_gla = _mix_bwd(xs, bc, rest, cpack, l, (new_ssd, new_gla), n_seq=batch,
                                            n_tiles=ctx_tiles, is_ctx=True)
        x_ctx, new_k, new_v, new_ssd, new_gla = _mix_fwd(
            x_ctx, mod5, lambda s: dec_batch, xs, bc, rest, yb, ob, cpack, attn_sink, w_out_b, l,
            (new_k, new_v, new_ssd, new_gla), n_seq=batch, n_tiles=ctx_tiles, is_ctx=True)

        xs, bc, rest = _in_proj(x_lat, mod5, cpack, w_in_p, gk_w, l, lambda i: i // lat_in_per_seq, dec_seq)
        yb, ob = _mix_bwd(xs, bc, rest, cpack, l, (state_ssd, state_gla), n_seq=dec_batch, n_tiles=lat_tiles,
                          is_ctx=False)
        (x_lat,) = _mix_fwd(x_lat, mod5, lambda s: s, xs, bc, rest, yb, ob, cpack, attn_sink, w_out_b, l,
                            (state_ssd, state_gla, rope_cs, cache_k2, cache_v2), n_seq=dec_batch, n_tiles=lat_tiles,
                            is_ctx=False)

    return (x_ctx.reshape(batch, seq, D_MODEL), x_lat.reshape(dec_batch, dec_seq, D_MODEL),
            new_k.reshape(batch, depth, seq, ATT_KV_HEADS, HEAD_DIM),
            new_v.reshape(batch, depth, seq, ATT_KV_HEADS, HEAD_DIM), new_ssd, new_gla)
```

```python
import functools
import math

import jax
import jax.numpy as jnp
from jax import lax
from jax.experimental import pallas as pl
from jax.experimental.pallas import tpu as pltpu

D_MODEL = 1024
GRID_W = 64
SSD_HEADS = 8
SSD_HEAD_DIM = 64
SSD_WIDTH = 512
SSD_GROUPS = 2
SSD_STATE = 128
SSD_BC = SSD_GROUPS * SSD_STATE
SSD_CONV = 5
SSD_CONV_DIM = SSD_WIDTH + 2 * SSD_BC
SSD_CHUNK = 128
GLA_HEADS = 4
GLA_DK = 32
GLA_DV = 64
GLA_KEY_DIM = GLA_HEADS * GLA_DK
GLA_WIDTH = GLA_HEADS * GLA_DV
GLA_LOWRANK = 16
GLA_GATE_NORM = 16.0
GLA_CHUNK = 64
ATT_HEADS = 4
ATT_KV_HEADS = 2
ATT_GROUP = ATT_HEADS // ATT_KV_HEADS
HEAD_DIM = 64
ATT_WIDTH = ATT_HEADS * HEAD_DIM
ATT_KV_DIM = ATT_KV_HEADS * HEAD_DIM
WINDOW = 128
ATT_BLOCK = 128
ROPE_THETA = 10000.0
D_MIX = SSD_WIDTH + GLA_WIDTH + ATT_WIDTH
EPS = 1e-6
NEG_INF = -1e30
LOG2E = math.log2(math.e)

LANES = 128
SUBLANES = 8
VMEM_LIMIT_BYTES = 56 * 1024 * 1024

P_DT = 0
P_GQ = P_DT + LANES
P_GK = P_GQ + GLA_KEY_DIM
P_GLR = P_GK + GLA_KEY_DIM
P_GV = P_GLR + LANES
P_AK = P_GV + GLA_WIDTH
P_AV = P_AK + ATT_KV_DIM
P_Z = P_AV + ATT_KV_DIM
P_GG = P_Z + SSD_WIDTH
P_AQ = P_GG + GLA_WIDTH
P_AG = P_AQ + ATT_WIDTH
D_LIN = P_AG + ATT_WIDTH
D_PROJ = SSD_CONV_DIM + D_LIN
P_GLF = D_LIN
D_REST = D_LIN + GLA_KEY_DIM
BWD_COLS = P_AK

CP_CONVW = 0
CP_CONVB = 8
CP_DTB = 9
CP_ALOG = 10
CP_GKB = 11
CP_DSKIP = 13
CP_SSDNW = 14
CP_GLANW = 15
CP_QNW = 16
CP_KNW = 17
CP_NW = 18
CP_ROWS = 24

TILE = 256
IN_TILE = 512
HALO = SUBLANES
CONV_COLS = 256
CONV_ROWS = 64
SEQ_PER_STEP = 2
BWD_SEQ_PER_STEP = 4

HI = lax.Precision.HIGHEST
F32 = jnp.float32
BF16 = jnp.bfloat16

NT_DIMS = (((1,), (1,)), ((), ()))
TN_DIMS = (((0,), (0,)), ((), ()))


def _dot(a, b):
    return jnp.dot(a, b, preferred_element_type=F32)


def _dot_nt(a, b):
    return lax.dot_general(a, b, NT_DIMS, preferred_element_type=F32)


def _dot_tn(a, b):
    return lax.dot_general(a, b, TN_DIMS, preferred_element_type=F32)


def _split(x, parts):
    out = []
    for _ in range(parts - 1):
        piece = x.astype(BF16)
        out.append(piece)
        x = x - piece.astype(F32)
    out.append(x.astype(BF16))
    return out


def _dot_split_lhs(x, w_bf16, parts):
    acc = None
    for piece in _split(x, parts):
        term = _dot(piece, w_bf16)
        acc = term if acc is None else acc + term
    return acc


def _dot_split_rhs(w_bf16, x, parts):
    acc = None
    for piece in _split(x, parts):
        term = _dot(w_bf16, piece)
        acc = term if acc is None else acc + term
    return acc


def _silu(x):
    half = 0.5 * x
    return half + half * jnp.tanh(half)


def _softplus(x):
    return jnp.maximum(x, 0.0) + jnp.log(1.0 + jnp.exp(-jnp.abs(x)))


def _log_sigmoid(x):
    return jnp.minimum(x, 0.0) - jnp.log(1.0 + jnp.exp(-jnp.abs(x)))


def _iota2(shape, dim):
    return lax.broadcasted_iota(jnp.int32, shape, dim)


def _log2(n):
    assert n > 0 and n & (n - 1) == 0
    return n.bit_length() - 1


def _run_interleaved(tasks):
    results = [None] * len(tasks)
    live = list(range(len(tasks)))
    while live:
        for i in list(live):
            try:
                next(tasks[i])
            except StopIteration as done:
                results[i] = done.value
                live.remove(i)
    return results


def _mod_kernel(cond_ref, w_ref, b_ref, o_ref):
    o_ref[...] = jnp.dot(_silu(cond_ref[...]), w_ref[...], preferred_element_type=F32, precision=HI) + b_ref[...]


def _modulation(cond, w_ada, b_ada):
    depth = w_ada.shape[0]
    rows = cond.shape[0]
    return pl.pallas_call(
        _mod_kernel,
        grid=(depth, 3),
        in_specs=[
            pl.BlockSpec((rows, D_MODEL), lambda l, j: (0, 0)),
            pl.BlockSpec((None, D_MODEL, D_MODEL), lambda l, j: (l, 0, j)),
            pl.BlockSpec((None, 1, D_MODEL), lambda l, j: (l, 0, j)),
        ],
        out_specs=pl.BlockSpec((None, rows, D_MODEL), lambda l, j: (l, 0, j)),
        out_shape=jax.ShapeDtypeStruct((depth, rows, 3 * D_MODEL), F32),
        compiler_params=pltpu.CompilerParams(dimension_semantics=("arbitrary", "arbitrary")),
        name="modulation",
    )(cond, w_ada, b_ada.reshape(depth, 1, 3 * D_MODEL))


def _in_proj_kernel(x_ref, xprev_ref, xnext_ref, shift_ref, scale_ref, cp_ref, w_ref, gkw_ref, xs_ref, bc_ref,
                    rest_ref, *, seq_len):
    gain = cp_ref[CP_NW:CP_NW + 1, :] * (1.0 + scale_ref[...])
    shift = shift_ref[...]

    def modnorm(x):
        xn = x * lax.rsqrt(jnp.mean(x * x, axis=-1, keepdims=True) + EPS)
        return (xn * gain + shift).astype(BF16)

    h = modnorm(x_ref[...])
    t = h.shape[0]
    h_up = jnp.concatenate([modnorm(xprev_ref[...]), h, modnorm(xnext_ref[...])], axis=0)
    seg = min(seq_len, t)
    pos0 = (pl.program_id(0) * t) & (seq_len - 1)
    zero_halo = jnp.zeros((HALO, CONV_COLS), F32)

    for c0 in range(0, SSD_CONV_DIM, CONV_COLS):
        up = _dot(h_up, w_ref[:, c0:c0 + CONV_COLS])
        conv_w = cp_ref[CP_CONVW:CP_CONVW + SUBLANES, c0:c0 + CONV_COLS]
        conv_b = cp_ref[CP_CONVB:CP_CONVB + 1, c0:c0 + CONV_COLS]
        for a in range(0, t, CONV_ROWS):
            if a % seq_len == 0 and seq_len <= t:
                prev = zero_halo
            elif a % seg == 0:
                prev = jnp.where(pos0 + a > 0, up[a:a + HALO, :], 0.0)
            else:
                prev = up[a:a + HALO, :]
            b = a + CONV_ROWS
            if b % seq_len == 0 and seq_len <= t:
                nxt = zero_halo
            elif b % seg == 0:
                nxt = jnp.where(pos0 + b < seq_len, up[HALO + b:2 * HALO + b, :], 0.0)
            else:
                nxt = up[HALO + b:2 * HALO + b, :]
            xbc = _conv_silu(up[HALO + a:HALO + b, :], prev, nxt, conv_w, conv_b)
            if c0 < SSD_WIDTH:
                xs_ref[a:b, c0:c0 + CONV_COLS] = xbc
            else:
                bc_ref[a:b, c0 - SSD_WIDTH:c0 - SSD_WIDTH + CONV_COLS] = xbc.astype(BF16)

    res = _dot(h, w_ref[:, SSD_CONV_DIM:])
    rest_ref[:, P_DT:P_DT + LANES] = _softplus(res[:, P_DT:P_DT + LANES] + cp_ref[CP_DTB:CP_DTB + 1, :LANES])
    rest_ref[:, P_GQ:P_GLR] = res[:, P_GQ:P_GLR]
    rest_ref[:, P_GV:D_LIN] = res[:, P_GV:]
    lr = res[:, P_GLR:P_GLR + LANES].astype(BF16)
    for d, col in ((0, P_GLF), (1, P_GLR)):
        gate = (_dot(lr, gkw_ref[:, d * GLA_KEY_DIM:(d + 1) * GLA_KEY_DIM].astype(BF16))
                + cp_ref[CP_GKB + d:CP_GKB + d + 1, :GLA_KEY_DIM])
        rest_ref[:, col:col + GLA_KEY_DIM] = _log_sigmoid(gate) * (1.0 / GLA_GATE_NORM)


def _mod_spec(l, part, row_of):
    return pl.BlockSpec((None, None, None, 1, D_MODEL), lambda *ids: (l, row_of(*ids), part, 0, 0))


def _in_proj(x, mod5, cpack, w_in_p, gk_w, l, mod_row, seq_len):
    n = x.shape[0]
    per = IN_TILE // HALO
    last_halo = n // HALO - 1
    assert seq_len & (seq_len - 1) == 0
    return pl.pallas_call(
        functools.partial(_in_proj_kernel, seq_len=seq_len),
        grid=(n // IN_TILE,),
        in_specs=[
            pl.BlockSpec((IN_TILE, D_MODEL), lambda i: (i, 0)),
            pl.BlockSpec((HALO, D_MODEL), lambda i: (jnp.maximum(i * per - 1, 0), 0)),
            pl.BlockSpec((HALO, D_MODEL), lambda i: (jnp.minimum((i + 1) * per, last_halo), 0)),
            _mod_spec(l, 0, mod_row),
            _mod_spec(l, 1, mod_row),
            pl.BlockSpec((None, CP_ROWS, D_MODEL), lambda i: (l, 0, 0)),
            pl.BlockSpec((None, D_MODEL, D_PROJ), lambda i: (l, 0, 0)),
            pl.BlockSpec((None, LANES, 2 * GLA_KEY_DIM), lambda i: (l, 0, 0)),
        ],
        out_specs=[pl.BlockSpec((IN_TILE, SSD_WIDTH), lambda i: (i, 0)),
                   pl.BlockSpec((IN_TILE, 2 * SSD_BC), lambda i: (i, 0)),
                   pl.BlockSpec((IN_TILE, D_REST), lambda i: (i, 0))],
        out_shape=[jax.ShapeDtypeStruct((n, SSD_WIDTH), F32), jax.ShapeDtypeStruct((n, 2 * SSD_BC), BF16),
                   jax.ShapeDtypeStruct((n, D_REST), F32)],
        compiler_params=pltpu.CompilerParams(dimension_semantics=("arbitrary",),
                                             vmem_limit_bytes=VMEM_LIMIT_BYTES),
        name="in_proj",
    )(x, x, x, mod5, mod5, cpack, w_in_p, gk_w)


def _conv_silu(main, prev, nxt, conv_w, conv_b):
    t = main.shape[0]
    up = jnp.concatenate([prev, main, nxt], axis=0)
    rows = t + 2 * HALO
    pad = (SSD_CONV - 1) // 2
    acc = conv_b
    for k in range(SSD_CONV):
        off = k - pad
        shifted = up if off == 0 else pltpu.roll(up, (rows - off) % rows, 0)
        acc = acc + conv_w[k:k + 1, :] * shifted[HALO:HALO + t, :]
    return _silu(acc)


def _ssd_direction(xs, bc, dt_small, a_small_row, ht_ref, rev, direction):
    t = xs.shape[0]
    q = SSD_CHUNK
    hpg = SSD_HEADS // SSD_GROUPS
    gw = hpg * SSD_HEAD_DIM
    er = _iota2((LANES, SSD_WIDTH), 0)
    ec = _iota2((LANES, SSD_WIDTH), 1)
    expand = (er == (ec >> _log2(SSD_HEAD_DIM)) + SSD_HEADS * direction).astype(BF16)
    a_small = dt_small * a_small_row
    ri = _iota2((q, q), 0)
    ci = _iota2((q, q), 1)
    tri_incl = (ri >= ci).astype(BF16)
    keep = (ri <= ci) if rev else (ri >= ci)
    lane_lo = _iota2((q, LANES), 1) < SSD_HEAD_DIM
    ys = [None] * (t // q)
    order = range(t // q - 1, -1, -1) if rev else range(t // q)
    for c in order:
        sl = slice(c * q, (c + 1) * q)
        a_s = a_small[sl]
        dts = dt_small[sl]
        cs_s = _dot_split_rhs(tri_incl, a_s, 2)
        yield
        tot_s = cs_s[q - 1:q, :]
        pos_s = (cs_s - a_s) if rev else cs_s
        if rev:
            fac_s = jnp.exp(tot_s - pos_s)
            wgt_s = jnp.exp(pos_s)
        else:
            fac_s = jnp.exp(pos_s)
            wgt_s = jnp.exp(tot_s - pos_s)
        yield
        dtw_exp = _dot((dts * wgt_s).astype(BF16), expand)
        dec_state = _dot_split_lhs(jnp.broadcast_to(jnp.exp(tot_s), (SUBLANES, LANES)), expand, 3)[0:1, :]
        yield
        xc = xs[sl]
        xb = xc.astype(BF16)
        xw = (xc * dtw_exp).astype(BF16)
        pos_t = pos_s.T
        dt_t = dts.T
        yield
        y_groups = []
        for g in range(SSD_GROUPS):
            bg = bc[sl, g * SSD_STATE:(g + 1) * SSD_STATE]
            cg = bc[sl, SSD_BC + g * SSD_STATE:SSD_BC + (g + 1) * SSD_STATE]
            cg32 = cg.astype(F32)
            gmat = _dot_nt(cg, bg)
            yield
            ht = ht_ref[g]
            htb = ht.astype(BF16)
            glanes = slice(g * gw, (g + 1) * gw)
            pairs = []
            for j in range(hpg // 2):
                plo = g * gw + j * LANES
                rhs = jnp.concatenate([xb[:, plo:plo + LANES], htb[:, j * LANES:(j + 1) * LANES]], axis=0)
                lhs = []
                for hh in range(2):
                    lane = SSD_HEADS * direction + g * hpg + 2 * j + hh
                    col = pos_s[:, lane:lane + 1]
                    row = pos_t[lane:lane + 1, :]
                    seg = (row - col) if rev else (col - row)
                    decay = jnp.where(keep, jnp.exp(seg), 0.0) * dt_t[lane:lane + 1, :]
                    lhs.append(jnp.concatenate([(gmat * decay).astype(BF16),
                                                (cg32 * fac_s[:, lane:lane + 1]).astype(BF16)], axis=1))
                res = _dot(jnp.concatenate(lhs, axis=0), rhs)
                pairs.append(jnp.where(lane_lo, res[:q], res[q:]))
                yield
            y_groups.append(jnp.concatenate(pairs, axis=1))
            ht_ref[g] = ht * dec_state[:, glanes] + _dot_tn(bg, xw[:, glanes])
            yield
        ys[c] = jnp.concatenate(y_groups, axis=1)
    return jnp.concatenate(ys, axis=0)


def _gla_direction(qh, kh, vh, glog, s_ref, rev):
    t = qh.shape[0]
    q = GLA_CHUNK
    pair = 2 * q
    ri = _iota2((t, t), 0)
    ci = _iota2((t, t), 1)
    order_ok = (ri <= ci) if rev else (ri >= ci)
    chunk_bits = _log2(q)
    prefix = (((ri >> chunk_bits) == (ci >> chunk_bits)) & order_ok).astype(BF16)
    gc = _dot_split_rhs(prefix, glog, 2)
    yield
    q_in = qh * jnp.exp(gc)
    k_in = kh * jnp.exp(-gc)
    yield
    pr = _iota2((pair, GLA_HEADS * pair), 0)
    pc = _iota2((pair, GLA_HEADS * pair), 1) & (pair - 1)
    causal = ((pr >> chunk_bits) == (pc >> chunk_bits)) & ((pc >= pr) if rev else (pc <= pr))
    dk_bits, dv_bits = _log2(GLA_DK), _log2(GLA_DV)
    khead = _iota2((GLA_KEY_DIM, pair), 0) >> dk_bits
    vhead_rows = _iota2((GLA_HEADS * pair, GLA_WIDTH), 0) >> _log2(pair)
    vhead_cols = _iota2((GLA_HEADS * pair, GLA_WIDTH), 1) >> dv_bits
    blockdiag = ((_iota2((GLA_KEY_DIM, GLA_WIDTH), 0) >> dk_bits)
                 == (_iota2((GLA_KEY_DIM, GLA_WIDTH), 1) >> dv_bits))
    outs = [None] * (t // q)
    pair_order = range(t // pair - 1, -1, -1) if rev else range(t // pair)
    for pi in pair_order:
        psl = slice(pi * pair, (pi + 1) * pair)
        k_t = k_in[psl].T
        gc_t = gc[psl].T
        yield
        kbd = jnp.concatenate([jnp.where(khead == h, k_t, 0.0) for h in range(GLA_HEADS)], axis=1).astype(BF16)
        att = _dot(q_in[psl].astype(BF16), kbd)
        yield
        att = jnp.where(causal, att, 0.0).astype(BF16)
        vp = vh[psl].astype(BF16)
        vbd = jnp.where(vhead_rows == vhead_cols, jnp.concatenate([vp] * GLA_HEADS, axis=0), 0.0)
        o_intra = _dot(att, vbd)
        yield
        chunk_order = (1, 0) if rev else (0, 1)
        for cc in chunk_order:
            c = 2 * pi + cc
            sl = slice(c * q, (c + 1) * q)
            edge = cc * q if rev else cc * q + q - 1
            egl_col = jnp.exp(gc_t[:, edge:edge + 1])
            st = s_ref[...]
            o_inter = _dot(q_in[sl].astype(BF16), st.astype(BF16))
            outs[c] = o_intra[cc * q:(cc + 1) * q] + o_inter
            kw_t = (k_t[:, cc * q:(cc + 1) * q] * egl_col).astype(BF16)
            upd = _dot(kw_t, vh[sl].astype(BF16))
            s_ref[...] = st * egl_col + jnp.where(blockdiag, upd, 0.0)
            yield
    return jnp.concatenate(outs, axis=0)


def _head_rms(x, w_row):
    n = x.shape[1]
    head_bits = _log2(HEAD_DIM)
    ones = ((_iota2((n, n), 0) >> head_bits) == (_iota2((n, n), 1) >> head_bits)).astype(BF16)
    ms = _dot((x * x).astype(BF16), ones) * (1.0 / HEAD_DIM)
    return x * lax.rsqrt(ms + EPS) * w_row


def _rope(x, cos, sin_signed):
    n = x.shape[1]
    quarter = HEAD_DIM // 4
    first = (_iota2((1, n), 1) & (2 * quarter - 1)) < quarter
    swapped = jnp.where(first, pltpu.roll(x, n - quarter, 1), pltpu.roll(x, quarter, 1))
    return x * cos + swapped * sin_signed


def _attend_block(q_blk, kcat, vt, sink2, edge_masks):
    assert ATT_GROUP == 2 and ATT_KV_HEADS == 2
    blk = q_blk.shape[0]
    lane = _iota2((blk, LANES), 1)
    pieces = []
    for kvh in range(ATT_KV_HEADS):
        qt = q_blk[:, kvh * LANES:(kvh + 1) * LANES]
        rolled = pltpu.roll(qt, HEAD_DIM, 1)
        own = (lane < HEAD_DIM) if kvh == 0 else (lane >= HEAD_DIM)
        for g in range(ATT_GROUP):
            pieces.append(jnp.where(own, qt if g == kvh else rolled, 0.0))
    q_all = jnp.concatenate(pieces, axis=0).astype(BF16)
    yield
    st = _dot_nt(kcat, q_all)
    yield
    if edge_masks is not None:
        row, keep_before, keep_after = edge_masks
        st = jnp.concatenate([st[:row],
                              jnp.where(keep_before, st[row:row + blk], NEG_INF),
                              st[row + blk:row + 2 * blk],
                              jnp.where(keep_after, st[row + 2 * blk:], NEG_INF)], axis=0)
    m = jnp.maximum(jnp.max(st, axis=0, keepdims=True), sink2)
    yield
    pb = jnp.exp2(st - m).astype(BF16)
    yield
    sink_term = jnp.exp2(sink2 - m)
    ones_rows = jnp.ones((2 * SUBLANES, vt.shape[1]), BF16)
    z = []
    for kvh in range(ATT_KV_HEADS):
        qcols = slice(kvh * ATT_GROUP * blk, (kvh + 1) * ATT_GROUP * blk)
        v_aug = jnp.concatenate([vt[kvh * HEAD_DIM:(kvh + 1) * HEAD_DIM], ones_rows], axis=0)
        ot = _dot(v_aug, pb[:, qcols])
        den = ot[HEAD_DIM:HEAD_DIM + 1] + sink_term[:, qcols]
        ot = ot[:HEAD_DIM] / den
        for g in range(ATT_GROUP):
            z.append(ot[:, g * blk:(g + 1) * blk])
        yield
    return jnp.concatenate(z, axis=0).T


def _gla_inputs(rest_ref, glog_col):
    qh = rest_ref[:, P_GQ:P_GQ + GLA_KEY_DIM] * (GLA_DK ** -0.5)
    kh = rest_ref[:, P_GK:P_GK + GLA_KEY_DIM]
    vh = rest_ref[:, P_GV:P_GV + GLA_WIDTH]
    return qh, kh, vh, rest_ref[:, glog_col:glog_col + GLA_KEY_DIM]


def _load_states(ht_ref, s_ref, ht0_ref, s0_ref):
    hpg = SSD_HEADS // SSD_GROUPS
    for g in range(SSD_GROUPS):
        blk = ht0_ref[g * hpg:(g + 1) * hpg].reshape(hpg * SSD_HEAD_DIM, SSD_STATE)
        ht_ref[g] = blk.T
    rows = []
    for h in range(GLA_HEADS):
        pieces = []
        if h > 0:
            pieces.append(jnp.zeros((GLA_DK, h * GLA_DV), F32))
        pieces.append(s0_ref[h])
        if h < GLA_HEADS - 1:
            pieces.append(jnp.zeros((GLA_DK, (GLA_HEADS - 1 - h) * GLA_DV), F32))
        rows.append(jnp.concatenate(pieces, axis=1))
    s_ref[...] = jnp.concatenate(rows, axis=0)


def _store_states(ht_ref, s_ref, htf_ref, sf_ref):
    hpg = SSD_HEADS // SSD_GROUPS
    for g in range(SSD_GROUPS):
        htf_ref[g * hpg:(g + 1) * hpg] = ht_ref[g].T.reshape(hpg, SSD_HEAD_DIM, SSD_STATE)
    st = s_ref[...]
    for h in range(GLA_HEADS):
        sf_ref[h] = st[h * GLA_DK:(h + 1) * GLA_DK, h * GLA_DV:(h + 1) * GLA_DV]


HT_SHAPE = (SSD_GROUPS, SSD_STATE, SSD_WIDTH // SSD_GROUPS)
S_SHAPE = (GLA_KEY_DIM, GLA_WIDTH)
SSD_STATE_SHAPE = (SSD_HEADS, SSD_HEAD_DIM, SSD_STATE)
GLA_STATE_SHAPE = (GLA_HEADS, GLA_DK, GLA_DV)


def _mix_bwd_kernel(*refs, is_ctx, n_tiles):
    it = iter(refs)
    xs_ref = next(it)
    bc_ref = next(it)
    rest_ref = next(it)
    cp_ref = next(it)
    if is_ctx:
        next(it)
        next(it)
    else:
        ht0_ref = next(it)
        s0_ref = next(it)
    yb_ref = next(it)
    ob_ref = next(it)
    if is_ctx:
        htf_ref = next(it)
        sf_ref = next(it)
    ht_ref = next(it)
    s_ref = next(it)
    n_u = xs_ref.shape[0]

    @pl.when(pl.program_id(1) == 0)
    def _():
        for u in range(n_u):
            if is_ctx:
                ht_ref[u] = jnp.zeros(HT_SHAPE, F32)
                s_ref[u] = jnp.zeros(S_SHAPE, F32)
            else:
                _load_states(ht_ref.at[u], s_ref.at[u], ht0_ref.at[u], s0_ref.at[u])

    a_small_row = -jnp.exp(cp_ref[CP_ALOG:CP_ALOG + 1, :LANES])
    tasks = []
    for u in range(n_u):
        rest_u = rest_ref.at[u]
        tasks.append(_ssd_direction(xs_ref[u], bc_ref[u], rest_u[:, P_DT:P_DT + LANES], a_small_row, ht_ref.at[u],
                                    True, 1))
        tasks.append(_gla_direction(*_gla_inputs(rest_u, P_GLR), s_ref.at[u], True))
    outs = _run_interleaved(tasks)
    for u in range(n_u):
        yb_ref[u] = outs[2 * u]
        ob_ref[u] = outs[2 * u + 1]
        if is_ctx:
            _store_states(ht_ref.at[u], s_ref.at[u], htf_ref.at[u], sf_ref.at[u])


def _state_specs(l, direction, u):
    return [pl.BlockSpec((u, None, None) + SSD_STATE_SHAPE, lambda s, t: (s, l, direction, 0, 0, 0)),
            pl.BlockSpec((u, None, None) + GLA_STATE_SHAPE, lambda s, t: (s, l, direction, 0, 0, 0))]


def _per_seq(a, n_seq):
    return a.reshape(n_seq, a.shape[0] // n_seq, a.shape[1])


def _mix_bwd(xs, bc, rest, cpack, l, states, *, n_seq, n_tiles, is_ctx):
    u = BWD_SEQ_PER_STEP
    assert n_seq % u == 0
    seq_len = n_tiles * TILE
    row = lambda s, t: (s, n_tiles - 1 - t, 0)
    in_specs = [
        pl.BlockSpec((u, TILE, SSD_WIDTH), row),
        pl.BlockSpec((u, TILE, 2 * SSD_BC), row),
        pl.BlockSpec((u, TILE, BWD_COLS), row),
        pl.BlockSpec((None, CP_ROWS, D_MODEL), lambda s, t: (l, 0, 0)),
    ]
    args = [_per_seq(xs, n_seq), _per_seq(bc, n_seq), _per_seq(rest, n_seq), cpack]
    out_specs = [pl.BlockSpec((u, TILE, SSD_WIDTH), row), pl.BlockSpec((u, TILE, GLA_WIDTH), row)]
    out_shape = [jax.ShapeDtypeStruct((n_seq, seq_len, SSD_WIDTH), F32),
                 jax.ShapeDtypeStruct((n_seq, seq_len, GLA_WIDTH), F32)]
    aliases = {}
    if is_ctx:
        in_specs += [pl.BlockSpec(memory_space=pl.ANY), pl.BlockSpec(memory_space=pl.ANY)]
        out_specs += _state_specs(l, 1, u)
        out_shape += [jax.ShapeDtypeStruct(a.shape, a.dtype) for a in states]
        aliases = {len(args): len(out_shape) - 2, len(args) + 1: len(out_shape) - 1}
    else:
        in_specs += _state_specs(l, 1, u)
    args += list(states)
    return pl.pallas_call(
        functools.partial(_mix_bwd_kernel, is_ctx=is_ctx, n_tiles=n_tiles),
        grid=(n_seq // u, n_tiles),
        in_specs=in_specs,
        out_specs=out_specs,
        out_shape=out_shape,
        input_output_aliases=aliases,
        scratch_shapes=[pltpu.VMEM((u,) + HT_SHAPE, F32), pltpu.VMEM((u,) + S_SHAPE, F32)],
        compiler_params=pltpu.CompilerParams(dimension_semantics=("arbitrary", "arbitrary"),
                                             vmem_limit_bytes=VMEM_LIMIT_BYTES),
        name="mix_bwd_ctx" if is_ctx else "mix_bwd_lat",
    )(*args)


def _mix_fwd_kernel(*refs, is_ctx, n_tiles, layer):
    it = iter(refs)
    sink_ref = next(it)
    x_ref = next(it)
    gate_ref = next(it)
    rest_ref = next(it)
    xs_ref = next(it)
    bc_ref = next(it)
    yb_ref = next(it)
    ob_ref = next(it)
    cp_ref = next(it)
    wout_ref = next(it)
    if is_ctx:
        for _ in range(4):
            next(it)
    else:
        ht0_ref = next(it)
        s0_ref = next(it)
        kvprev_ref = next(it)
        kvnext_ref = next(it)
        cs_ref = next(it)
        csp_ref = next(it)
        csn_ref = next(it)
        kc_ref = next(it)
        vc_ref = next(it)
    o_ref = next(it)
    if is_ctx:
        kout_ref = next(it)
        vout_ref = next(it)
        htf_ref = next(it)
        sf_ref = next(it)
    ht_ref = next(it)
    s_ref = next(it)

    tile_idx = pl.program_id(1)

    @pl.when(tile_idx == 0)
    def _():
        for u in range(SEQ_PER_STEP):
            if is_ctx:
                ht_ref[u] = jnp.zeros(HT_SHAPE, F32)
                s_ref[u] = jnp.zeros(S_SHAPE, F32)
            else:
                _load_states(ht_ref.at[u], s_ref.at[u], ht0_ref.at[u], s0_ref.at[u])

    tasks = []
    for u in range(SEQ_PER_STEP):
        rest_u = rest_ref.at[u]
        tasks.append(_fwd_ssd(xs_ref[u], bc_ref[u], yb_ref[u], rest_u, cp_ref, ht_ref.at[u]))
        tasks.append(_fwd_gla(ob_ref[u], rest_u, cp_ref, s_ref.at[u]))
        tasks.append(_fwd_attention(
            u, tile_idx, sink_ref, rest_u, cp_ref,
            (kout_ref, vout_ref) if is_ctx else
            (kvprev_ref.at[u], kvnext_ref.at[u], cs_ref, csp_ref, csn_ref, kc_ref, vc_ref),
            is_ctx=is_ctx, n_tiles=n_tiles, layer=layer))
    parts = _run_interleaved(tasks)
    mixed = [jnp.concatenate(parts[3 * u:3 * u + 3], axis=1).astype(BF16) for u in range(SEQ_PER_STEP)]
    y = _dot(jnp.concatenate(mixed, axis=0), wout_ref[...])
    for u in range(SEQ_PER_STEP):
        gate = gate_ref[...] if is_ctx else gate_ref[u]
        o_ref[u] = x_ref[u] + gate * y[u * TILE:(u + 1) * TILE]
        if is_ctx:
            _store_states(ht_ref.at[u], s_ref.at[u], htf_ref.at[u], sf_ref.at[u])


def _fwd_ssd(xs, bc, yb, rest_ref, cp_ref, ht_ref):
    a_small_row = -jnp.exp(cp_ref[CP_ALOG:CP_ALOG + 1, :LANES])
    y_f = yield from _ssd_direction(xs, bc, rest_ref[:, P_DT:P_DT + LANES], a_small_row, ht_ref, False, 0)
    y = y_f + yb + xs * cp_ref[CP_DSKIP:CP_DSKIP + 1, :SSD_WIDTH]
    y = y * _silu(rest_ref[:, P_Z:P_Z + SSD_WIDTH])
    yield
    return y * lax.rsqrt(jnp.mean(y * y, axis=-1, keepdims=True) + EPS) * cp_ref[CP_SSDNW:CP_SSDNW + 1, :SSD_WIDTH]


def _fwd_gla(ob, rest_ref, cp_ref, s_ref):
    o_f = yield from _gla_direction(*_gla_inputs(rest_ref, P_GLF), s_ref, False)
    o = o_f + ob
    yield
    return _head_rms(o, cp_ref[CP_GLANW:CP_GLANW + 1, :GLA_WIDTH]) * _silu(rest_ref[:, P_GG:P_GG + GLA_WIDTH])


def _fwd_attention(u, tile_idx, sink_ref, rest_ref, cp_ref, extra, *, is_ctx, n_tiles, layer):
    if is_ctx:
        kout_ref, vout_ref = extra
    else:
        kvprev_ref, kvnext_ref, cs_ref, csp_ref, csn_ref, kc_ref, vc_ref = extra
    t = TILE
    knw = cp_ref[CP_KNW:CP_KNW + 1, :ATT_KV_DIM]
    qn = _head_rms(rest_ref[:, P_AQ:P_AQ + ATT_WIDTH], cp_ref[CP_QNW:CP_QNW + 1, :ATT_WIDTH])
    yield
    kn = _head_rms(rest_ref[:, P_AK:P_AK + ATT_KV_DIM], knw)
    vv = rest_ref[:, P_AV:P_AV + ATT_KV_DIM]
    yield
    qscale = (HEAD_DIM ** -0.5) * LOG2E
    blk = ATT_BLOCK
    n_blk = t // blk
    sink2 = jnp.concatenate([jnp.full((1, blk), sink_ref[layer, h] * LOG2E, F32) for h in range(ATT_HEADS)], axis=1)
    blocks = []
    if is_ctx:
        kout_ref[u] = kn
        vout_ref[u] = vv
        qs = qn * qscale
        kb = kn.astype(BF16)
        v_t = vv.T.astype(BF16)
        yield
        for nb in range(n_blk):
            blocks.append((yield from _attend_block(qs[nb * blk:(nb + 1) * blk], kb, v_t, sink2, None)))
    else:
        cos = cs_ref[:, :ATT_KV_DIM]
        sin = cs_ref[:, ATT_KV_DIM:]
        qr = _rope(qn, jnp.concatenate([cos, cos], axis=1), jnp.concatenate([sin, sin], axis=1)) * qscale
        yield
        k_halo_p = _rope(_head_rms(kvprev_ref[:, :ATT_KV_DIM], knw), csp_ref[:, :ATT_KV_DIM], csp_ref[:, ATT_KV_DIM:])
        k_halo_n = _rope(_head_rms(kvnext_ref[:, :ATT_KV_DIM], knw), csn_ref[:, :ATT_KV_DIM], csn_ref[:, ATT_KV_DIM:])
        yield
        k_loc = jnp.concatenate([k_halo_p, _rope(kn, cos, sin), k_halo_n], axis=0).astype(BF16)
        kc = kc_ref[u].astype(BF16)
        n_ctx = kc.shape[0]
        yield
        v_all_t = jnp.concatenate([vc_ref[u], kvprev_ref[:, ATT_KV_DIM:], vv, kvnext_ref[:, ATT_KV_DIM:]],
                                  axis=0).T.astype(BF16)
        yield
        assert WINDOW == ATT_BLOCK
        kj = _iota2((blk, ATT_HEADS * blk), 0)
        qi = _iota2((blk, ATT_HEADS * blk), 1) & (blk - 1)
        for nb in range(n_blk):
            kcat = jnp.concatenate([kc, k_loc[nb * blk:(nb + 3) * blk]], axis=0)
            vt = jnp.concatenate([v_all_t[:, :n_ctx], v_all_t[:, n_ctx + nb * blk:n_ctx + (nb + 3) * blk]], axis=1)
            off_p = jnp.where(tile_idx > 0, 0, 2 * blk) if nb == 0 else 0
            off_n = jnp.where(tile_idx < n_tiles - 1, 0, 2 * blk) if nb == n_blk - 1 else 0
            masks = (n_ctx, kj >= qi + off_p, kj <= qi - off_n)
            blocks.append((yield from _attend_block(qr[nb * blk:(nb + 1) * blk], kcat, vt, sink2, masks)))
    return jnp.concatenate(blocks, axis=0) * _silu(rest_ref[:, P_AG:P_AG + ATT_WIDTH])


def _mix_fwd(x, mod5, gate_row, xs, bc, rest, yb, ob, cpack, sink, w_out, l, extra, *, n_seq, n_tiles, is_ctx):
    u = SEQ_PER_STEP
    seq_len = n_tiles * TILE
    row = lambda s, t: (s, t, 0)
    rest3 = _per_seq(rest, n_seq)
    if is_ctx:
        gate_spec = _mod_spec(l, 2, lambda s, t: gate_row(s))
    else:
        gate_spec = pl.BlockSpec((None, u, None, 1, D_MODEL), lambda s, t: (l, s, 2, 0, 0))
    in_specs = [
        pl.BlockSpec(memory_space=pltpu.SMEM),
        pl.BlockSpec((u, TILE, D_MODEL), row),
        gate_spec,
        pl.BlockSpec((u, TILE, D_REST), row),
        pl.BlockSpec((u, TILE, SSD_WIDTH), row),
        pl.BlockSpec((u, TILE, 2 * SSD_BC), row),
        pl.BlockSpec((u, TILE, SSD_WIDTH), row),
        pl.BlockSpec((u, TILE, GLA_WIDTH), row),
        pl.BlockSpec((None, CP_ROWS, D_MODEL), lambda s, t: (l, 0, 0)),
        pl.BlockSpec((None, D_MIX, D_MODEL), lambda s, t: (l, 0, 0)),
    ]
    args = [sink, _per_seq(x, n_seq), mod5, rest3, _per_seq(xs, n_seq), _per_seq(bc, n_seq), yb, ob, cpack, w_out]
    out_specs = [pl.BlockSpec((u, TILE, D_MODEL), row)]
    out_shape = [jax.ShapeDtypeStruct((n_seq, seq_len, D_MODEL), F32)]
    aliases = {}
    if is_ctx:
        kv_spec = pl.BlockSpec((u, None, TILE, ATT_KV_DIM), lambda s, t: (s, l, t, 0))
        in_specs += [pl.BlockSpec(memory_space=pl.ANY)] * 4
        out_specs += [kv_spec, kv_spec] + _state_specs(l, 0, u)
        out_shape += [jax.ShapeDtypeStruct(a.shape, a.dtype) for a in extra]
        aliases = {len(args) + i: 1 + i for i in range(4)}
        args += list(extra)
    else:
        state_ssd, state_gla, rope_cs, cache_k, cache_v = extra
        per_blk = TILE // ATT_BLOCK
        kv_col = P_AK // (2 * ATT_KV_DIM)
        last_blk = n_tiles * per_blk - 1
        prev_blk = lambda t: jnp.maximum(t * per_blk - 1, 0)
        next_blk = lambda t: jnp.minimum((t + 1) * per_blk, last_blk)
        n_ctx = cache_k.shape[2]
        ctx_spec = pl.BlockSpec((u, None, n_ctx, ATT_KV_DIM), lambda s, t: (s, l, 0, 0))
        in_specs += _state_specs(l, 0, u) + [
            pl.BlockSpec((u, ATT_BLOCK, 2 * ATT_KV_DIM), lambda s, t: (s, prev_blk(t), kv_col)),
            pl.BlockSpec((u, ATT_BLOCK, 2 * ATT_KV_DIM), lambda s, t: (s, next_blk(t), kv_col)),
            pl.BlockSpec((TILE, 2 * ATT_KV_DIM), lambda s, t: (t, 0)),
            pl.BlockSpec((ATT_BLOCK, 2 * ATT_KV_DIM), lambda s, t: (prev_blk(t), 0)),
            pl.BlockSpec((ATT_BLOCK, 2 * ATT_KV_DIM), lambda s, t: (next_blk(t), 0)),
            ctx_spec,
            ctx_spec,
        ]
        args += [state_ssd, state_gla, rest3, rest3, rope_cs, rope_cs, rope_cs, cache_k, cache_v]
    outs = pl.pallas_call(
        functools.partial(_mix_fwd_kernel, is_ctx=is_ctx, n_tiles=n_tiles, layer=l),
        grid=(n_seq // u, n_tiles),
        in_specs=in_specs,
        out_specs=out_specs,
        out_shape=out_shape,
        input_output_aliases=aliases,
        scratch_shapes=[pltpu.VMEM((u,) + HT_SHAPE, F32), pltpu.VMEM((u,) + S_SHAPE, F32)],
        compiler_params=pltpu.CompilerParams(dimension_semantics=("arbitrary", "arbitrary"),
                                             vmem_limit_bytes=VMEM_LIMIT_BYTES),
        name="mix_fwd_ctx" if is_ctx else "mix_fwd_lat",
    )(*args)
    return [outs[0].reshape(n_seq * seq_len, D_MODEL)] + list(outs[1:])


def _pad_cols(a, width):
    return jnp.pad(a, [(0, 0)] * (a.ndim - 1) + [(0, width - a.shape[-1])])


def _prep_w_in(w_in):
    sizes = (SSD_WIDTH, SSD_CONV_DIM, 2 * SSD_HEADS, GLA_KEY_DIM, GLA_KEY_DIM, GLA_WIDTH, GLA_WIDTH, 2 * GLA_LOWRANK,
             ATT_WIDTH, ATT_KV_DIM, ATT_KV_DIM, ATT_WIDTH)
    parts, start = [], 0
    for size in sizes:
        parts.append(w_in[..., start:start + size])
        start += size
    z, xbc, dt, gq, gk, gv, gg, glr, aq, ak, av, ag = parts
    cols = [xbc, _pad_cols(dt, LANES), gq, gk, _pad_cols(glr, LANES), gv, ak, av, z, gg, aq, ag]
    return jnp.concatenate(cols, axis=-1).astype(BF16)


def _pack_small_params(norm_w, conv_w, conv_b, ssd_a_log, ssd_dt_bias, ssd_d, ssd_norm_w, gla_gk_b, gla_norm_w,
                       attn_q_norm, attn_k_norm):
    depth = norm_w.shape[0]
    row = lambda a: _pad_cols(a.reshape(depth, 1, -1), D_MODEL)
    blank = lambda n: jnp.zeros((depth, n, D_MODEL), F32)
    rows = [
        _pad_cols(conv_w, D_MODEL), blank(CP_CONVB - SSD_CONV),
        row(conv_b), row(ssd_dt_bias), row(ssd_a_log), row(gla_gk_b[:, 0]), row(gla_gk_b[:, 1]),
        row(jnp.repeat(ssd_d, SSD_HEAD_DIM, axis=-1)), row(ssd_norm_w), row(jnp.tile(gla_norm_w, (1, GLA_HEADS))),
        row(jnp.tile(attn_q_norm, (1, ATT_HEADS))), row(jnp.tile(attn_k_norm, (1, ATT_KV_HEADS))), row(norm_w),
        blank(CP_ROWS - CP_NW - 1),
    ]
    return jnp.concatenate(rows, axis=1)


def _pad_gate_weights(gla_gk_up):
    depth = gla_gk_up.shape[0]
    w = jnp.zeros((depth, LANES, 2 * GLA_KEY_DIM), F32)
    for d in range(2):
        w = w.at[:, d * GLA_LOWRANK:(d + 1) * GLA_LOWRANK, d * GLA_KEY_DIM:(d + 1) * GLA_KEY_DIM].set(gla_gk_up[:, d])
    return w


def _rope_tables(seq_len):
    quarter = HEAD_DIM // 4
    rows = seq_len // GRID_W
    row_pos = jnp.repeat(jnp.arange(rows, dtype=F32), GRID_W)
    col_pos = jnp.tile(jnp.arange(GRID_W, dtype=F32), rows)
    inv = ROPE_THETA ** (-jnp.arange(quarter, dtype=F32) / quarter)
    ang_r = row_pos[:, None] * inv[None, :]
    ang_c = col_pos[:, None] * inv[None, :]
    cos = jnp.concatenate([jnp.cos(ang_r), jnp.cos(ang_r), jnp.cos(ang_c), jnp.cos(ang_c)], axis=1)
    sin = jnp.concatenate([-jnp.sin(ang_r), jnp.sin(ang_r), -jnp.sin(ang_c), jnp.sin(ang_c)], axis=1)
    return jnp.tile(cos, (1, ATT_KV_HEADS)), jnp.tile(sin, (1, ATT_KV_HEADS))


def kernel(x_prompt, x_sample, c, cache_k, cache_v, state_ssd, state_gla, c_ctx, w_ada, b_ada, norm_w, w_in, conv_w,
           conv_b, ssd_a_log, ssd_dt_bias, ssd_d, ssd_norm_w, gla_gk_up, gla_gk_b, gla_norm_w, attn_q_norm, attn_k_norm,
           attn_sink, w_out):
    batch, seq, _ = x_prompt.shape
    dec_batch, dec_seq, _ = x_sample.shape
    depth = w_in.shape[0]
    past = cache_k.shape[2]
    assert seq % TILE == 0 and dec_seq % TILE == 0
    assert (batch * seq) % IN_TILE == 0 and dec_seq % IN_TILE == 0

    rows = -(-(dec_batch + 1) // SUBLANES) * SUBLANES
    cond = jnp.zeros((rows, D_MODEL), F32).at[:dec_batch].set(c).at[dec_batch].set(c_ctx)
    mod = _modulation(cond, w_ada, b_ada)
    mod5 = mod.reshape(depth, rows, 3, 1, D_MODEL)

    w_in_p = _prep_w_in(w_in)
    w_out_b = w_out.astype(BF16)
    cpack = _pack_small_params(norm_w, conv_w, conv_b, ssd_a_log, ssd_dt_bias, ssd_d, ssd_norm_w, gla_gk_b,
                               gla_norm_w, attn_q_norm, attn_k_norm)
    gk_w = _pad_gate_weights(gla_gk_up)
    rope_cs = jnp.concatenate(_rope_tables(dec_seq), axis=1)
    cache_k2 = cache_k.reshape(dec_batch, depth, past, ATT_KV_DIM)
    cache_v2 = cache_v.reshape(dec_batch, depth, past, ATT_KV_DIM)

    x_ctx = x_prompt.reshape(batch * seq, D_MODEL)
    x_lat = x_sample.reshape(dec_batch * dec_seq, D_MODEL)
    ctx_tiles = seq // TILE
    lat_tiles = dec_seq // TILE
    lat_in_per_seq = dec_seq // IN_TILE

    new_k = jnp.zeros((batch, depth, seq, ATT_KV_DIM), F32)
    new_v = jnp.zeros((batch, depth, seq, ATT_KV_DIM), F32)
    new_ssd = jnp.zeros((batch, depth, 2) + SSD_STATE_SHAPE, F32)
    new_gla = jnp.zeros((batch, depth, 2) + GLA_STATE_SHAPE, F32)

    for l in range(depth):
        xs, bc, rest = _in_proj(x_ctx, mod5, cpack, w_in_p, gk_w, l, lambda i: dec_batch, seq)
        yb, ob, new_ssd, new_gla = _mix_bwd(xs, bc, rest, cpack, l, (new_ssd, new_gla), n_seq=batch,
                                            n_tiles=ctx_tiles, is_ctx=True)
        x_ctx, new_k, new_v, new_ssd, new_gla = _mix_fwd(
            x_ctx, mod5, lambda s: dec_batch, xs, bc, rest, yb, ob, cpack, attn_sink, w_out_b, l,
            (new_k, new_v, new_ssd, new_gla), n_seq=batch, n_tiles=ctx_tiles, is_ctx=True)

        xs, bc, rest = _in_proj(x_lat, mod5, cpack, w_in_p, gk_w, l, lambda i: i // lat_in_per_seq, dec_seq)
        yb, ob = _mix_bwd(xs, bc, rest, cpack, l, (state_ssd, state_gla), n_seq=dec_batch, n_tiles=lat_tiles,
                          is_ctx=False)
        (x_lat,) = _mix_fwd(x_lat, mod5, lambda s: s, xs, bc, rest, yb, ob, cpack, attn_sink, w_out_b, l,
                            (state_ssd, state_gla, rope_cs, cache_k2, cache_v2), n_seq=dec_batch, n_tiles=lat_tiles,
                            is_ctx=False)

    return (x_ctx.reshape(batch, seq, D_MODEL), x_lat.reshape(dec_batch, dec_seq, D_MODEL),
            new_k.reshape(batch, depth, seq, ATT_KV_HEADS, HEAD_DIM),
            new_v.reshape(batch, depth, seq, ATT_KV_HEADS, HEAD_DIM), new_ssd, new_gla)
```

```python
import functools
import math

import jax
import jax.numpy as jnp
from jax import lax
from jax.experimental import pallas as pl
from jax.experimental.pallas import tpu as pltpu

D_MODEL = 1024
GRID_W = 64
SSD_HEADS = 8
SSD_HEAD_DIM = 64
SSD_WIDTH = 512
SSD_GROUPS = 2
SSD_STATE = 128
SSD_BC = SSD_GROUPS * SSD_STATE
SSD_CONV = 5
SSD_CONV_DIM = SSD_WIDTH + 2 * SSD_BC
SSD_CHUNK = 128
GLA_HEADS = 4
GLA_DK = 32
GLA_DV = 64
GLA_KEY_DIM = GLA_HEADS * GLA_DK
GLA_WIDTH = GLA_HEADS * GLA_DV
GLA_LOWRANK = 16
GLA_GATE_NORM = 16.0
GLA_CHUNK = 64
ATT_HEADS = 4
ATT_KV_HEADS = 2
ATT_GROUP = ATT_HEADS // ATT_KV_HEADS
HEAD_DIM = 64
ATT_WIDTH = ATT_HEADS * HEAD_DIM
ATT_KV_DIM = ATT_KV_HEADS * HEAD_DIM
WINDOW = 128
ATT_BLOCK = 128
ROPE_THETA = 10000.0
D_MIX = SSD_WIDTH + GLA_WIDTH + ATT_WIDTH
EPS = 1e-6
NEG_INF = -1e30
LOG2E = math.log2(math.e)

LANES = 128
SUBLANES = 8
VMEM_LIMIT_BYTES = 56 * 1024 * 1024

P_DT = 0
P_GQ = P_DT + LANES
P_GK = P_GQ + GLA_KEY_DIM
P_GLR = P_GK + GLA_KEY_DIM
P_GV = P_GLR + LANES
P_AK = P_GV + GLA_WIDTH
P_AV = P_AK + ATT_KV_DIM
P_Z = P_AV + ATT_KV_DIM
P_GG = P_Z + SSD_WIDTH
P_AQ = P_GG + GLA_WIDTH
P_AG = P_AQ + ATT_WIDTH
D_LIN = P_AG + ATT_WIDTH
D_PROJ = SSD_CONV_DIM + D_LIN
P_GLF = D_LIN
D_REST = D_LIN + GLA_KEY_DIM
BWD_COLS = P_AK

CP_CONVW = 0
CP_CONVB = 8
CP_DTB = 9
CP_ALOG = 10
CP_GKB = 11
CP_DSKIP = 13
CP_SSDNW = 14
CP_GLANW = 15
CP_QNW = 16
CP_KNW = 17
CP_NW = 18
CP_ROWS = 24

TILE = 256
IN_TILE = 512
HALO = SUBLANES
CONV_COLS = 256
CONV_ROWS = 64
SEQ_PER_STEP = 2
BWD_SEQ_PER_STEP = 4

HI = lax.Precision.HIGHEST
F32 = jnp.float32
BF16 = jnp.bfloat16

NT_DIMS = (((1,), (1,)), ((), ()))
TN_DIMS = (((0,), (0,)), ((), ()))


def _dot(a, b):
    return jnp.dot(a, b, preferred_element_type=F32)


def _dot_nt(a, b):
    return lax.dot_general(a, b, NT_DIMS, preferred_element_type=F32)


def _dot_tn(a, b):
    return lax.dot_general(a, b, TN_DIMS, preferred_element_type=F32)


def _split(x, parts):
    out = []
    for _ in range(parts - 1):
        piece = x.astype(BF16)
        out.append(piece)
        x = x - piece.astype(F32)
    out.append(x.astype(BF16))
    return out


def _dot_split_lhs(x, w_bf16, parts):
    acc = None
    for piece in _split(x, parts):
        term = _dot(piece, w_bf16)
        acc = term if acc is None else acc + term
    return acc


def _dot_split_rhs(w_bf16, x, parts):
    acc = None
    for piece in _split(x, parts):
        term = _dot(w_bf16, piece)
        acc = term if acc is None else acc + term
    return acc


def _silu(x):
    half = 0.5 * x
    return half + half * jnp.tanh(half)


def _softplus(x):
    return jnp.maximum(x, 0.0) + jnp.log(1.0 + jnp.exp(-jnp.abs(x)))


def _log_sigmoid(x):
    return jnp.minimum(x, 0.0) - jnp.log(1.0 + jnp.exp(-jnp.abs(x)))


def _iota2(shape, dim):
    return lax.broadcasted_iota(jnp.int32, shape, dim)


def _log2(n):
    assert n > 0 and n & (n - 1) == 0
    return n.bit_length() - 1


def _run_interleaved(tasks):
    results = [None] * len(tasks)
    live = list(range(len(tasks)))
    while live:
        for i in list(live):
            try:
                next(tasks[i])
            except StopIteration as done:
                results[i] = done.value
                live.remove(i)
    return results


def _mod_kernel(cond_ref, w_ref, b_ref, o_ref):
    o_ref[...] = jnp.dot(_silu(cond_ref[...]), w_ref[...], preferred_element_type=F32, precision=HI) + b_ref[...]


def _modulation(cond, w_ada, b_ada):
    depth = w_ada.shape[0]
    rows = cond.shape[0]
    return pl.pallas_call(
        _mod_kernel,
        grid=(depth, 3),
        in_specs=[
            pl.BlockSpec((rows, D_MODEL), lambda l, j: (0, 0)),
            pl.BlockSpec((None, D_MODEL, D_MODEL), lambda l, j: (l, 0, j)),
            pl.BlockSpec((None, 1, D_MODEL), lambda l, j: (l, 0, j)),
        ],
        out_specs=pl.BlockSpec((None, rows, D_MODEL), lambda l, j: (l, 0, j)),
        out_shape=jax.ShapeDtypeStruct((depth, rows, 3 * D_MODEL), F32),
        compiler_params=pltpu.CompilerParams(dimension_semantics=("arbitrary", "arbitrary")),
        name="modulation",
    )(cond, w_ada, b_ada.reshape(depth, 1, 3 * D_MODEL))


def _in_proj_kernel(x_ref, xprev_ref, xnext_ref, shift_ref, scale_ref, cp_ref, w_ref, gkw_ref, xs_ref, bc_ref,
                    rest_ref, *, seq_len):
    gain = cp_ref[CP_NW:CP_NW + 1, :] * (1.0 + scale_ref[...])
    shift = shift_ref[...]

    def modnorm(x):
        xn = x * lax.rsqrt(jnp.mean(x * x, axis=-1, keepdims=True) + EPS)
        return (xn * gain + shift).astype(BF16)

    h = modnorm(x_ref[...])
    t = h.shape[0]
    h_up = jnp.concatenate([modnorm(xprev_ref[...]), h, modnorm(xnext_ref[...])], axis=0)
    seg = min(seq_len, t)
    pos0 = (pl.program_id(0) * t) & (seq_len - 1)
    zero_halo = jnp.zeros((HALO, CONV_COLS), F32)

    for c0 in range(0, SSD_CONV_DIM, CONV_COLS):
        up = _dot(h_up, w_ref[:, c0:c0 + CONV_COLS])
        conv_w = cp_ref[CP_CONVW:CP_CONVW + SUBLANES, c0:c0 + CONV_COLS]
        conv_b = cp_ref[CP_CONVB:CP_CONVB + 1, c0:c0 + CONV_COLS]
        for a in range(0, t, CONV_ROWS):
            if a % seq_len == 0 and seq_len <= t:
                prev = zero_halo
            elif a % seg == 0:
                prev = jnp.where(pos0 + a > 0, up[a:a + HALO, :], 0.0)
            else:
                prev = up[a:a + HALO, :]
            b = a + CONV_ROWS
            if b % seq_len == 0 and seq_len <= t:
                nxt = zero_halo
            elif b % seg == 0:
                nxt = jnp.where(pos0 + b < seq_len, up[HALO + b:2 * HALO + b, :], 0.0)
            else:
                nxt = up[HALO + b:2 * HALO + b, :]
            xbc = _conv_silu(up[HALO + a:HALO + b, :], prev, nxt, conv_w, conv_b)
            if c0 < SSD_WIDTH:
                xs_ref[a:b, c0:c0 + CONV_COLS] = xbc
            else:
                bc_ref[a:b, c0 - SSD_WIDTH:c0 - SSD_WIDTH + CONV_COLS] = xbc.astype(BF16)

    res = _dot(h, w_ref[:, SSD_CONV_DIM:])
    rest_ref[:, P_DT:P_DT + LANES] = _softplus(res[:, P_DT:P_DT + LANES] + cp_ref[CP_DTB:CP_DTB + 1, :LANES])
    rest_ref[:, P_GQ:P_GLR] = res[:, P_GQ:P_GLR]
    rest_ref[:, P_GV:D_LIN] = res[:, P_GV:]
    lr = res[:, P_GLR:P_GLR + LANES].astype(BF16)
    for d, col in ((0, P_GLF), (1, P_GLR)):
        gate = (_dot(lr, gkw_ref[:, d * GLA_KEY_DIM:(d + 1) * GLA_KEY_DIM].astype(BF16))
                + cp_ref[CP_GKB + d:CP_GKB + d + 1, :GLA_KEY_DIM])
        rest_ref[:, col:col + GLA_KEY_DIM] = _log_sigmoid(gate) * (1.0 / GLA_GATE_NORM)


def _mod_spec(l, part, row_of):
    return pl.BlockSpec((None, None, None, 1, D_MODEL), lambda *ids: (l, row_of(*ids), part, 0, 0))


def _in_proj(x, mod5, cpack, w_in_p, gk_w, l, mod_row, seq_len):
    n = x.shape[0]
    per = IN_TILE // HALO
    last_halo = n // HALO - 1
    assert seq_len & (seq_len - 1) == 0
    return pl.pallas_call(
        functools.partial(_in_proj_kernel, seq_len=seq_len),
        grid=(n // IN_TILE,),
        in_specs=[
            pl.BlockSpec((IN_TILE, D_MODEL), lambda i: (i, 0)),
            pl.BlockSpec((HALO, D_MODEL), lambda i: (jnp.maximum(i * per - 1, 0), 0)),
            pl.BlockSpec((HALO, D_MODEL), lambda i: (jnp.minimum((i + 1) * per, last_halo), 0)),
            _mod_spec(l, 0, mod_row),
            _mod_spec(l, 1, mod_row),
            pl.BlockSpec((None, CP_ROWS, D_MODEL), lambda i: (l, 0, 0)),
            pl.BlockSpec((None, D_MODEL, D_PROJ), lambda i: (l, 0, 0)),
            pl.BlockSpec((None, LANES, 2 * GLA_KEY_DIM), lambda i: (l, 0, 0)),
        ],
        out_specs=[pl.BlockSpec((IN_TILE, SSD_WIDTH), lambda i: (i, 0)),
                   pl.BlockSpec((IN_TILE, 2 * SSD_BC), lambda i: (i, 0)),
                   pl.BlockSpec((IN_TILE, D_REST), lambda i: (i, 0))],
        out_shape=[jax.ShapeDtypeStruct((n, SSD_WIDTH), F32), jax.ShapeDtypeStruct((n, 2 * SSD_BC), BF16),
                   jax.ShapeDtypeStruct((n, D_REST), F32)],
        compiler_params=pltpu.CompilerParams(dimension_semantics=("arbitrary",),
                                             vmem_limit_bytes=VMEM_LIMIT_BYTES),
        name="in_proj",
    )(x, x, x, mod5, mod5, cpack, w_in_p, gk_w)


def _conv_silu(main, prev, nxt, conv_w, conv_b):
    t = main.shape[0]
    up = jnp.concatenate([prev, main, nxt], axis=0)
    rows = t + 2 * HALO
    pad = (SSD_CONV - 1) // 2
    acc = conv_b
    for k in range(SSD_CONV):
        off = k - pad
        shifted = up if off == 0 else pltpu.roll(up, (rows - off) % rows, 0)
        acc = acc + conv_w[k:k + 1, :] * shifted[HALO:HALO + t, :]
    return _silu(acc)


def _ssd_direction(xs, bc, dt_small, a_small_row, ht_ref, rev, direction):
    t = xs.shape[0]
    q = SSD_CHUNK
    hpg = SSD_HEADS // SSD_GROUPS
    gw = hpg * SSD_HEAD_DIM
    er = _iota2((LANES, SSD_WIDTH), 0)
    ec = _iota2((LANES, SSD_WIDTH), 1)
    expand = (er == (ec >> _log2(SSD_HEAD_DIM)) + SSD_HEADS * direction).astype(BF16)
    a_small = dt_small * a_small_row
    ri = _iota2((q, q), 0)
    ci = _iota2((q, q), 1)
    tri_incl = (ri >= ci).astype(BF16)
    keep = (ri <= ci) if rev else (ri >= ci)
    lane_lo = _iota2((q, LANES), 1) < SSD_HEAD_DIM
    ys = [None] * (t // q)
    order = range(t // q - 1, -1, -1) if rev else range(t // q)
    for c in order:
        sl = slice(c * q, (c + 1) * q)
        a_s = a_small[sl]
        dts = dt_small[sl]
        cs_s = _dot_split_rhs(tri_incl, a_s, 2)
        yield
        tot_s = cs_s[q - 1:q, :]
        pos_s = (cs_s - a_s) if rev else cs_s
        if rev:
            fac_s = jnp.exp(tot_s - pos_s)
            wgt_s = jnp.exp(pos_s)
        else:
            fac_s = jnp.exp(pos_s)
            wgt_s = jnp.exp(tot_s - pos_s)
        yield
        dtw_exp = _dot((dts * wgt_s).astype(BF16), expand)
        dec_state = _dot_split_lhs(jnp.broadcast_to(jnp.exp(tot_s), (SUBLANES, LANES)), expand, 3)[0:1, :]
        yield
        xc = xs[sl]
        xb = xc.astype(BF16)
        xw = (xc * dtw_exp).astype(BF16)
        pos_t = pos_s.T
        dt_t = dts.T
        yield
        y_groups = []
        for g in range(SSD_GROUPS):
            bg = bc[sl, g * SSD_STATE:(g + 1) * SSD_STATE]
            cg = bc[sl, SSD_BC + g * SSD_STATE:SSD_BC + (g + 1) * SSD_STATE]
            cg32 = cg.astype(F32)
            gmat = _dot_nt(cg, bg)
            yield
            ht = ht_ref[g]
            htb = ht.astype(BF16)
            glanes = slice(g * gw, (g + 1) * gw)
            pairs = []
            for j in range(hpg // 2):
                plo = g * gw + j * LANES
                rhs = jnp.concatenate([xb[:, plo:plo + LANES], htb[:, j * LANES:(j + 1) * LANES]], axis=0)
                lhs = []
                for hh in range(2):
                    lane = SSD_HEADS * direction + g * hpg + 2 * j + hh
                    col = pos_s[:, lane:lane + 1]
                    row = pos_t[lane:lane + 1, :]
                    seg = (row - col) if rev else (col - row)
                    decay = jnp.where(keep, jnp.exp(seg), 0.0) * dt_t[lane:lane + 1, :]
                    lhs.append(jnp.concatenate([(gmat * decay).astype(BF16),
                                                (cg32 * fac_s[:, lane:lane + 1]).astype(BF16)], axis=1))
                res = _dot(jnp.concatenate(lhs, axis=0), rhs)
                pairs.append(jnp.where(lane_lo, res[:q], res[q:]))
                yield
            y_groups.append(jnp.concatenate(pairs, axis=1))
            ht_ref[g] = ht * dec_state[:, glanes] + _dot_tn(bg, xw[:, glanes])
            yield
        ys[c] = jnp.concatenate(y_groups, axis=1)
    return jnp.concatenate(ys, axis=0)


def _gla_direction(qh, kh, vh, glog, s_ref, rev):
    t = qh.shape[0]
    q = GLA_CHUNK
    pair = 2 * q
    ri = _iota2((t, t), 0)
    ci = _iota2((t, t), 1)
    order_ok = (ri <= ci) if rev else (ri >= ci)
    chunk_bits = _log2(q)
    prefix = (((ri >> chunk_bits) == (ci >> chunk_bits)) & order_ok).astype(BF16)
    gc = _dot_split_rhs(prefix, glog, 2)
    yield
    q_in = qh * jnp.exp(gc)
    k_in = kh * jnp.exp(-gc)
    yield
    pr = _iota2((pair, GLA_HEADS * pair), 0)
    pc = _iota2((pair, GLA_HEADS * pair), 1) & (pair - 1)
    causal = ((pr >> chunk_bits) == (pc >> chunk_bits)) & ((pc >= pr) if rev else (pc <= pr))
    dk_bits, dv_bits = _log2(GLA_DK), _log2(GLA_DV)
    khead = _iota2((GLA_KEY_DIM, pair), 0) >> dk_bits
    vhead_rows = _iota2((GLA_HEADS * pair, GLA_WIDTH), 0) >> _log2(pair)
    vhead_cols = _iota2((GLA_HEADS * pair, GLA_WIDTH), 1) >> dv_bits
    blockdiag = ((_iota2((GLA_KEY_DIM, GLA_WIDTH), 0) >> dk_bits)
                 == (_iota2((GLA_KEY_DIM, GLA_WIDTH), 1) >> dv_bits))
    outs = [None] * (t // q)
    pair_order = range(t // pair - 1, -1, -1) if rev else range(t // pair)
    for pi in pair_order:
        psl = slice(pi * pair, (pi + 1) * pair)
        k_t = k_in[psl].T
        gc_t = gc[psl].T
        yield
        kbd = jnp.concatenate([jnp.where(khead == h, k_t, 0.0) for h in range(GLA_HEADS)], axis=1).astype(BF16)
        att = _dot(q_in[psl].astype(BF16), kbd)
        yield
        att = jnp.where(causal, att, 0.0).astype(BF16)
        vp = vh[psl].astype(BF16)
        vbd = jnp.where(vhead_rows == vhead_cols, jnp.concatenate([vp] * GLA_HEADS, axis=0), 0.0)
        o_intra = _dot(att, vbd)
        yield
        chunk_order = (1, 0) if rev else (0, 1)
        for cc in chunk_order:
            c = 2 * pi + cc
            sl = slice(c * q, (c + 1) * q)
            edge = cc * q if rev else cc * q + q - 1
            egl_col = jnp.exp(gc_t[:, edge:edge + 1])
            st = s_ref[...]
            o_inter = _dot(q_in[sl].astype(BF16), st.astype(BF16))
            outs[c] = o_intra[cc * q:(cc + 1) * q] + o_inter
            kw_t = (k_t[:, cc * q:(cc + 1) * q] * egl_col).astype(BF16)
            upd = _dot(kw_t, vh[sl].astype(BF16))
            s_ref[...] = st * egl_col + jnp.where(blockdiag, upd, 0.0)
            yield
    return jnp.concatenate(outs, axis=0)


def _head_rms(x, w_row):
    n = x.shape[1]
    head_bits = _log2(HEAD_DIM)
    ones = ((_iota2((n, n), 0) >> head_bits) == (_iota2((n, n), 1) >> head_bits)).astype(BF16)
    ms = _dot((x * x).astype(BF16), ones) * (1.0 / HEAD_DIM)
    return x * lax.rsqrt(ms + EPS) * w_row


def _rope(x, cos, sin_signed):
    n = x.shape[1]
    quarter = HEAD_DIM // 4
    first = (_iota2((1, n), 1) & (2 * quarter - 1)) < quarter
    swapped = jnp.where(first, pltpu.roll(x, n - quarter, 1), pltpu.roll(x, quarter, 1))
    return x * cos + swapped * sin_signed


def _attend_block(q_blk, kcat, vt, sink2, edge_masks):
    assert ATT_GROUP == 2 and ATT_KV_HEADS == 2
    blk = q_blk.shape[0]
    lane = _iota2((blk, LANES), 1)
    pieces = []
    for kvh in range(ATT_KV_HEADS):
        qt = q_blk[:, kvh * LANES:(kvh + 1) * LANES]
        rolled = pltpu.roll(qt, HEAD_DIM, 1)
        own = (lane < HEAD_DIM) if kvh == 0 else (lane >= HEAD_DIM)
        for g in range(ATT_GROUP):
            pieces.append(jnp.where(own, qt if g == kvh else rolled, 0.0))
    q_all = jnp.concatenate(pieces, axis=0).astype(BF16)
    yield
    st = _dot_nt(kcat, q_all)
    yield
    if edge_masks is not None:
        row, keep_before, keep_after = edge_masks
        st = jnp.concatenate([st[:row],
                              jnp.where(keep_before, st[row:row + blk], NEG_INF),
                              st[row + blk:row + 2 * blk],
                              jnp.where(keep_after, st[row + 2 * blk:], NEG_INF)], axis=0)
    m = jnp.maximum(jnp.max(st, axis=0, keepdims=True), sink2)
    yield
    pb = jnp.exp2(st - m).astype(BF16)
    yield
    sink_term = jnp.exp2(sink2 - m)
    ones_rows = jnp.ones((2 * SUBLANES, vt.shape[1]), BF16)
    z = []
    for kvh in range(ATT_KV_HEADS):
        qcols = slice(kvh * ATT_GROUP * blk, (kvh + 1) * ATT_GROUP * blk)
        v_aug = jnp.concatenate([vt[kvh * HEAD_DIM:(kvh + 1) * HEAD_DIM], ones_rows], axis=0)
        ot = _dot(v_aug, pb[:, qcols])
        den = ot[HEAD_DIM:HEAD_DIM + 1] + sink_term[:, qcols]
        ot = ot[:HEAD_DIM] / den
        for g in range(ATT_GROUP):
            z.append(ot[:, g * blk:(g + 1) * blk])
        yield
    return jnp.concatenate(z, axis=0).T


def _gla_inputs(rest_ref, glog_col):
    qh = rest_ref[:, P_GQ:P_GQ + GLA_KEY_DIM] * (GLA_DK ** -0.5)
    kh = rest_ref[:, P_GK:P_GK + GLA_KEY_DIM]
    vh = rest_ref.at[:, P_GV:P_GV + GLA_WIDTH]
    return qh, kh, vh, rest_ref[:, glog_col:glog_col + GLA_KEY_DIM]


def _load_states(ht_ref, s_ref, ht0_ref, s0_ref):
    hpg = SSD_HEADS // SSD_GROUPS
    for g in range(SSD_GROUPS):
        blk = ht0_ref[g * hpg:(g + 1) * hpg].reshape(hpg * SSD_HEAD_DIM, SSD_STATE)
        ht_ref[g] = blk.T
    rows = []
    for h in range(GLA_HEADS):
        pieces = []
        if h > 0:
            pieces.append(jnp.zeros((GLA_DK, h * GLA_DV), F32))
        pieces.append(s0_ref[h])
        if h < GLA_HEADS - 1:
            pieces.append(jnp.zeros((GLA_DK, (GLA_HEADS - 1 - h) * GLA_DV), F32))
        rows.append(jnp.concatenate(pieces, axis=1))
    s_ref[...] = jnp.concatenate(rows, axis=0)


def _store_states(ht_ref, s_ref, htf_ref, sf_ref):
    hpg = SSD_HEADS // SSD_GROUPS
    for g in range(SSD_GROUPS):
        htf_ref[g * hpg:(g + 1) * hpg] = ht_ref[g].T.reshape(hpg, SSD_HEAD_DIM, SSD_STATE)
    st = s_ref[...]
    for h in range(GLA_HEADS):
        sf_ref[h] = st[h * GLA_DK:(h + 1) * GLA_DK, h * GLA_DV:(h + 1) * GLA_DV]


HT_SHAPE = (SSD_GROUPS, SSD_STATE, SSD_WIDTH // SSD_GROUPS)
S_SHAPE = (GLA_KEY_DIM, GLA_WIDTH)
SSD_STATE_SHAPE = (SSD_HEADS, SSD_HEAD_DIM, SSD_STATE)
GLA_STATE_SHAPE = (GLA_HEADS, GLA_DK, GLA_DV)


def _mix_bwd_kernel(*refs, is_ctx, n_tiles):
    it = iter(refs)
    xs_ref = next(it)
    bc_ref = next(it)
    rest_ref = next(it)
    cp_ref = next(it)
    if is_ctx:
        next(it)
        next(it)
    else:
        ht0_ref = next(it)
        s0_ref = next(it)
    yb_ref = next(it)
    ob_ref = next(it)
    if is_ctx:
        htf_ref = next(it)
        sf_ref = next(it)
    ht_ref = next(it)
    s_ref = next(it)
    n_u = xs_ref.shape[0]

    @pl.when(pl.program_id(1) == 0)
    def _():
        for u in range(n_u):
            if is_ctx:
                ht_ref[u] = jnp.zeros(HT_SHAPE, F32)
                s_ref[u] = jnp.zeros(S_SHAPE, F32)
            else:
                _load_states(ht_ref.at[u], s_ref.at[u], ht0_ref.at[u], s0_ref.at[u])

    a_small_row = -jnp.exp(cp_ref[CP_ALOG:CP_ALOG + 1, :LANES])
    tasks = []
    for u in range(n_u):
        rest_u = rest_ref.at[u]
        tasks.append(_ssd_direction(xs_ref.at[u], bc_ref.at[u], rest_u[:, P_DT:P_DT + LANES], a_small_row,
                                    ht_ref.at[u], True, 1))
        tasks.append(_gla_direction(*_gla_inputs(rest_u, P_GLR), s_ref.at[u], True))
    outs = _run_interleaved(tasks)
    for u in range(n_u):
        yb_ref[u] = outs[2 * u]
        ob_ref[u] = outs[2 * u + 1]
        if is_ctx:
            _store_states(ht_ref.at[u], s_ref.at[u], htf_ref.at[u], sf_ref.at[u])


def _state_specs(l, direction, u):
    return [pl.BlockSpec((u, None, None) + SSD_STATE_SHAPE, lambda s, t: (s, l, direction, 0, 0, 0)),
            pl.BlockSpec((u, None, None) + GLA_STATE_SHAPE, lambda s, t: (s, l, direction, 0, 0, 0))]


def _per_seq(a, n_seq):
    return a.reshape(n_seq, a.shape[0] // n_seq, a.shape[1])


def _mix_bwd(xs, bc, rest, cpack, l, states, *, n_seq, n_tiles, is_ctx):
    u = BWD_SEQ_PER_STEP
    assert n_seq % u == 0
    seq_len = n_tiles * TILE
    row = lambda s, t: (s, n_tiles - 1 - t, 0)
    in_specs = [
        pl.BlockSpec((u, TILE, SSD_WIDTH), row),
        pl.BlockSpec((u, TILE, 2 * SSD_BC), row),
        pl.BlockSpec((u, TILE, BWD_COLS), row),
        pl.BlockSpec((None, CP_ROWS, D_MODEL), lambda s, t: (l, 0, 0)),
    ]
    args = [_per_seq(xs, n_seq), _per_seq(bc, n_seq), _per_seq(rest, n_seq), cpack]
    out_specs = [pl.BlockSpec((u, TILE, SSD_WIDTH), row), pl.BlockSpec((u, TILE, GLA_WIDTH), row)]
    out_shape = [jax.ShapeDtypeStruct((n_seq, seq_len, SSD_WIDTH), F32),
                 jax.ShapeDtypeStruct((n_seq, seq_len, GLA_WIDTH), F32)]
    aliases = {}
    if is_ctx:
        in_specs += [pl.BlockSpec(memory_space=pl.ANY), pl.BlockSpec(memory_space=pl.ANY)]
        out_specs += _state_specs(l, 1, u)
        out_shape += [jax.ShapeDtypeStruct(a.shape, a.dtype) for a in states]
        aliases = {len(args): len(out_shape) - 2, len(args) + 1: len(out_shape) - 1}
    else:
        in_specs += _state_specs(l, 1, u)
    args += list(states)
    return pl.pallas_call(
        functools.partial(_mix_bwd_kernel, is_ctx=is_ctx, n_tiles=n_tiles),
        grid=(n_seq // u, n_tiles),
        in_specs=in_specs,
        out_specs=out_specs,
        out_shape=out_shape,
        input_output_aliases=aliases,
        scratch_shapes=[pltpu.VMEM((u,) + HT_SHAPE, F32), pltpu.VMEM((u,) + S_SHAPE, F32)],
        compiler_params=pltpu.CompilerParams(dimension_semantics=("arbitrary", "arbitrary"),
                                             vmem_limit_bytes=VMEM_LIMIT_BYTES),
        name="mix_bwd_ctx" if is_ctx else "mix_bwd_lat",
    )(*args)


def _mix_fwd_kernel(*refs, is_ctx, n_tiles, layer):
    it = iter(refs)
    sink_ref = next(it)
    x_ref = next(it)
    gate_ref = next(it)
    rest_ref = next(it)
    xs_ref = next(it)
    bc_ref = next(it)
    yb_ref = next(it)
    ob_ref = next(it)
    cp_ref = next(it)
    wout_ref = next(it)
    if is_ctx:
        for _ in range(4):
            next(it)
    else:
        ht0_ref = next(it)
        s0_ref = next(it)
        kvprev_ref = next(it)
        kvnext_ref = next(it)
        cs_ref = next(it)
        csp_ref = next(it)
        csn_ref = next(it)
        kc_ref = next(it)
        vc_ref = next(it)
    o_ref = next(it)
    if is_ctx:
        kout_ref = next(it)
        vout_ref = next(it)
        htf_ref = next(it)
        sf_ref = next(it)
    ht_ref = next(it)
    s_ref = next(it)

    tile_idx = pl.program_id(1)

    @pl.when(tile_idx == 0)
    def _():
        for u in range(SEQ_PER_STEP):
            if is_ctx:
                ht_ref[u] = jnp.zeros(HT_SHAPE, F32)
                s_ref[u] = jnp.zeros(S_SHAPE, F32)
            else:
                _load_states(ht_ref.at[u], s_ref.at[u], ht0_ref.at[u], s0_ref.at[u])

    tasks = []
    for u in range(SEQ_PER_STEP):
        rest_u = rest_ref.at[u]
        tasks.append(_fwd_ssd(xs_ref.at[u], bc_ref.at[u], yb_ref.at[u], rest_u, cp_ref, ht_ref.at[u]))
        tasks.append(_fwd_gla(ob_ref.at[u], rest_u, cp_ref, s_ref.at[u]))
        tasks.append(_fwd_attention(
            u, tile_idx, sink_ref, rest_u, cp_ref,
            (kout_ref, vout_ref) if is_ctx else
            (kvprev_ref.at[u], kvnext_ref.at[u], cs_ref, csp_ref, csn_ref, kc_ref, vc_ref),
            is_ctx=is_ctx, n_tiles=n_tiles, layer=layer))
    parts = _run_interleaved(tasks)
    mixed = [jnp.concatenate(parts[3 * u:3 * u + 3], axis=1).astype(BF16) for u in range(SEQ_PER_STEP)]
    y = _dot(jnp.concatenate(mixed, axis=0), wout_ref[...])
    for u in range(SEQ_PER_STEP):
        gate = gate_ref[...] if is_ctx else gate_ref[u]
        o_ref[u] = x_ref[u] + gate * y[u * TILE:(u + 1) * TILE]
        if is_ctx:
            _store_states(ht_ref.at[u], s_ref.at[u], htf_ref.at[u], sf_ref.at[u])


def _fwd_ssd(xs_ref, bc_ref, yb_ref, rest_ref, cp_ref, ht_ref):
    a_small_row = -jnp.exp(cp_ref[CP_ALOG:CP_ALOG + 1, :LANES])
    y_f = yield from _ssd_direction(xs_ref, bc_ref, rest_ref[:, P_DT:P_DT + LANES], a_small_row, ht_ref, False, 0)
    y = y_f + yb_ref[...] + xs_ref[...] * cp_ref[CP_DSKIP:CP_DSKIP + 1, :SSD_WIDTH]
    y = y * _silu(rest_ref[:, P_Z:P_Z + SSD_WIDTH])
    yield
    return y * lax.rsqrt(jnp.mean(y * y, axis=-1, keepdims=True) + EPS) * cp_ref[CP_SSDNW:CP_SSDNW + 1, :SSD_WIDTH]


def _fwd_gla(ob_ref, rest_ref, cp_ref, s_ref):
    o_f = yield from _gla_direction(*_gla_inputs(rest_ref, P_GLF), s_ref, False)
    o = o_f + ob_ref[...]
    yield
    return _head_rms(o, cp_ref[CP_GLANW:CP_GLANW + 1, :GLA_WIDTH]) * _silu(rest_ref[:, P_GG:P_GG + GLA_WIDTH])


def _fwd_attention(u, tile_idx, sink_ref, rest_ref, cp_ref, extra, *, is_ctx, n_tiles, layer):
    if is_ctx:
        kout_ref, vout_ref = extra
    else:
        kvprev_ref, kvnext_ref, cs_ref, csp_ref, csn_ref, kc_ref, vc_ref = extra
    t = TILE
    knw = cp_ref[CP_KNW:CP_KNW + 1, :ATT_KV_DIM]
    qn = _head_rms(rest_ref[:, P_AQ:P_AQ + ATT_WIDTH], cp_ref[CP_QNW:CP_QNW + 1, :ATT_WIDTH])
    yield
    kn = _head_rms(rest_ref[:, P_AK:P_AK + ATT_KV_DIM], knw)
    vv = rest_ref[:, P_AV:P_AV + ATT_KV_DIM]
    yield
    qscale = (HEAD_DIM ** -0.5) * LOG2E
    blk = ATT_BLOCK
    n_blk = t // blk
    sink2 = jnp.concatenate([jnp.full((1, blk), sink_ref[layer, h] * LOG2E, F32) for h in range(ATT_HEADS)], axis=1)
    blocks = []
    if is_ctx:
        kout_ref[u] = kn
        vout_ref[u] = vv
        qs = qn * qscale
        kb = kn.astype(BF16)
        v_t = vv.T.astype(BF16)
        yield
        for nb in range(n_blk):
            blocks.append((yield from _attend_block(qs[nb * blk:(nb + 1) * blk], kb, v_t, sink2, None)))
    else:
        cos = cs_ref[:, :ATT_KV_DIM]
        sin = cs_ref[:, ATT_KV_DIM:]
        qr = _rope(qn, jnp.concatenate([cos, cos], axis=1), jnp.concatenate([sin, sin], axis=1)) * qscale
        yield
        k_halo_p = _rope(_head_rms(kvprev_ref[:, :ATT_KV_DIM], knw), csp_ref[:, :ATT_KV_DIM], csp_ref[:, ATT_KV_DIM:])
        k_halo_n = _rope(_head_rms(kvnext_ref[:, :ATT_KV_DIM], knw), csn_ref[:, :ATT_KV_DIM], csn_ref[:, ATT_KV_DIM:])
        yield
        k_loc = jnp.concatenate([k_halo_p, _rope(kn, cos, sin), k_halo_n], axis=0).astype(BF16)
        kc = kc_ref[u].astype(BF16)
        n_ctx = kc.shape[0]
        yield
        v_all_t = jnp.concatenate([vc_ref[u], kvprev_ref[:, ATT_KV_DIM:], vv, kvnext_ref[:, ATT_KV_DIM:]],
                                  axis=0).T.astype(BF16)
        yield
        assert WINDOW == ATT_BLOCK
        kj = _iota2((blk, ATT_HEADS * blk), 0)
        qi = _iota2((blk, ATT_HEADS * blk), 1) & (blk - 1)
        for nb in range(n_blk):
            kcat = jnp.concatenate([kc, k_loc[nb * blk:(nb + 3) * blk]], axis=0)
            vt = jnp.concatenate([v_all_t[:, :n_ctx], v_all_t[:, n_ctx + nb * blk:n_ctx + (nb + 3) * blk]], axis=1)
            off_p = jnp.where(tile_idx > 0, 0, 2 * blk) if nb == 0 else 0
            off_n = jnp.where(tile_idx < n_tiles - 1, 0, 2 * blk) if nb == n_blk - 1 else 0
            masks = (n_ctx, kj >= qi + off_p, kj <= qi - off_n)
            blocks.append((yield from _attend_block(qr[nb * blk:(nb + 1) * blk], kcat, vt, sink2, masks)))
    return jnp.concatenate(blocks, axis=0) * _silu(rest_ref[:, P_AG:P_AG + ATT_WIDTH])


def _mix_fwd(x, mod5, gate_row, xs, bc, rest, yb, ob, cpack, sink, w_out, l, extra, *, n_seq, n_tiles, is_ctx):
    u = SEQ_PER_STEP
    seq_len = n_tiles * TILE
    row = lambda s, t: (s, t, 0)
    rest3 = _per_seq(rest, n_seq)
    if is_ctx:
        gate_spec = _mod_spec(l, 2, lambda s, t: gate_row(s))
    else:
        gate_spec = pl.BlockSpec((None, u, None, 1, D_MODEL), lambda s, t: (l, s, 2, 0, 0))
    in_specs = [
        pl.BlockSpec(memory_space=pltpu.SMEM),
        pl.BlockSpec((u, TILE, D_MODEL), row),
        gate_spec,
        pl.BlockSpec((u, TILE, D_REST), row),
        pl.BlockSpec((u, TILE, SSD_WIDTH), row),
        pl.BlockSpec((u, TILE, 2 * SSD_BC), row),
        pl.BlockSpec((u, TILE, SSD_WIDTH), row),
        pl.BlockSpec((u, TILE, GLA_WIDTH), row),
        pl.BlockSpec((None, CP_ROWS, D_MODEL), lambda s, t: (l, 0, 0)),
        pl.BlockSpec((None, D_MIX, D_MODEL), lambda s, t: (l, 0, 0)),
    ]
    args = [sink, _per_seq(x, n_seq), mod5, rest3, _per_seq(xs, n_seq), _per_seq(bc, n_seq), yb, ob, cpack, w_out]
    out_specs = [pl.BlockSpec((u, TILE, D_MODEL), row)]
    out_shape = [jax.ShapeDtypeStruct((n_seq, seq_len, D_MODEL), F32)]
    aliases = {}
    if is_ctx:
        kv_spec = pl.BlockSpec((u, None, TILE, ATT_KV_DIM), lambda s, t: (s, l, t, 0))
        in_specs += [pl.BlockSpec(memory_space=pl.ANY)] * 4
        out_specs += [kv_spec, kv_spec] + _state_specs(l, 0, u)
        out_shape += [jax.ShapeDtypeStruct(a.shape, a.dtype) for a in extra]
        aliases = {len(args) + i: 1 + i for i in range(4)}
        args += list(extra)
    else:
        state_ssd, state_gla, rope_cs, cache_k, cache_v = extra
        per_blk = TILE // ATT_BLOCK
        kv_col = P_AK // (2 * ATT_KV_DIM)
        last_blk = n_tiles * per_blk - 1
        prev_blk = lambda t: jnp.maximum(t * per_blk - 1, 0)
        next_blk = lambda t: jnp.minimum((t + 1) * per_blk, last_blk)
        n_ctx = cache_k.shape[2]
        ctx_spec = pl.BlockSpec((u, None, n_ctx, ATT_KV_DIM), lambda s, t: (s, l, 0, 0))
        in_specs += _state_specs(l, 0, u) + [
            pl.BlockSpec((u, ATT_BLOCK, 2 * ATT_KV_DIM), lambda s, t: (s, prev_blk(t), kv_col)),
            pl.BlockSpec((u, ATT_BLOCK, 2 * ATT_KV_DIM), lambda s, t: (s, next_blk(t), kv_col)),
            pl.BlockSpec((TILE, 2 * ATT_KV_DIM), lambda s, t: (t, 0)),
            pl.BlockSpec((ATT_BLOCK, 2 * ATT_KV_DIM), lambda s, t: (prev_blk(t), 0)),
            pl.BlockSpec((ATT_BLOCK, 2 * ATT_KV_DIM), lambda s, t: (next_blk(t), 0)),
            ctx_spec,
            ctx_spec,
        ]
        args += [state_ssd, state_gla, rest3, rest3, rope_cs, rope_cs, rope_cs, cache_k, cache_v]
    outs = pl.pallas_call(
        functools.partial(_mix_fwd_kernel, is_ctx=is_ctx, n_tiles=n_tiles, layer=l),
        grid=(n_seq // u, n_tiles),
        in_specs=in_specs,
        out_specs=out_specs,
        out_shape=out_shape,
        input_output_aliases=aliases,
        scratch_shapes=[pltpu.VMEM((u,) + HT_SHAPE, F32), pltpu.VMEM((u,) + S_SHAPE, F32)],
        compiler_params=pltpu.CompilerParams(dimension_semantics=("arbitrary", "arbitrary"),
                                             vmem_limit_bytes=VMEM_LIMIT_BYTES),
        name="mix_fwd_ctx" if is_ctx else "mix_fwd_lat",
    )(*args)
    return [outs[0].reshape(n_seq * seq_len, D_MODEL)] + list(outs[1:])


def _pad_cols(a, width):
    return jnp.pad(a, [(0, 0)] * (a.ndim - 1) + [(0, width - a.shape[-1])])


def _prep_w_in(w_in):
    sizes = (SSD_WIDTH, SSD_CONV_DIM, 2 * SSD_HEADS, GLA_KEY_DIM, GLA_KEY_DIM, GLA_WIDTH, GLA_WIDTH, 2 * GLA_LOWRANK,
             ATT_WIDTH, ATT_KV_DIM, ATT_KV_DIM, ATT_WIDTH)
    parts, start = [], 0
    for size in sizes:
        parts.append(w_in[..., start:start + size])
        start += size
    z, xbc, dt, gq, gk, gv, gg, glr, aq, ak, av, ag = parts
    cols = [xbc, _pad_cols(dt, LANES), gq, gk, _pad_cols(glr, LANES), gv, ak, av, z, gg, aq, ag]
    return jnp.concatenate(cols, axis=-1).astype(BF16)


def _pack_small_params(norm_w, conv_w, conv_b, ssd_a_log, ssd_dt_bias, ssd_d, ssd_norm_w, gla_gk_b, gla_norm_w,
                       attn_q_norm, attn_k_norm):
    depth = norm_w.shape[0]
    row = lambda a: _pad_cols(a.reshape(depth, 1, -1), D_MODEL)
    blank = lambda n: jnp.zeros((depth, n, D_MODEL), F32)
    rows = [
        _pad_cols(conv_w, D_MODEL), blank(CP_CONVB - SSD_CONV),
        row(conv_b), row(ssd_dt_bias), row(ssd_a_log), row(gla_gk_b[:, 0]), row(gla_gk_b[:, 1]),
        row(jnp.repeat(ssd_d, SSD_HEAD_DIM, axis=-1)), row(ssd_norm_w), row(jnp.tile(gla_norm_w, (1, GLA_HEADS))),
        row(jnp.tile(attn_q_norm, (1, ATT_HEADS))), row(jnp.tile(attn_k_norm, (1, ATT_KV_HEADS))), row(norm_w),
        blank(CP_ROWS - CP_NW - 1),
    ]
    return jnp.concatenate(rows, axis=1)


def _pad_gate_weights(gla_gk_up):
    depth = gla_gk_up.shape[0]
    w = jnp.zeros((depth, LANES, 2 * GLA_KEY_DIM), F32)
    for d in range(2):
        w = w.at[:, d * GLA_LOWRANK:(d + 1) * GLA_LOWRANK, d * GLA_KEY_DIM:(d + 1) * GLA_KEY_DIM].set(gla_gk_up[:, d])
    return w


def _rope_tables(seq_len):
    quarter = HEAD_DIM // 4
    rows = seq_len // GRID_W
    row_pos = jnp.repeat(jnp.arange(rows, dtype=F32), GRID_W)
    col_pos = jnp.tile(jnp.arange(GRID_W, dtype=F32), rows)
    inv = ROPE_THETA ** (-jnp.arange(quarter, dtype=F32) / quarter)
    ang_r = row_pos[:, None] * inv[None, :]
    ang_c = col_pos[:, None] * inv[None, :]
    cos = jnp.concatenate([jnp.cos(ang_r), jnp.cos(ang_r), jnp.cos(ang_c), jnp.cos(ang_c)], axis=1)
    sin = jnp.concatenate([-jnp.sin(ang_r), jnp.sin(ang_r), -jnp.sin(ang_c), jnp.sin(ang_c)], axis=1)
    return jnp.tile(cos, (1, ATT_KV_HEADS)), jnp.tile(sin, (1, ATT_KV_HEADS))


def kernel(x_prompt, x_sample, c, cache_k, cache_v, state_ssd, state_gla, c_ctx, w_ada, b_ada, norm_w, w_in, conv_w,
           conv_b, ssd_a_log, ssd_dt_bias, ssd_d, ssd_norm_w, gla_gk_up, gla_gk_b, gla_norm_w, attn_q_norm, attn_k_norm,
           attn_sink, w_out):
    batch, seq, _ = x_prompt.shape
    dec_batch, dec_seq, _ = x_sample.shape
    depth = w_in.shape[0]
    past = cache_k.shape[2]
    assert seq % TILE == 0 and dec_seq % TILE == 0
    assert (batch * seq) % IN_TILE == 0 and dec_seq % IN_TILE == 0

    rows = -(-(dec_batch + 1) // SUBLANES) * SUBLANES
    cond = jnp.zeros((rows, D_MODEL), F32).at[:dec_batch].set(c).at[dec_batch].set(c_ctx)
    mod = _modulation(cond, w_ada, b_ada)
    mod5 = mod.reshape(depth, rows, 3, 1, D_MODEL)

    w_in_p = _prep_w_in(w_in)
    w_out_b = w_out.astype(BF16)
    cpack = _pack_small_params(norm_w, conv_w, conv_b, ssd_a_log, ssd_dt_bias, ssd_d, ssd_norm_w, gla_gk_b,
                               gla_norm_w, attn_q_norm, attn_k_norm)
    gk_w = _pad_gate_weights(gla_gk_up)
    rope_cs = jnp.concatenate(_rope_tables(dec_seq), axis=1)
    cache_k2 = cache_k.reshape(dec_batch, depth, past, ATT_KV_DIM)
    cache_v2 = cache_v.reshape(dec_batch, depth, past, ATT_KV_DIM)

    x_ctx = x_prompt.reshape(batch * seq, D_MODEL)
    x_lat = x_sample.reshape(dec_batch * dec_seq, D_MODEL)
    ctx_tiles = seq // TILE
    lat_tiles = dec_seq // TILE
    lat_in_per_seq = dec_seq // IN_TILE

    new_k = jnp.zeros((batch, depth, seq, ATT_KV_DIM), F32)
    new_v = jnp.zeros((batch, depth, seq, ATT_KV_DIM), F32)
    new_ssd = jnp.zeros((batch, depth, 2) + SSD_STATE_SHAPE, F32)
    new_gla = jnp.zeros((batch, depth, 2) + GLA_STATE_SHAPE, F32)

    for l in range(depth):
        xs, bc, rest = _in_proj(x_ctx, mod5, cpack, w_in_p, gk_w, l, lambda i: dec_batch, seq)
        yb, ob, new_ssd, new_gla = _mix_bwd(xs, bc, rest, cpack, l, (new_ssd, new_gla), n_seq=batch,
                                            n_tiles=ctx_tiles, is_ctx=True)
        x_ctx, new_k, new_v, new_ssd, new_gla = _mix_fwd(
            x_ctx, mod5, lambda s: dec_batch, xs, bc, rest, yb, ob, cpack, attn_sink, w_out_b, l,
            (new_k, new_v, new_ssd, new_gla), n_seq=batch, n_tiles=ctx_tiles, is_ctx=True)

        xs, bc, rest = _in_proj(x_lat, mod5, cpack, w_in_p, gk_w, l, lambda i: i // lat_in_per_seq, dec_seq)
        yb, ob = _mix_bwd(xs, bc, rest, cpack, l, (state_ssd, state_gla), n_seq=dec_batch, n_tiles=lat_tiles,
                          is_ctx=False)
        (x_lat,) = _mix_fwd(x_lat, mod5, lambda s: s, xs, bc, rest, yb, ob, cpack, attn_sink, w_out_b, l,
                            (state_ssd, state_gla, rope_cs, cache_k2, cache_v2), n_seq=dec_batch, n_tiles=lat_tiles,
                            is_ctx=False)

    return (x_ctx.reshape(batch, seq, D_MODEL), x_lat.reshape(dec_batch, dec_seq, D_MODEL),
            new_k.reshape(batch, depth, seq, ATT_KV_HEADS, HEAD_DIM),
            new_v.reshape(batch, depth, seq, ATT_KV_HEADS, HEAD_DIM), new_ssd, new_gla)
```
